```python
import math
import numpy as np
import jax
import jax.numpy as jnp
from jax import lax

D_MODEL = 1024
BATCH = 16
SEQ = 2048
DEPTH = 2

HEAD_DIM = 64
MOBA_HEADS = 4
MOBA_BLOCK = 256
MOBA_TOPK = 3
MOBA_QCHUNK = 16
NSA_HEADS = 4
NSA_CMP_LEN = 32
NSA_CMP_STRIDE = 16
NSA_SEL_LEN = 64
NSA_SEL_TOPN = 16
NSA_WINDOW = 512
NSA_QCHUNK = 64
NSA_FORCE_BONUS = 1e4
WIN_QBLOCK = 128
GLA_HEADS = 4
GLA_DK = 64
GLA_DV = 128
GLA_GATE_RANK = 16
GLA_GATE_NORM = 16.0
GLA_CHUNK = 64
N_BRANCH = 3
D_FF = 2816
CONV_WIDTH = 3
REL_BUCKETS = 32
REL_MAX_DIST = 128
N_SOFTMAX_HEADS = MOBA_HEADS + NSA_HEADS

NORM_EPS = 1e-6
NEG_INF = -1e30

SPLIT_SIZES = (
    MOBA_HEADS * HEAD_DIM,
    MOBA_HEADS * HEAD_DIM,
    MOBA_HEADS * HEAD_DIM,
    NSA_HEADS * HEAD_DIM,
    6 * HEAD_DIM,
    NSA_HEADS * 3,
    GLA_HEADS * GLA_DK,
    GLA_HEADS * GLA_DK,
    GLA_HEADS * GLA_DV,
    GLA_GATE_RANK,
    GLA_HEADS * GLA_DV,
    N_BRANCH * D_MODEL,
)
D_IN = sum(SPLIT_SIZES)
SPLIT_POINTS = tuple(int(v) for v in np.cumsum(SPLIT_SIZES)[:-1])

kernel_name = "hybrid_moba_nsa_gla_convffn"


def rms_norm(x, gain):
    xf = x.astype(jnp.float32)
    y = xf * lax.rsqrt(jnp.mean(xf * xf, axis=-1, keepdims=True) + NORM_EPS)
    return (y * gain.astype(jnp.float32)).astype(x.dtype)


def masked_softmax(logits, mask):
    s = jnp.where(mask, logits.astype(jnp.float32), NEG_INF)
    return jax.nn.softmax(s, axis=-1) * mask.astype(jnp.float32)


def rel_bucket(dist):
    n = jnp.maximum(dist, 0)
    max_exact = REL_BUCKETS // 2
    nf = jnp.maximum(n, 1).astype(jnp.float32)
    large = max_exact + (jnp.log(nf / max_exact) / math.log(REL_MAX_DIST / max_exact)
                         * (REL_BUCKETS - max_exact)).astype(jnp.int32)
    large = jnp.minimum(large, REL_BUCKETS - 1)
    return jnp.where(n < max_exact, n, large)


def moba_attention(q, k, v, rel_tab):
    B, S, H, Dh = q.shape
    nb = -(-S // MOBA_BLOCK)
    s_pad = nb * MOBA_BLOCK
    pad = ((0, 0), (0, s_pad - S), (0, 0), (0, 0))
    q, k, v = (jnp.pad(a, pad).transpose(0, 2, 1, 3) for a in (q, k, v))
    kb = k.reshape(B, H, nb, MOBA_BLOCK, Dh)
    vb = v.reshape(B, H, nb, MOBA_BLOCK, Dh)
    k_mean = jnp.mean(kb.astype(jnp.float32), axis=3)
    qblk = jnp.arange(s_pad) // MOBA_BLOCK
    past = jnp.arange(nb)[None, :] < qblk[:, None]
    gate = jnp.where(past, jnp.einsum('bhsd,bhnd->bhsn', q.astype(jnp.float32), k_mean), NEG_INF)
    kk = min(MOBA_TOPK, nb)
    _, sel = lax.top_k(gate, kk)
    sel_valid = jnp.arange(kk)[None, :] < qblk[:, None]
    scale = Dh ** -0.5
    tab_t = rel_tab.T
    head_idx = jnp.arange(H)[None, :, None, None]
    gather_blocks = jax.vmap(jax.vmap(lambda blocks, ix: blocks[ix]))
    n_sel = kk * MOBA_BLOCK

    def chunk(c):
        start = c * MOBA_QCHUNK
        qc = lax.dynamic_slice_in_dim(q, start, MOBA_QCHUNK, axis=2)
        sc = lax.dynamic_slice_in_dim(sel, start, MOBA_QCHUNK, axis=2)
        valid = lax.dynamic_slice_in_dim(sel_valid, start, MOBA_QCHUNK, axis=0)
        tpos = start + jnp.arange(MOBA_QCHUNK)
        own = start // MOBA_BLOCK
        kg = gather_blocks(kb, sc).reshape(B, H, MOBA_QCHUNK, n_sel, Dh)
        vg = gather_blocks(vb, sc).reshape(B, H, MOBA_QCHUNK, n_sel, Dh)
        ko = lax.dynamic_index_in_dim(kb, own, axis=2, keepdims=False)
        vo = lax.dynamic_index_in_dim(vb, own, axis=2, keepdims=False)
        kpos_sel = (sc[..., None] * MOBA_BLOCK + jnp.arange(MOBA_BLOCK)).reshape(B, H, MOBA_QCHUNK, n_sel)
        dist_own = tpos[:, None] - (own * MOBA_BLOCK + jnp.arange(MOBA_BLOCK))[None, :]
        bias_sel = tab_t[head_idx, rel_bucket(tpos[None, None, :, None] - kpos_sel)]
        bias_own = rel_tab[rel_bucket(dist_own)].transpose(2, 0, 1)
        s_sel = jnp.einsum('bhqd,bhqkd->bhqk', qc, kg) * scale + bias_sel
        s_own = jnp.einsum('bhqd,bhkd->bhqk', qc, ko) * scale + bias_own
        mask = jnp.concatenate([jnp.repeat(valid, MOBA_BLOCK, axis=1), dist_own >= 0], axis=-1)
        p = masked_softmax(jnp.concatenate([s_sel, s_own], axis=-1), mask).astype(v.dtype)
        return (jnp.einsum('bhqk,bhqkd->bhqd', p[..., :n_sel], vg)
                + jnp.einsum('bhqk,bhkd->bhqd', p[..., n_sel:], vo))

    out = lax.map(chunk, jnp.arange(s_pad // MOBA_QCHUNK))
    out = out.transpose(1, 2, 0, 3, 4).reshape(B, H, s_pad, Dh).transpose(0, 2, 1, 3)
    return out[:, :S]


def compress_tokens(x, pos_emb, w1, w2):
    B, S, Dh = x.shape
    r = NSA_CMP_LEN // NSA_CMP_STRIDE
    n_cmp = S // NSA_CMP_STRIDE - r + 1
    xs = x.reshape(B, S // NSA_CMP_STRIDE, NSA_CMP_STRIDE, Dh)
    blocks = jnp.concatenate([xs[:, j:j + n_cmp] for j in range(r)], axis=2)
    h = (blocks + pos_emb).reshape(B, n_cmp, NSA_CMP_LEN * Dh)
    return jax.nn.gelu(h @ w1) @ w2


def overlap_matrix(n_cmp, n_blk):
    starts = np.arange(n_cmp) * NSA_CMP_STRIDE
    tok = np.arange(n_blk * NSA_SEL_LEN)
    inside = (tok[None, :] >= starts[:, None]) & (tok[None, :] < starts[:, None] + NSA_CMP_LEN)
    m = inside.reshape(n_cmp, n_blk, NSA_SEL_LEN).sum(-1) / NSA_CMP_LEN
    return jnp.asarray(m, jnp.float32)


def window_attention(q, k, v, rel_tab):
    B, S, H, Dh = q.shape
    nqb = S // WIN_QBLOCK
    nkb = NSA_WINDOW // WIN_QBLOCK
    kp = jnp.pad(k, ((0, 0), (NSA_WINDOW, 0), (0, 0))).reshape(B, nqb + nkb, WIN_QBLOCK, Dh)
    vp = jnp.pad(v, ((0, 0), (NSA_WINDOW, 0), (0, 0))).reshape(B, nqb + nkb, WIN_QBLOCK, Dh)
    kband = jnp.concatenate([kp[:, j:j + nqb] for j in range(nkb + 1)], axis=2)
    vband = jnp.concatenate([vp[:, j:j + nqb] for j in range(nkb + 1)], axis=2)
    qb = q.reshape(B, nqb, WIN_QBLOCK, H, Dh)
    qoff = jnp.arange(WIN_QBLOCK)
    koff = jnp.arange((nkb + 1) * WIN_QBLOCK) - NSA_WINDOW
    dist = qoff[:, None] - koff[None, :]
    kpos = jnp.arange(nqb)[:, None] * WIN_QBLOCK + koff[None, :]
    mask = ((dist >= 0) & (dist < NSA_WINDOW))[None] & (kpos >= 0)[:, None, :]
    bias = rel_tab[rel_bucket(dist)].transpose(2, 0, 1)
    s = jnp.einsum('bnqhd,bnkd->bhnqk', qb, kband) * (Dh ** -0.5) + bias[:, None]
    p = masked_softmax(s, mask).astype(v.dtype)
    return jnp.einsum('bhnqk,bnkd->bnqhd', p, vband).reshape(B, S, H, Dh)


def nsa_attention(q, kc, vc, k_slc, v_slc, k_win, v_win, gates, rel_tab):
    B, S, H, Dh = q.shape
    scale = Dh ** -0.5
    n_cmp = kc.shape[1]
    t = jnp.arange(S)
    cend = jnp.arange(n_cmp) * NSA_CMP_STRIDE + NSA_CMP_LEN - 1
    p_cmp = masked_softmax(jnp.einsum('bshd,bnd->bhsn', q, kc) * scale, cend[None, :] <= t[:, None])
    o_cmp = jnp.einsum('bhsn,bnd->bshd', p_cmp.astype(vc.dtype), vc)
    n_blk = S // NSA_SEL_LEN
    imp = jnp.einsum('bhsn,nj->bsj', p_cmp, overlap_matrix(n_cmp, n_blk))
    cur = t // NSA_SEL_LEN
    blk = jnp.arange(n_blk)[None, :]
    forced = (blk == 0) | (blk == cur[:, None]) | (blk == cur[:, None] - 1)
    imp = jnp.where(blk <= cur[:, None], imp + jnp.where(forced, NSA_FORCE_BONUS, 0.0), NEG_INF)
    n_top = min(NSA_SEL_TOPN, n_blk)
    _, sel = lax.top_k(imp, n_top)
    sel_valid = jnp.arange(n_top)[None, :] < (cur + 1)[:, None]
    ksb = k_slc.reshape(B, n_blk, NSA_SEL_LEN, Dh)
    vsb = v_slc.reshape(B, n_blk, NSA_SEL_LEN, Dh)
    gather_blocks = jax.vmap(lambda blocks, ix: blocks[ix])
    n_keys = n_top * NSA_SEL_LEN

    def chunk(c):
        start = c * NSA_QCHUNK
        qc = lax.dynamic_slice_in_dim(q, start, NSA_QCHUNK, axis=1)
        sc = lax.dynamic_slice_in_dim(sel, start, NSA_QCHUNK, axis=1)
        valid = lax.dynamic_slice_in_dim(sel_valid, start, NSA_QCHUNK, axis=0)
        tpos = start + jnp.arange(NSA_QCHUNK)
        kg = gather_blocks(ksb, sc).reshape(B, NSA_QCHUNK, n_keys, Dh)
        vg = gather_blocks(vsb, sc).reshape(B, NSA_QCHUNK, n_keys, Dh)
        kpos = (sc[..., None] * NSA_SEL_LEN + jnp.arange(NSA_SEL_LEN)).reshape(B, NSA_QCHUNK, n_keys)
        dist = tpos[None, :, None] - kpos
        bias = rel_tab[rel_bucket(dist)].transpose(0, 3, 1, 2)
        mask = (dist >= 0) & jnp.repeat(valid, NSA_SEL_LEN, axis=1)[None]
        s = jnp.einsum('bqhd,bqkd->bhqk', qc, kg) * scale + bias
        p = masked_softmax(s, mask[:, None]).astype(vg.dtype)
        return jnp.einsum('bhqk,bqkd->bqhd', p, vg)

    o_slc = lax.map(chunk, jnp.arange(S // NSA_QCHUNK))
    o_slc = o_slc.transpose(1, 0, 2, 3, 4).reshape(B, S, H, Dh)
    o_win = window_attention(q, k_win, v_win, rel_tab)
    return gates[..., 0:1] * o_cmp + gates[..., 1:2] * o_slc + gates[..., 2:3] * o_win


def gla_attention(q, k, v, log_a):
    B, S, H, Dk = q.shape
    Dv = v.shape[-1]
    n_chunk = S // GLA_CHUNK

    def to_chunks(a):
        return a.astype(jnp.float32).reshape(B, n_chunk, GLA_CHUNK, H, -1).transpose(1, 0, 3, 2, 4)

    qc, kc, vc, gc = (to_chunks(a) for a in (q * (Dk ** -0.5), k, v, log_a))
    causal = jnp.tril(jnp.ones((GLA_CHUNK, GLA_CHUNK), dtype=bool))

    def step(state, inp):
        qi, ki, vi, gi = inp
        b = jnp.cumsum(gi, axis=2)
        decay = jnp.exp(jnp.where(causal[:, :, None], b[:, :, :, None, :] - b[:, :, None, :, :], NEG_INF))
        a_intra = jnp.einsum('bhid,bhjd,bhijd->bhij', qi, ki, decay)
        o = (jnp.einsum('bhij,bhjv->bhiv', a_intra, vi)
             + jnp.einsum('bhid,bhdv->bhiv', qi * jnp.exp(b), state))
        b_last = b[:, :, -1:, :]
        new_state = (state * jnp.exp(b_last[:, :, 0, :, None])
                     + jnp.einsum('bhjd,bhjv->bhdv', ki * jnp.exp(b_last - b), vi))
        return new_state, o

    state0 = jnp.zeros((B, H, Dk, Dv), jnp.float32)
    _, o = lax.scan(step, state0, (qc, kc, vc, gc))
    return o.transpose(1, 0, 3, 2, 4).reshape(B, S, H, Dv)


def hybrid_mixer(h, rel_bias, w_in, moba_q_norm, moba_k_norm, nsa_q_norm, nsa_k_norm,
                 cmp_pos_k, cmp_pos_v, cmp_k_w1, cmp_k_w2, cmp_v_w1, cmp_v_w2,
                 gla_gate_w, gla_gate_b, gla_out_norm,
                 w_branch_moba, w_branch_nsa, w_branch_gla, w_out):
    B, S, _ = h.shape
    (mq, mk, mv, nq, nkv, ngate, gq, gk, gv, g_lr, g_out, merge) = jnp.split(h @ w_in, SPLIT_POINTS, axis=-1)

    def heads(a, n):
        return a.reshape(B, S, n, -1)

    o_moba = moba_attention(rms_norm(heads(mq, MOBA_HEADS), moba_q_norm),
                            rms_norm(heads(mk, MOBA_HEADS), moba_k_norm),
                            heads(mv, MOBA_HEADS), rel_bias[:, :MOBA_HEADS]).reshape(B, S, -1)
    k_c, v_c, k_s, v_s, k_w, v_w = jnp.split(nkv, 6, axis=-1)
    kc = rms_norm(compress_tokens(k_c, cmp_pos_k, cmp_k_w1, cmp_k_w2), nsa_k_norm[0])
    vc = compress_tokens(v_c, cmp_pos_v, cmp_v_w1, cmp_v_w2)
    o_nsa = nsa_attention(rms_norm(heads(nq, NSA_HEADS), nsa_q_norm), kc, vc,
                          rms_norm(k_s, nsa_k_norm[1]), v_s, rms_norm(k_w, nsa_k_norm[2]), v_w,
                          jax.nn.sigmoid(heads(ngate, NSA_HEADS)), rel_bias[:, MOBA_HEADS:]).reshape(B, S, -1)
    log_a = jax.nn.log_sigmoid((g_lr @ gla_gate_w + gla_gate_b).astype(jnp.float32)) / GLA_GATE_NORM
    o_gla = gla_attention(heads(gq, GLA_HEADS), heads(gk, GLA_HEADS), heads(gv, GLA_HEADS),
                          heads(log_a, GLA_HEADS)).astype(h.dtype)
    o_gla = (rms_norm(o_gla, gla_out_norm) * jax.nn.silu(heads(g_out, GLA_HEADS))).reshape(B, S, -1)
    g_a, g_b, g_c = jnp.split(jax.nn.sigmoid(merge), N_BRANCH, axis=-1)
    z = g_a * (o_moba @ w_branch_moba) + g_b * (o_nsa @ w_branch_nsa) + g_c * (o_gla @ w_branch_gla)
    return z @ w_out


def conv_ffn(h, w_up, conv_w, conv_b, w_down):
    a, g = jnp.split(h @ w_up, 2, axis=-1)
    a = lax.conv_general_dilated(a, conv_w[:, None, :].astype(a.dtype), (1,), [(CONV_WIDTH - 1, 0)],
                                 dimension_numbers=('NWC', 'WIO', 'NWC'), feature_group_count=D_FF) + conv_b
    return (jax.nn.gelu(a) * g) @ w_down


def setup_inputs(seed: int = 0) -> dict:
    key = jax.random.key(seed)
    ks = jax.random.split(key, 32)
    L = DEPTH
    qkv_w = MOBA_HEADS * HEAD_DIM

    def nrm(k, shape, scale):
        return jax.random.normal(k, shape, jnp.float32) * scale

    def gain(k, shape):
        return 1.0 + 0.02 * jax.random.normal(k, shape, jnp.float32)

    return {
        "x": nrm(ks[0], (BATCH, SEQ, D_MODEL), 1.0),
        "rel_bias": nrm(ks[1], (REL_BUCKETS, N_SOFTMAX_HEADS), 0.1),
        "attn_norm": gain(ks[2], (L, D_MODEL)),
        "w_in": nrm(ks[3], (L, D_MODEL, D_IN), D_MODEL ** -0.5),
        "moba_q_norm": gain(ks[4], (L, HEAD_DIM)),
        "moba_k_norm": gain(ks[5], (L, HEAD_DIM)),
        "nsa_q_norm": gain(ks[6], (L, HEAD_DIM)),
        "nsa_k_norm": gain(ks[7], (L, 3, HEAD_DIM)),
        "cmp_pos_k": nrm(ks[8], (L, NSA_CMP_LEN, HEAD_DIM), 0.1),
        "cmp_pos_v": nrm(ks[9], (L, NSA_CMP_LEN, HEAD_DIM), 0.1),
        "cmp_k_w1": nrm(ks[10], (L, NSA_CMP_LEN * HEAD_DIM, HEAD_DIM), (NSA_CMP_LEN * HEAD_DIM) ** -0.5),
        "cmp_k_w2": nrm(ks[11], (L, HEAD_DIM, HEAD_DIM), HEAD_DIM ** -0.5),
        "cmp_v_w1": nrm(ks[12], (L, NSA_CMP_LEN * HEAD_DIM, HEAD_DIM), (NSA_CMP_LEN * HEAD_DIM) ** -0.5),
        "cmp_v_w2": nrm(ks[13], (L, HEAD_DIM, HEAD_DIM), HEAD_DIM ** -0.5),
        "gla_gate_w": nrm(ks[14], (L, GLA_GATE_RANK, GLA_HEADS * GLA_DK), GLA_GATE_RANK ** -0.5),
        "gla_gate_b": nrm(ks[15], (L, GLA_HEADS * GLA_DK), 0.1),
        "gla_out_norm": gain(ks[16], (L, GLA_DV)),
        "w_branch_moba": nrm(ks[17], (L, qkv_w, D_MODEL), qkv_w ** -0.5),
        "w_branch_nsa": nrm(ks[18], (L, NSA_HEADS * HEAD_DIM, D_MODEL), (NSA_HEADS * HEAD_DIM) ** -0.5),
        "w_branch_gla": nrm(ks[19], (L, GLA_HEADS * GLA_DV, D_MODEL), (GLA_HEADS * GLA_DV) ** -0.5),
        "w_out": nrm(ks[20], (L, D_MODEL, D_MODEL), D_MODEL ** -0.5),
        "ffn_norm": gain(ks[21], (L, D_MODEL)),
        "w_up": nrm(ks[22], (L, D_MODEL, 2 * D_FF), D_MODEL ** -0.5),
        "conv_w": nrm(ks[23], (L, CONV_WIDTH, D_FF), CONV_WIDTH ** -0.5),
        "conv_b": nrm(ks[24], (L, D_FF), 0.02),
        "w_down": nrm(ks[25], (L, D_FF, D_MODEL), D_FF ** -0.5),
    }


def reference(x, rel_bias, attn_norm, w_in, moba_q_norm, moba_k_norm, nsa_q_norm, nsa_k_norm,
              cmp_pos_k, cmp_pos_v, cmp_k_w1, cmp_k_w2, cmp_v_w1, cmp_v_w2,
              gla_gate_w, gla_gate_b, gla_out_norm,
              w_branch_moba, w_branch_nsa, w_branch_gla, w_out,
              ffn_norm, w_up, conv_w, conv_b, w_down):
    for l in range(DEPTH):
        h = rms_norm(x, attn_norm[l])
        x = x + hybrid_mixer(h, rel_bias, w_in[l], moba_q_norm[l], moba_k_norm[l], nsa_q_norm[l], nsa_k_norm[l],
                             cmp_pos_k[l], cmp_pos_v[l], cmp_k_w1[l], cmp_k_w2[l], cmp_v_w1[l], cmp_v_w2[l],
                             gla_gate_w[l], gla_gate_b[l], gla_out_norm[l],
                             w_branch_moba[l], w_branch_nsa[l], w_branch_gla[l], w_out[l])
        h = rms_norm(x, ffn_norm[l])
        x = x + conv_ffn(h, w_up[l], conv_w[l], conv_b[l], w_down[l])
    return x
```

```python
import functools
import math

import jax
import jax.numpy as jnp
from jax import lax
from jax.experimental import pallas as pl
from jax.experimental.pallas import tpu as pltpu

F32 = jnp.float32
BF16 = jnp.bfloat16

D_MODEL = 1024
HEAD_DIM = 64
N_HEADS = 4
MOBA_BLOCK = 256
MOBA_TOPK = 3
NSA_CMP_LEN = 32
NSA_CMP_STRIDE = 16
NSA_SEL_LEN = 64
NSA_SEL_TOPN = 16
NSA_WINDOW = 512
NSA_FORCE_BONUS = 1e4
GLA_DK = 64
GLA_DV = 128
GLA_GATE_RANK = 16
GLA_GATE_NORM = 16.0
GLA_CHUNK = 64
GLA_SUB = 16
D_FF = 2816
REL_BUCKETS = 32
REL_MAX_DIST = 128
NORM_EPS = 1e-6
NEG_INF = -1e30

ATT_TILE = 256
LANE = 128
VMEM_LIMIT = 56 * 1024 * 1024

_C_MQ, _C_MK, _C_MV, _C_NQ = 0, 256, 512, 768
_C_KVC, _C_KSVS, _C_KWVW = 1024, 1152, 1280
_C_GQ, _C_GK, _C_GV, _C_GO, _C_MISC, _C_END = 1408, 1664, 1920, 2432, 2944, 3072
_MISC_NGATE = GLA_GATE_RANK


def _dot(a, b):
    return jnp.dot(a, b, preferred_element_type=F32)


def _dot_nt(a, b):
    return lax.dot_general(a, b, (((1,), (1,)), ((), ())), preferred_element_type=F32)


def _dot_tn(a, b):
    return lax.dot_general(a, b, (((0,), (0,)), ((), ())), preferred_element_type=F32)


def _rms_rows(x, gain):
    ms = jnp.mean(x * x, axis=-1, keepdims=True)
    return x * lax.rsqrt(ms + NORM_EPS) * gain


def _split_dot(x, w):
    hi = x.astype(BF16)
    lo = (x - hi.astype(F32)).astype(BF16)
    return _dot(hi, w) + _dot(lo, w)


def _params(sem):
    return pltpu.CompilerParams(dimension_semantics=sem, vmem_limit_bytes=VMEM_LIMIT)


def _const_spec(shape):
    return pl.BlockSpec(shape, lambda *_: (0,) * len(shape))


def _proj_kernel(x_ref, an_ref, w_ref, bd_ref, gmq_ref, gmk_ref, gnq_ref, gks_ref, gkw_ref, gw_ref, gb_ref,
                 mqkv_ref, nq_ref, kvc_ref, nkv_ref, gqk_ref, gv_ref, la_ref, go_ref, ng_ref):
    h = _rms_rows(x_ref[...], an_ref[...]).astype(BF16)

    def sec(lo, hi):
        return _dot(h, w_ref[:, lo:hi])

    def head_norm(y, gain):
        w = y.shape[1]
        ms = _split_dot(y * y, bd_ref[0:w, 0:w])
        return y * lax.rsqrt(ms + NORM_EPS) * gain

    scale = HEAD_DIM ** -0.5
    mqkv_ref[:, 0:256] = (head_norm(sec(_C_MQ, _C_MK), gmq_ref[...]) * scale).astype(BF16)
    mqkv_ref[:, 256:512] = head_norm(sec(_C_MK, _C_MV), gmk_ref[...]).astype(BF16)
    mqkv_ref[:, 512:768] = sec(_C_MV, _C_NQ).astype(BF16)
    nq_ref[...] = (head_norm(sec(_C_NQ, _C_KVC), gnq_ref[...]) * scale).astype(BF16)
    kvc_ref[...] = sec(_C_KVC, _C_KSVS).astype(BF16)
    first_half = lax.broadcasted_iota(jnp.int32, (h.shape[0], LANE), 1) < HEAD_DIM
    ksvs = sec(_C_KSVS, _C_KWVW)
    nkv_ref[:, 0:128] = jnp.where(first_half, head_norm(ksvs, gks_ref[...]), ksvs).astype(BF16)
    kwvw = sec(_C_KWVW, _C_GQ)
    nkv_ref[:, 128:256] = jnp.where(first_half, head_norm(kwvw, gkw_ref[...]), kwvw).astype(BF16)
    gqk_ref[:, 0:256] = (sec(_C_GQ, _C_GK) * (GLA_DK ** -0.5)).astype(BF16)
    gqk_ref[:, 256:512] = sec(_C_GK, _C_GV).astype(BF16)
    gv_ref[...] = sec(_C_GV, _C_GO).astype(BF16)
    go_ref[...] = jax.nn.silu(sec(_C_GO, _C_MISC)).astype(BF16)
    misc = sec(_C_MISC, _C_END)
    ng_ref[...] = jax.nn.sigmoid(misc)
    pre = _dot(misc.astype(BF16), gw_ref[...]) + gb_ref[...]
    la_ref[...] = (jnp.minimum(pre, 0.0) - jnp.log(1.0 + jnp.exp(-jnp.abs(pre)))) * (1.0 / GLA_GATE_NORM)


def _proj(x, an, w_a, bd, gmq, gmk, gnq, gks, gkw, gw, gb, tm=512):
    n = x.shape[0]
    row = lambda w: pl.BlockSpec((tm, w), lambda i: (i, 0))
    outs = [(768, BF16), (256, BF16), (128, BF16), (256, BF16), (512, BF16), (512, BF16), (256, F32), (512, BF16),
            (128, F32)]
    return pl.pallas_call(
        _proj_kernel,
        grid=(n // tm,),
        in_specs=[row(D_MODEL), _const_spec((1, D_MODEL)), _const_spec((D_MODEL, _C_END)), _const_spec((256, 256)),
                  _const_spec((1, 256)), _const_spec((1, 256)), _const_spec((1, 256)), _const_spec((1, 128)),
                  _const_spec((1, 128)), _const_spec((LANE, 256)), _const_spec((1, 256))],
        out_specs=[row(w) for w, _ in outs],
        out_shape=[jax.ShapeDtypeStruct((n, w), dt) for w, dt in outs],
        compiler_params=_params(("arbitrary",)),
        name="proj",
    )(x, an, w_a, bd, gmq, gmk, gnq, gks, gkw, gw, gb)


def _softmax_step(carry, s, v):
    m, l, acc = carry
    m_new = jnp.maximum(m, jnp.max(s, axis=1, keepdims=True))
    alpha = jnp.exp(m - m_new)
    p = jnp.exp(s - m_new)
    l = alpha * l + jnp.sum(p, axis=1, keepdims=True)
    acc = alpha * acc + _dot(p.astype(BF16), v)
    return m_new, l, acc


def _softmax_first(s, v):
    m = jnp.max(s, axis=1, keepdims=True)
    p = jnp.exp(s - m)
    return m, jnp.sum(p, axis=1, keepdims=True), _dot(p.astype(BF16), v)


def _moba_kernel(q_ref, k_ref, v_ref, bias_ref, o_ref, kmean_ref):
    t = ATT_TILE
    nb = k_ref.shape[0] // t
    qi = pl.program_id(1)

    @pl.when(qi == 0)
    def _():
        for h in range(N_HEADS):
            kh = k_ref[:, h * HEAD_DIM:(h + 1) * HEAD_DIM].astype(F32)
            kmean_ref[h] = jnp.mean(kh.reshape(nb, t, HEAD_DIM), axis=1)

    causal = lax.broadcasted_iota(jnp.int32, (t, t), 0) >= lax.broadcasted_iota(jnp.int32, (t, t), 1)
    blk = lax.broadcasted_iota(jnp.int32, (t, nb), 1)
    past = blk < qi
    outs = []
    for h in range(N_HEADS):
        hs = slice(h * HEAD_DIM, (h + 1) * HEAD_DIM)
        q = q_ref[:, hs]
        gate = jnp.where(past, _dot_nt(q, kmean_ref[h].astype(BF16)), NEG_INF)
        rank = jnp.zeros((t, nb), F32)
        for m in range(nb):
            gm = gate[:, m:m + 1]
            rank += jnp.where((gm > gate) | ((gm == gate) & (blk > m)), 1.0, 0.0)
        sel = jnp.where(past & (rank < MOBA_TOPK), 1.0, 0.0)

        own = pl.multiple_of(qi * t, t)
        s = _dot_nt(q, k_ref[pl.ds(own, t), hs]) + bias_ref[h, 0]
        carry = _softmax_first(jnp.where(causal, s, NEG_INF), v_ref[pl.ds(own, t), hs])

        def body(j, carry):
            rows = pl.ds(pl.multiple_of(j * t, t), t)
            s = _dot_nt(q, k_ref[rows, hs]) + bias_ref[h, jnp.minimum(qi - j, 2)]
            sel_j = jnp.sum(jnp.where(blk == j, sel, 0.0), axis=1, keepdims=True)
            return _softmax_step(carry, jnp.where(sel_j > 0.5, s, NEG_INF), v_ref[rows, hs])

        _, l, acc = lax.fori_loop(0, qi, body, carry)
        outs.append(acc / l)
    o_ref[...] = jnp.concatenate(outs, axis=1).astype(BF16)


def _moba(mqkv, bias, batch, seq):
    t = ATT_TILE
    nq = seq // t
    return pl.pallas_call(
        _moba_kernel,
        grid=(batch, nq),
        in_specs=[pl.BlockSpec((t, 256), lambda b, i: (b * nq + i, 0)),
                  pl.BlockSpec((seq, 256), lambda b, i: (b, 1)),
                  pl.BlockSpec((seq, 256), lambda b, i: (b, 2)),
                  _const_spec((N_HEADS, 3, t, t))],
        out_specs=pl.BlockSpec((t, 256), lambda b, i: (b * nq + i, 0)),
        out_shape=jax.ShapeDtypeStruct((batch * seq, 256), BF16),
        scratch_shapes=[pltpu.VMEM((N_HEADS, seq // t, HEAD_DIM), F32)],
        compiler_params=_params(("arbitrary", "arbitrary")),
        name="moba",
    )(mqkv, mqkv, mqkv, bias)


def _compress_kernel(x_ref, pa_ref, pb_ref, w1a_ref, w1b_ref, w2_ref, gk_ref, o_ref):
    x = x_ref[0].astype(F32)
    u = _dot((x + pa_ref[...]).astype(BF16), w1a_ref[...])
    v = _dot((x + pb_ref[...]).astype(BF16), w1b_ref[...])
    n = u.shape[0]
    hid = u + pltpu.roll(v, n - 1, axis=0)
    y = _dot(jax.nn.gelu(hid, approximate=True).astype(BF16), w2_ref[...])
    is_k = lax.broadcasted_iota(jnp.int32, y.shape, 1) < HEAD_DIM
    ms = jnp.sum(jnp.where(is_k, y * y, 0.0), axis=1, keepdims=True) * (1.0 / HEAD_DIM)
    o_ref[0] = jnp.where(is_k, y * lax.rsqrt(ms + NORM_EPS) * gk_ref[...], y).astype(BF16)


def _compress(kvc3, pa, pb, w1a, w1b, w2, gk):
    batch, n, w = kvc3.shape
    return pl.pallas_call(
        _compress_kernel,
        grid=(batch,),
        in_specs=[pl.BlockSpec((1, n, w), lambda b: (b, 0, 0)), _const_spec((1, w)), _const_spec((1, w)),
                  _const_spec((w, LANE)), _const_spec((w, LANE)), _const_spec((LANE, LANE)), _const_spec((1, LANE))],
        out_specs=pl.BlockSpec((1, n, LANE), lambda b: (b, 0, 0)),
        out_shape=jax.ShapeDtypeStruct((batch, n, LANE), BF16),
        compiler_params=_params(("arbitrary",)),
        name="compress",
    )(kvc3, pa, pb, w1a, w1b, w2, gk)


def _nsa_kernel(q_ref, kcv_ref, kv_ref, ng_ref, bias_ref, ov_ref, e_ref, o_ref):
    t = ATT_TILE
    nh = N_HEADS
    qi = pl.program_id(1)
    q_all = q_ref[...]
    qs = jnp.concatenate([q_all[:, h * HEAD_DIM:(h + 1) * HEAD_DIM] for h in range(nh)], axis=0)

    kcv = kcv_ref[0]
    n_cmp = kcv.shape[0]
    pos = qi * t + (lax.broadcasted_iota(jnp.int32, (nh * t, n_cmp), 0) & (t - 1))
    cend = lax.broadcasted_iota(jnp.int32, (nh * t, n_cmp), 1) * NSA_CMP_STRIDE + (NSA_CMP_LEN - 1)
    vis = cend <= pos
    sc = jnp.where(vis, _dot_nt(qs, kcv[:, 0:HEAD_DIM]), NEG_INF)
    e = jnp.where(vis, jnp.exp(sc - jnp.max(sc, axis=1, keepdims=True)), 0.0)
    den = jnp.sum(e, axis=1, keepdims=True)
    p = e / jnp.where(den > 0.0, den, 1.0)
    o_cmp = _dot(p.astype(BF16), kcv[:, HEAD_DIM:2 * HEAD_DIM])

    p_sum = p[0:t] + p[t:2 * t] + p[2 * t:3 * t] + p[3 * t:4 * t]
    imp = _split_dot(p_sum, ov_ref[...])
    blk = lax.broadcasted_iota(jnp.int32, (t, LANE), 1)
    cur = (qi * t + lax.broadcasted_iota(jnp.int32, (t, LANE), 0)) // NSA_SEL_LEN
    forced = (blk == 0) | (blk == cur) | (blk == cur - 1)
    valid = blk <= cur
    imp = jnp.where(valid, imp + jnp.where(forced, NSA_FORCE_BONUS, 0.0), NEG_INF)
    n_blk = kv_ref.shape[0] // NSA_SEL_LEN
    rank = jnp.zeros((t, LANE), F32)
    for m in range(n_blk):
        im = imp[:, m:m + 1]
        rank += jnp.where((im > imp) | ((im == imp) & (blk > m)), 1.0, 0.0)
    sel = jnp.where(valid & (rank < NSA_SEL_TOPN), 1.0, 0.0).astype(BF16)

    dist0 = ((lax.broadcasted_iota(jnp.int32, (nh * t, t), 0) & (t - 1))
             - lax.broadcasted_iota(jnp.int32, (nh * t, t), 1))

    def sel_tile(j, causal):
        rows = pl.ds(pl.multiple_of(j * t, t), t)
        s = _dot_nt(qs, kv_ref[rows, 0:64]) + bias_ref[jnp.minimum(qi - j, 2)]
        picked = _dot(sel, e_ref[j])
        mask = jnp.concatenate([picked] * nh, axis=0) > 0.5
        if causal:
            mask = mask & (dist0 >= 0)
        return jnp.where(mask, s, NEG_INF), kv_ref[rows, 64:128]

    carry = _softmax_first(*sel_tile(qi, True))
    _, l, acc = lax.fori_loop(0, qi, lambda j, c: _softmax_step(c, *sel_tile(j, False)), carry)
    o_slc = acc / l

    def win_tile(d):
        j = qi - d
        rows = pl.ds(pl.multiple_of(j * t, t), t)
        s = _dot_nt(qs, kv_ref[rows, 128:192]) + bias_ref[d]
        dist = dist0 + d * t
        return jnp.where((dist >= 0) & (dist < NSA_WINDOW), s, NEG_INF), kv_ref[rows, 192:256]

    carry = _softmax_first(*win_tile(0))
    n_win = jnp.minimum(qi, NSA_WINDOW // t) + 1
    _, l, acc = lax.fori_loop(1, n_win, lambda d, c: _softmax_step(c, *win_tile(d)), carry)
    o_win = acc / l

    gates = ng_ref[...]
    outs = []
    for h in range(nh):
        rs = slice(h * t, (h + 1) * t)
        g = lambda i: gates[:, _MISC_NGATE + 3 * h + i:_MISC_NGATE + 3 * h + i + 1]
        outs.append(g(0) * o_cmp[rs] + g(1) * o_slc[rs] + g(2) * o_win[rs])
    o_ref[...] = jnp.concatenate(outs, axis=1).astype(BF16)


def _nsa(nq_arr, kcv, nkv, ng, bias, ov, e, batch, seq):
    t = ATT_TILE
    nq = seq // t
    n_cmp = kcv.shape[1]
    return pl.pallas_call(
        _nsa_kernel,
        grid=(batch, nq),
        in_specs=[pl.BlockSpec((t, 256), lambda b, i: (b * nq + i, 0)),
                  pl.BlockSpec((1, n_cmp, LANE), lambda b, i: (b, 0, 0)),
                  pl.BlockSpec((seq, 256), lambda b, i: (b, 0)),
                  pl.BlockSpec((t, LANE), lambda b, i: (b * nq + i, 0)),
                  _const_spec((3, N_HEADS * t, t)),
                  _const_spec((n_cmp, LANE)),
                  _const_spec((nq, LANE, t))],
        out_specs=pl.BlockSpec((t, 256), lambda b, i: (b * nq + i, 0)),
        out_shape=jax.ShapeDtypeStruct((batch * seq, 256), BF16),
        compiler_params=_params(("arbitrary", "arbitrary")),
        name="nsa",
    )(nq_arr, kcv, nkv, ng, bias, ov, e)


def _gla_kernel(qk_ref, v_ref, la_ref, go_ref, tri_ref, gn_ref, o_ref, st_ref):
    c = GLA_CHUNK
    nh = N_HEADS
    n_chunk = qk_ref.shape[0] // c
    st_ref[...] = jnp.zeros_like(st_ref)
    tri = tri_ref[...]
    row = lax.broadcasted_iota(jnp.int32, (c, nh * GLA_DK), 0)
    sub_causal = (lax.broadcasted_iota(jnp.int32, (GLA_SUB, c), 0)
                  - lax.broadcasted_iota(jnp.int32, (GLA_SUB, c), 1))

    def chunk(ci, _):
        rows = pl.ds(pl.multiple_of(ci * c, c), c)
        g = la_ref[rows, :]
        g1 = g.astype(BF16)
        r1 = g - g1.astype(F32)
        g2 = r1.astype(BF16)
        g3 = (r1 - g2.astype(F32)).astype(BF16)
        b = _dot(tri, g1) + _dot(tri, g2) + _dot(tri, g3)
        q = qk_ref[rows, 0:256].astype(F32)
        k = qk_ref[rows, 256:512].astype(F32)
        v = v_ref[rows, :]
        b_last = b[c - 1:c, :]
        q_inter = (q * jnp.exp(b)).astype(BF16)
        k_state = (k * jnp.exp(b_last - b)).astype(BF16)
        decay_state = jnp.exp(b_last)

        a_rows = [[] for _ in range(nh)]
        for i in range(c // GLA_SUB):
            lo, hi = i * GLA_SUB, (i + 1) * GLA_SUB
            ref_b = b[lo:lo + 1, :]
            k_i = (k * jnp.exp(jnp.where(row < hi, ref_b - b, 0.0))).astype(BF16)
            q_i = (q[lo:hi] * jnp.exp(b[lo:hi] - ref_b)).astype(BF16)
            for h in range(nh):
                hs = slice(h * GLA_DK, (h + 1) * GLA_DK)
                a = _dot_nt(q_i[:, hs], k_i[:, hs])
                a_rows[h].append(jnp.where(sub_causal + lo >= 0, a, 0.0))

        outs = []
        for h in range(nh):
            hs = slice(h * GLA_DK, (h + 1) * GLA_DK)
            vs = slice(h * GLA_DV, (h + 1) * GLA_DV)
            a = jnp.concatenate(a_rows[h], axis=0).astype(BF16)
            st = st_ref[h]
            o = _dot(a, v[:, vs]) + _dot_nt(q_inter[:, hs], st.astype(BF16))
            st_ref[h] = st * decay_state[:, hs] + _dot_tn(v[:, vs], k_state[:, hs])
            outs.append(_rms_rows(o, gn_ref[...]))
        o_ref[rows, :] = (jnp.concatenate(outs, axis=1) * go_ref[rows, :].astype(F32)).astype(BF16)
        return 0

    lax.fori_loop(0, n_chunk, chunk, 0)


def _gla(gqk, gv, la, go, tri, gn, batch, seq):
    spec = lambda w: pl.BlockSpec((seq, w), lambda b: (b, 0))
    return pl.pallas_call(
        _gla_kernel,
        grid=(batch,),
        in_specs=[spec(512), spec(512), spec(256), spec(512), _const_spec((GLA_CHUNK, GLA_CHUNK)),
                  _const_spec((1, GLA_DV))],
        out_specs=spec(512),
        out_shape=jax.ShapeDtypeStruct((batch * seq, 512), BF16),
        scratch_shapes=[pltpu.VMEM((N_HEADS, GLA_DV, GLA_DK), F32)],
        compiler_params=_params(("arbitrary",)),
        name="gla",
    )(gqk, gv, la, go, tri, gn)


def _merge_kernel(x_ref, an_ref, om_ref, on_ref, og_ref, wm_ref, pm_ref, pn_ref, pg_ref, wo_ref, o_ref):
    x = x_ref[...]
    h = _rms_rows(x, an_ref[...]).astype(BF16)
    z = jax.nn.sigmoid(_dot(h, wm_ref[:, 0:D_MODEL])) * _dot(om_ref[...], pm_ref[...])
    z += jax.nn.sigmoid(_dot(h, wm_ref[:, D_MODEL:2 * D_MODEL])) * _dot(on_ref[...], pn_ref[...])
    z += jax.nn.sigmoid(_dot(h, wm_ref[:, 2 * D_MODEL:3 * D_MODEL])) * _dot(og_ref[...], pg_ref[...])
    o_ref[...] = x + _dot(z.astype(BF16), wo_ref[...])


def _merge(x, an, om, on, og, wm, pm, pn, pg, wo, tm=512):
    n = x.shape[0]
    row = lambda w: pl.BlockSpec((tm, w), lambda i: (i, 0))
    return pl.pallas_call(
        _merge_kernel,
        grid=(n // tm,),
        in_specs=[row(D_MODEL), _const_spec((1, D_MODEL)), row(256), row(256), row(512),
                  _const_spec((D_MODEL, 3 * D_MODEL)), _const_spec((256, D_MODEL)), _const_spec((256, D_MODEL)),
                  _const_spec((512, D_MODEL)), _const_spec((D_MODEL, D_MODEL))],
        out_specs=row(D_MODEL),
        out_shape=jax.ShapeDtypeStruct((n, D_MODEL), F32),
        compiler_params=_params(("arbitrary",)),
        name="merge",
    )(x, an, om, on, og, wm, pm, pn, pg, wo)


FFN_CHUNK = 256


def _ffn_kernel(x_ref, fn_ref, wa_ref, wg_ref, cw_ref, cb_ref, wd_ref, o_ref, carry_ref, *, tiles_per_seq):
    i = pl.program_id(0)
    tm = x_ref.shape[0]
    x = x_ref[...]
    h = _rms_rows(x, fn_ref[...]).astype(BF16)
    row = lax.broadcasted_iota(jnp.int32, (tm, FFN_CHUNK), 0)

    @pl.when((i % tiles_per_seq) == 0)
    def _():
        carry_ref[...] = jnp.zeros_like(carry_ref)

    o_ref[...] = x
    for c in range(D_FF // FFN_CHUNK):
        cs = slice(c * FFN_CHUNK, (c + 1) * FFN_CHUNK)
        a = _dot(h, wa_ref[:, cs])
        g = _dot(h, wg_ref[:, cs])
        prev = carry_ref[:, cs]
        p1 = prev[7:8, :]
        p2 = prev[6:7, :]
        a1 = jnp.where(row == 0, p1, pltpu.roll(a, 1, axis=0))
        a2 = jnp.where(row == 0, p2, jnp.where(row == 1, p1, pltpu.roll(a, 2, axis=0)))
        carry_ref[:, cs] = a[tm - 8:tm, :]
        w = cw_ref[:, cs]
        conv = w[0:1, :] * a2 + w[1:2, :] * a1 + w[2:3, :] * a + cb_ref[:, cs]
        act = (jax.nn.gelu(conv, approximate=True) * g).astype(BF16)
        o_ref[...] += _dot(act, wd_ref[cs, :])


def _ffn(x, fn, wa, wg, cw, cb, wd, seq, tm=256):
    n = x.shape[0]
    row = pl.BlockSpec((tm, D_MODEL), lambda i: (i, 0))
    return pl.pallas_call(
        functools.partial(_ffn_kernel, tiles_per_seq=seq // tm),
        grid=(n // tm,),
        in_specs=[row, _const_spec((1, D_MODEL)), _const_spec((D_MODEL, D_FF)), _const_spec((D_MODEL, D_FF)),
                  _const_spec((8, D_FF)), _const_spec((1, D_FF)), _const_spec((D_FF, D_MODEL))],
        out_specs=row,
        out_shape=jax.ShapeDtypeStruct((n, D_MODEL), F32),
        scratch_shapes=[pltpu.VMEM((8, D_FF), F32)],
        compiler_params=_params(("arbitrary",)),
        name="ffn",
    )(x, fn, wa, wg, cw, cb, wd)


def _rel_bucket(dist):
    n = jnp.maximum(dist, 0)
    max_exact = REL_BUCKETS // 2
    nf = jnp.maximum(n, 1).astype(F32)
    large = max_exact + (jnp.log(nf / max_exact) / math.log(REL_MAX_DIST / max_exact)
                         * (REL_BUCKETS - max_exact)).astype(jnp.int32)
    return jnp.where(n < max_exact, n, jnp.minimum(large, REL_BUCKETS - 1))


def _bias_tiles(rel_tab):
    t = ATT_TILE
    d0 = jnp.arange(t)[:, None] - jnp.arange(t)[None, :]
    tiles = [rel_tab[_rel_bucket(d0 + k * t)] for k in range(3)]
    return jnp.stack(tiles, axis=0).transpose(3, 0, 1, 2)


def _block_diag_mean():
    g = jnp.arange(256) // HEAD_DIM
    return jnp.where(g[:, None] == g[None, :], 1.0 / HEAD_DIM, 0.0).astype(BF16)


def _overlap(n_cmp_pad, n_cmp, n_blk):
    tok = jnp.arange(n_blk * NSA_SEL_LEN)
    starts = jnp.arange(n_cmp_pad) * NSA_CMP_STRIDE
    inside = (tok[None, :] >= starts[:, None]) & (tok[None, :] < starts[:, None] + NSA_CMP_LEN)
    m = inside.reshape(n_cmp_pad, n_blk, NSA_SEL_LEN).sum(-1).astype(F32) / NSA_CMP_LEN
    m = jnp.where(jnp.arange(n_cmp_pad)[:, None] < n_cmp, m, 0.0)
    return jnp.pad(m, ((0, 0), (0, LANE - n_blk))).astype(BF16)


def _sel_expand(seq):
    t = ATT_TILE
    key_blk = (jnp.arange(seq) // NSA_SEL_LEN).reshape(seq // t, 1, t)
    return (jnp.arange(LANE)[None, :, None] == key_blk).astype(BF16)


def _tile_gain(g, reps):
    return jnp.tile(g.astype(F32), reps)[None, :]


def kernel(x, rel_bias, attn_norm, w_in, moba_q_norm, moba_k_norm, nsa_q_norm, nsa_k_norm, cmp_pos_k, cmp_pos_v,
           cmp_k_w1, cmp_k_w2, cmp_v_w1, cmp_v_w2, gla_gate_w, gla_gate_b, gla_out_norm, w_branch_moba,
           w_branch_nsa, w_branch_gla, w_out, ffn_norm, w_up, conv_w, conv_b, w_down):
    batch, seq, _ = x.shape
    depth = w_in.shape[0]
    n_cmp = seq // NSA_CMP_STRIDE - NSA_CMP_LEN // NSA_CMP_STRIDE + 1
    n_cmp_pad = seq // NSA_CMP_STRIDE
    n_blk = seq // NSA_SEL_LEN

    tiles = _bias_tiles(rel_bias.astype(F32))
    bias_moba = tiles[:N_HEADS]
    bias_nsa = tiles[N_HEADS:].transpose(1, 0, 2, 3).reshape(3, N_HEADS * ATT_TILE, ATT_TILE)
    bd = _block_diag_mean()
    ov = _overlap(n_cmp_pad, n_cmp, n_blk)
    e_sel = _sel_expand(seq)
    tri = (jnp.arange(GLA_CHUNK)[:, None] >= jnp.arange(GLA_CHUNK)[None, :]).astype(BF16)
    ones64 = jnp.ones((HEAD_DIM,), F32)

    xf = x.reshape(batch * seq, D_MODEL)
    for l in range(depth):
        w = w_in[l]
        sp = [0, 256, 512, 768, 1024, 1408, 1420, 1676, 1932, 2444, 2460, 2972, 6044]
        mq, mk, mv, nq, nkv, ngate, gq, gk, gv, glr, gout, wmerge = (w[:, sp[i]:sp[i + 1]] for i in range(12))
        pad = jnp.zeros((D_MODEL, LANE - GLA_GATE_RANK - 3 * N_HEADS), w.dtype)
        w_a = jnp.concatenate([mq, mk, mv, nq, nkv, gq, gk, gv, gout, glr, ngate, pad], axis=1).astype(BF16)
        gw = jnp.pad(gla_gate_w[l], ((0, LANE - GLA_GATE_RANK), (0, 0))).astype(BF16)

        mqkv, nq_a, kvc, nkv_a, gqk, gv_a, la, go, ng = _proj(
            xf, attn_norm[l][None, :], w_a, bd,
            _tile_gain(moba_q_norm[l], 4), _tile_gain(moba_k_norm[l], 4), _tile_gain(nsa_q_norm[l], 4),
            jnp.concatenate([nsa_k_norm[l, 1], ones64])[None, :], jnp.concatenate([nsa_k_norm[l, 2], ones64])[None, :],
            gw, gla_gate_b[l][None, :])

        o_moba = _moba(mqkv, bias_moba, batch, seq)

        half = NSA_CMP_STRIDE * HEAD_DIM
        zero = jnp.zeros((NSA_CMP_STRIDE, HEAD_DIM, HEAD_DIM), F32)

        def w1_part(part):
            wk = cmp_k_w1[l][part * half:(part + 1) * half].reshape(NSA_CMP_STRIDE, HEAD_DIM, HEAD_DIM)
            wv = cmp_v_w1[l][part * half:(part + 1) * half].reshape(NSA_CMP_STRIDE, HEAD_DIM, HEAD_DIM)
            top = jnp.concatenate([wk, zero], axis=2)
            bot = jnp.concatenate([zero, wv], axis=2)
            return jnp.concatenate([top, bot], axis=1).reshape(NSA_CMP_STRIDE * LANE, LANE).astype(BF16)

        def pos_part(part):
            pk = cmp_pos_k[l][part * NSA_CMP_STRIDE:(part + 1) * NSA_CMP_STRIDE]
            pv = cmp_pos_v[l][part * NSA_CMP_STRIDE:(part + 1) * NSA_CMP_STRIDE]
            return jnp.concatenate([pk, pv], axis=1).reshape(1, NSA_CMP_STRIDE * LANE).astype(F32)

        z64 = jnp.zeros((HEAD_DIM, HEAD_DIM), F32)
        w2 = jnp.concatenate([jnp.concatenate([cmp_k_w2[l], z64], axis=1),
                              jnp.concatenate([z64, cmp_v_w2[l]], axis=1)], axis=0).astype(BF16)
        kcv = _compress(kvc.reshape(batch, n_cmp_pad, NSA_CMP_STRIDE * LANE), pos_part(0), pos_part(1),
                        w1_part(0), w1_part(1), w2, jnp.concatenate([nsa_k_norm[l, 0], ones64])[None, :])

        o_nsa = _nsa(nq_a, kcv, nkv_a, ng, bias_nsa, ov, e_sel, batch, seq)
        o_gla = _gla(gqk, gv_a, la, go, tri, gla_out_norm[l][None, :].astype(F32), batch, seq)

        xf = _merge(xf, attn_norm[l][None, :], o_moba, o_nsa, o_gla, wmerge.astype(BF16),
                    w_branch_moba[l].astype(BF16), w_branch_nsa[l].astype(BF16), w_branch_gla[l].astype(BF16),
                    w_out[l].astype(BF16))

        cw = jnp.pad(conv_w[l], ((0, 8 - conv_w.shape[1]), (0, 0)))
        xf = _ffn(xf, ffn_norm[l][None, :], w_up[l][:, :D_FF].astype(BF16), w_up[l][:, D_FF:].astype(BF16),
                  cw, conv_b[l][None, :], w_down[l].astype(BF16), seq)
    return xf.reshape(batch, seq, D_MODEL)
```

```python
import functools
import math

import jax
import jax.numpy as jnp
from jax import lax
from jax.experimental import pallas as pl
from jax.experimental.pallas import tpu as pltpu

F32 = jnp.float32
BF16 = jnp.bfloat16

D_MODEL = 1024
HEAD_DIM = 64
N_HEADS = 4
MOBA_BLOCK = 256
MOBA_TOPK = 3
NSA_CMP_LEN = 32
NSA_CMP_STRIDE = 16
NSA_SEL_LEN = 64
NSA_SEL_TOPN = 16
NSA_WINDOW = 512
NSA_FORCE_BONUS = 1e4
GLA_DK = 64
GLA_DV = 128
GLA_GATE_RANK = 16
GLA_GATE_NORM = 16.0
GLA_CHUNK = 64
GLA_SUB = 16
D_FF = 2816
REL_BUCKETS = 32
REL_MAX_DIST = 128
NORM_EPS = 1e-6
NEG_INF = -1e30

ATT_TILE = 256
LANE = 128
VMEM_LIMIT = 56 * 1024 * 1024

_C_MQ, _C_MK, _C_MV, _C_NQ = 0, 256, 512, 768
_C_KVC, _C_KSVS, _C_KWVW = 1024, 1152, 1280
_C_GQ, _C_GK, _C_GV, _C_GO, _C_MISC, _C_END = 1408, 1664, 1920, 2432, 2944, 3072
_MISC_NGATE = GLA_GATE_RANK


def _dot(a, b):
    return jnp.dot(a, b, preferred_element_type=F32)


def _dot_nt(a, b):
    return lax.dot_general(a, b, (((1,), (1,)), ((), ())), preferred_element_type=F32)


def _dot_tn(a, b):
    return lax.dot_general(a, b, (((0,), (0,)), ((), ())), preferred_element_type=F32)


def _rms_rows(x, gain):
    ms = jnp.mean(x * x, axis=-1, keepdims=True)
    return x * lax.rsqrt(ms + NORM_EPS) * gain


def _split_dot(x, w):
    hi = x.astype(BF16)
    lo = (x - hi.astype(F32)).astype(BF16)
    return _dot(hi, w) + _dot(lo, w)


def _params(sem):
    return pltpu.CompilerParams(dimension_semantics=sem, vmem_limit_bytes=VMEM_LIMIT)


def _const_spec(shape):
    return pl.BlockSpec(shape, lambda *_: (0,) * len(shape))


def _proj_kernel(x_ref, an_ref, w_ref, bd_ref, gmq_ref, gmk_ref, gnq_ref, gks_ref, gkw_ref, gw_ref, gb_ref,
                 mqt_ref, mk_ref, mvt_ref, nqt_ref, kvc_ref, nkv_ref, nkvt_ref, gqk_ref, gv_ref, la_ref, go_ref,
                 ngt_ref):
    h = _rms_rows(x_ref[...], an_ref[...]).astype(BF16)

    def sec(lo, hi):
        return _dot(h, w_ref[:, lo:hi])

    def head_norm(y, gain):
        w = y.shape[1]
        ms = _split_dot(y * y, bd_ref[0:w, 0:w])
        return y * lax.rsqrt(ms + NORM_EPS) * gain

    scale = HEAD_DIM ** -0.5
    mqt_ref[...] = (head_norm(sec(_C_MQ, _C_MK), gmq_ref[...]) * scale).T.astype(BF16)
    mk_ref[...] = head_norm(sec(_C_MK, _C_MV), gmk_ref[...]).astype(BF16)
    mvt_ref[...] = sec(_C_MV, _C_NQ).T.astype(BF16)
    nqt_ref[...] = (head_norm(sec(_C_NQ, _C_KVC), gnq_ref[...]) * scale).T.astype(BF16)
    kvc_ref[...] = sec(_C_KVC, _C_KSVS).astype(BF16)
    first_half = lax.broadcasted_iota(jnp.int32, (h.shape[0], LANE), 1) < HEAD_DIM
    ksvs = sec(_C_KSVS, _C_KWVW)
    ksvs = jnp.where(first_half, head_norm(ksvs, gks_ref[...]), ksvs)
    nkv_ref[:, 0:128] = ksvs.astype(BF16)
    nkvt_ref[0:128, :] = ksvs.T.astype(BF16)
    kwvw = sec(_C_KWVW, _C_GQ)
    kwvw = jnp.where(first_half, head_norm(kwvw, gkw_ref[...]), kwvw)
    nkv_ref[:, 128:256] = kwvw.astype(BF16)
    nkvt_ref[128:256, :] = kwvw.T.astype(BF16)
    gqk_ref[:, 0:256] = (sec(_C_GQ, _C_GK) * (GLA_DK ** -0.5)).astype(BF16)
    gqk_ref[:, 256:512] = sec(_C_GK, _C_GV).astype(BF16)
    gv_ref[...] = sec(_C_GV, _C_GO).astype(BF16)
    go_ref[...] = jax.nn.silu(sec(_C_GO, _C_MISC)).astype(BF16)
    misc = sec(_C_MISC, _C_END)
    ngt_ref[...] = jax.nn.sigmoid(misc).T
    pre = _dot(misc.astype(BF16), gw_ref[...]) + gb_ref[...]
    la_ref[...] = (jnp.minimum(pre, 0.0) - jnp.log(1.0 + jnp.exp(-jnp.abs(pre)))) * (1.0 / GLA_GATE_NORM)


def _proj(x, an, w_a, bd, gmq, gmk, gnq, gks, gkw, gw, gb, tm=512):
    n = x.shape[0]
    row = lambda w: pl.BlockSpec((tm, w), lambda i: (i, 0))
    col = lambda w: pl.BlockSpec((w, tm), lambda i: (0, i))
    outs = [(256, BF16, True), (256, BF16, False), (256, BF16, True), (256, BF16, True), (128, BF16, False),
            (256, BF16, False), (256, BF16, True), (512, BF16, False), (512, BF16, False), (256, F32, False),
            (512, BF16, False), (128, F32, True)]
    return pl.pallas_call(
        _proj_kernel,
        grid=(n // tm,),
        in_specs=[row(D_MODEL), _const_spec((1, D_MODEL)), _const_spec((D_MODEL, _C_END)), _const_spec((256, 256)),
                  _const_spec((1, 256)), _const_spec((1, 256)), _const_spec((1, 256)), _const_spec((1, 128)),
                  _const_spec((1, 128)), _const_spec((LANE, 256)), _const_spec((1, 256))],
        out_specs=[col(w) if tr else row(w) for w, _, tr in outs],
        out_shape=[jax.ShapeDtypeStruct((w, n) if tr else (n, w), dt) for w, dt, tr in outs],
        compiler_params=_params(("arbitrary",)),
        name="proj",
    )(x, an, w_a, bd, gmq, gmk, gnq, gks, gkw, gw, gb)


def _softmax_step(carry, s, vt):
    m, l, acc = carry
    m_new = jnp.maximum(m, jnp.max(s, axis=0, keepdims=True))
    alpha = jnp.exp(m - m_new)
    p = jnp.exp(s - m_new)
    l = alpha * l + jnp.sum(p, axis=0, keepdims=True)
    acc = alpha * acc + _dot(vt, p.astype(BF16))
    return m_new, l, acc


def _softmax_first(s, vt):
    m = jnp.max(s, axis=0, keepdims=True)
    p = jnp.exp(s - m)
    return m, jnp.sum(p, axis=0, keepdims=True), _dot(vt, p.astype(BF16))


def _rank_rows(score, n):
    idx = lax.broadcasted_iota(jnp.int32, score.shape, 0)
    rank = jnp.zeros(score.shape, F32)
    for m in range(n):
        sm = score[m:m + 1, :]
        rank += jnp.where((sm > score) | ((sm == score) & (idx > m)), 1.0, 0.0)
    return rank


def _moba_kernel(qt_ref, k_ref, vt_ref, bias_ref, o_ref, kmean_ref, sel_ref):
    t = ATT_TILE
    nb = k_ref.shape[0] // t
    qi = pl.program_id(1)

    @pl.when(qi == 0)
    def _():
        for h in range(N_HEADS):
            kh = k_ref[:, h * HEAD_DIM:(h + 1) * HEAD_DIM].astype(F32)
            kmean_ref[h] = jnp.mean(kh.reshape(nb, t, HEAD_DIM), axis=1)

    causal = lax.broadcasted_iota(jnp.int32, (t, t), 0) <= lax.broadcasted_iota(jnp.int32, (t, t), 1)
    past = lax.broadcasted_iota(jnp.int32, (nb, t), 0) < qi
    own = pl.multiple_of(qi * t, t)
    outs = []
    for h in range(N_HEADS):
        hs = slice(h * HEAD_DIM, (h + 1) * HEAD_DIM)
        qt = qt_ref[hs, :]
        gate = jnp.where(past, _dot(kmean_ref[h].astype(BF16), qt), NEG_INF)
        sel_ref[h] = jnp.where(past & (_rank_rows(gate, nb) < MOBA_TOPK), 1.0, 0.0)

        s = _dot(k_ref[pl.ds(own, t), hs], qt) + bias_ref[h, 0]
        carry = _softmax_first(jnp.where(causal, s, NEG_INF), vt_ref[hs, pl.ds(own, t)])

        def body(j, carry):
            keys = pl.ds(pl.multiple_of(j * t, t), t)
            s = _dot(k_ref[keys, hs], qt) + bias_ref[h, jnp.minimum(qi - j, 2)]
            s = jnp.where(sel_ref[h, pl.ds(j, 1), :] > 0.5, s, NEG_INF)
            return _softmax_step(carry, s, vt_ref[hs, keys])

        _, l, acc = lax.fori_loop(0, qi, body, carry)
        outs.append(acc / l)
    o_ref[...] = jnp.concatenate(outs, axis=0).T.astype(BF16)


def _moba(mqt, mk, mvt, bias, batch, seq):
    t = ATT_TILE
    nq = seq // t
    return pl.pallas_call(
        _moba_kernel,
        grid=(batch, nq),
        in_specs=[pl.BlockSpec((256, t), lambda b, i: (0, b * nq + i)),
                  pl.BlockSpec((seq, 256), lambda b, i: (b, 0)),
                  pl.BlockSpec((256, seq), lambda b, i: (0, b)),
                  _const_spec((N_HEADS, 3, t, t))],
        out_specs=pl.BlockSpec((t, 256), lambda b, i: (b * nq + i, 0)),
        out_shape=jax.ShapeDtypeStruct((batch * seq, 256), BF16),
        scratch_shapes=[pltpu.VMEM((N_HEADS, nq, HEAD_DIM), F32), pltpu.VMEM((N_HEADS, nq, t), F32)],
        compiler_params=_params(("arbitrary", "arbitrary")),
        name="moba",
    )(mqt, mk, mvt, bias)


def _compress_kernel(x_ref, pa_ref, pb_ref, w1a_ref, w1b_ref, w2_ref, gk_ref, o_ref, ot_ref):
    x = x_ref[0].astype(F32)
    u = _dot((x + pa_ref[...]).astype(BF16), w1a_ref[...])
    v = _dot((x + pb_ref[...]).astype(BF16), w1b_ref[...])
    n = u.shape[0]
    hid = u + pltpu.roll(v, n - 1, axis=0)
    y = _dot(jax.nn.gelu(hid, approximate=True).astype(BF16), w2_ref[...])
    is_k = lax.broadcasted_iota(jnp.int32, y.shape, 1) < HEAD_DIM
    ms = jnp.sum(jnp.where(is_k, y * y, 0.0), axis=1, keepdims=True) * (1.0 / HEAD_DIM)
    y = jnp.where(is_k, y * lax.rsqrt(ms + NORM_EPS) * gk_ref[...], y)
    o_ref[0] = y.astype(BF16)
    ot_ref[0] = y.T.astype(BF16)


def _compress(kvc3, pa, pb, w1a, w1b, w2, gk):
    batch, n, w = kvc3.shape
    return pl.pallas_call(
        _compress_kernel,
        grid=(batch,),
        in_specs=[pl.BlockSpec((1, n, w), lambda b: (b, 0, 0)), _const_spec((1, w)), _const_spec((1, w)),
                  _const_spec((w, LANE)), _const_spec((w, LANE)), _const_spec((LANE, LANE)), _const_spec((1, LANE))],
        out_specs=[pl.BlockSpec((1, n, LANE), lambda b: (b, 0, 0)), pl.BlockSpec((1, LANE, n), lambda b: (b, 0, 0))],
        out_shape=[jax.ShapeDtypeStruct((batch, n, LANE), BF16), jax.ShapeDtypeStruct((batch, LANE, n), BF16)],
        compiler_params=_params(("arbitrary",)),
        name="compress",
    )(kvc3, pa, pb, w1a, w1b, w2, gk)


def _nsa_kernel(qt_ref, kc_ref, kct_ref, kv_ref, kvt_ref, ngt_ref, bias_ref, ovt_ref, et_ref, o_ref):
    t = ATT_TILE
    nh = N_HEADS
    qi = pl.program_id(1)
    qt_all = qt_ref[...]
    qs = jnp.concatenate([qt_all[h * HEAD_DIM:(h + 1) * HEAD_DIM, :] for h in range(nh)], axis=1)

    kc = kc_ref[0][:, 0:HEAD_DIM]
    vct = kct_ref[0][HEAD_DIM:2 * HEAD_DIM, :]
    n_cmp = kc.shape[0]
    pos = qi * t + (lax.broadcasted_iota(jnp.int32, (n_cmp, nh * t), 1) & (t - 1))
    cend = lax.broadcasted_iota(jnp.int32, (n_cmp, nh * t), 0) * NSA_CMP_STRIDE + (NSA_CMP_LEN - 1)
    vis = cend <= pos
    sc = jnp.where(vis, _dot(kc, qs), NEG_INF)
    e = jnp.where(vis, jnp.exp(sc - jnp.max(sc, axis=0, keepdims=True)), 0.0)
    den = jnp.sum(e, axis=0, keepdims=True)
    p = e / jnp.where(den > 0.0, den, 1.0)
    o_cmp = _dot(vct, p.astype(BF16))

    p_sum = p[:, 0:t] + p[:, t:2 * t] + p[:, 2 * t:3 * t] + p[:, 3 * t:4 * t]
    p_hi = p_sum.astype(BF16)
    p_lo = (p_sum - p_hi.astype(F32)).astype(BF16)
    ovt = ovt_ref[...]
    n_blk = ovt.shape[0]
    imp = _dot(ovt, p_hi) + _dot(ovt, p_lo)
    blk = lax.broadcasted_iota(jnp.int32, (n_blk, t), 0)
    cur = (qi * t + lax.broadcasted_iota(jnp.int32, (n_blk, t), 1)) // NSA_SEL_LEN
    forced = (blk == 0) | (blk == cur) | (blk == cur - 1)
    valid = blk <= cur
    imp = jnp.where(valid, imp + jnp.where(forced, NSA_FORCE_BONUS, 0.0), NEG_INF)
    sel = jnp.where(valid & (_rank_rows(imp, n_blk) < NSA_SEL_TOPN), 1.0, 0.0).astype(BF16)

    dist0 = ((lax.broadcasted_iota(jnp.int32, (t, nh * t), 1) & (t - 1))
             - lax.broadcasted_iota(jnp.int32, (t, nh * t), 0))

    def sel_tile(j, causal):
        keys = pl.ds(pl.multiple_of(j * t, t), t)
        s = _dot(kv_ref[keys, 0:64], qs) + bias_ref[jnp.minimum(qi - j, 2)]
        picked = _dot(et_ref[j], sel)
        mask = jnp.concatenate([picked] * nh, axis=1) > 0.5
        if causal:
            mask = mask & (dist0 >= 0)
        return jnp.where(mask, s, NEG_INF), kvt_ref[64:128, keys]

    carry = _softmax_first(*sel_tile(qi, True))
    _, l, acc = lax.fori_loop(0, qi, lambda j, c: _softmax_step(c, *sel_tile(j, False)), carry)
    o_slc = acc / l

    def win_tile(d):
        keys = pl.ds(pl.multiple_of((qi - d) * t, t), t)
        s = _dot(kv_ref[keys, 128:192], qs) + bias_ref[d]
        dist = dist0 + d * t
        return jnp.where((dist >= 0) & (dist < NSA_WINDOW), s, NEG_INF), kvt_ref[192:256, keys]

    carry = _softmax_first(*win_tile(0))
    n_win = jnp.minimum(qi, NSA_WINDOW // t) + 1
    _, l, acc = lax.fori_loop(1, n_win, lambda d, c: _softmax_step(c, *win_tile(d)), carry)
    o_win = acc / l

    outs = []
    for h in range(nh):
        cs = slice(h * t, (h + 1) * t)
        g = [ngt_ref[_MISC_NGATE + 3 * h + i:_MISC_NGATE + 3 * h + i + 1, :] for i in range(3)]
        outs.append(g[0] * o_cmp[:, cs] + g[1] * o_slc[:, cs] + g[2] * o_win[:, cs])
    o_ref[...] = jnp.concatenate(outs, axis=0).T.astype(BF16)


def _nsa(nqt, kc, kct, nkv, nkvt, ngt, bias, ovt, et, batch, seq):
    t = ATT_TILE
    nq = seq // t
    n_cmp = kc.shape[1]
    n_blk = ovt.shape[0]
    return pl.pallas_call(
        _nsa_kernel,
        grid=(batch, nq),
        in_specs=[pl.BlockSpec((256, t), lambda b, i: (0, b * nq + i)),
                  pl.BlockSpec((1, n_cmp, LANE), lambda b, i: (b, 0, 0)),
                  pl.BlockSpec((1, LANE, n_cmp), lambda b, i: (b, 0, 0)),
                  pl.BlockSpec((seq, 256), lambda b, i: (b, 0)),
                  pl.BlockSpec((256, seq), lambda b, i: (0, b)),
                  pl.BlockSpec((LANE, t), lambda b, i: (0, b * nq + i)),
                  _const_spec((3, t, N_HEADS * t)),
                  _const_spec((n_blk, n_cmp)),
                  _const_spec((nq, t, n_blk))],
        out_specs=pl.BlockSpec((t, 256), lambda b, i: (b * nq + i, 0)),
        out_shape=jax.ShapeDtypeStruct((batch * seq, 256), BF16),
        compiler_params=_params(("arbitrary", "arbitrary")),
        name="nsa",
    )(nqt, kc, kct, nkv, nkvt, ngt, bias, ovt, et)


def _gla_kernel(qk_ref, v_ref, la_ref, go_ref, tri_ref, gn_ref, o_ref, st_ref):
    c = GLA_CHUNK
    nh = N_HEADS
    n_chunk = qk_ref.shape[0] // c
    st_ref[...] = jnp.zeros_like(st_ref)
    tri = tri_ref[...]
    row = lax.broadcasted_iota(jnp.int32, (c, nh * GLA_DK), 0)
    sub_causal = (lax.broadcasted_iota(jnp.int32, (GLA_SUB, c), 0)
                  - lax.broadcasted_iota(jnp.int32, (GLA_SUB, c), 1))

    def chunk(ci, _):
        rows = pl.ds(pl.multiple_of(ci * c, c), c)
        g = la_ref[rows, :]
        g1 = g.astype(BF16)
        r1 = g - g1.astype(F32)
        g2 = r1.astype(BF16)
        g3 = (r1 - g2.astype(F32)).astype(BF16)
        b = _dot(tri, g1) + _dot(tri, g2) + _dot(tri, g3)
        q = qk_ref[rows, 0:256].astype(F32)
        k = qk_ref[rows, 256:512].astype(F32)
        v = v_ref[rows, :]
        b_last = b[c - 1:c, :]
        q_inter = (q * jnp.exp(b)).astype(BF16)
        k_state = (k * jnp.exp(b_last - b)).astype(BF16)
        decay_state = jnp.exp(b_last)

        a_rows = [[] for _ in range(nh)]
        for i in range(c // GLA_SUB):
            lo, hi = i * GLA_SUB, (i + 1) * GLA_SUB
            ref_b = b[lo:lo + 1, :]
            k_i = (k * jnp.exp(jnp.where(row < hi, ref_b - b, 0.0))).astype(BF16)
            q_i = (q[lo:hi] * jnp.exp(b[lo:hi] - ref_b)).astype(BF16)
            for h in range(nh):
                hs = slice(h * GLA_DK, (h + 1) * GLA_DK)
                a = _dot_nt(q_i[:, hs], k_i[:, hs])
                a_rows[h].append(jnp.where(sub_causal + lo >= 0, a, 0.0))

        outs = []
        for h in range(nh):
            hs = slice(h * GLA_DK, (h + 1) * GLA_DK)
            vs = slice(h * GLA_DV, (h + 1) * GLA_DV)
            a = jnp.concatenate(a_rows[h], axis=0).astype(BF16)
            st = st_ref[h]
            o = _dot(a, v[:, vs]) + _dot_nt(q_inter[:, hs], st.astype(BF16))
            st_ref[h] = st * decay_state[:, hs] + _dot_tn(v[:, vs], k_state[:, hs])
            outs.append(_rms_rows(o, gn_ref[...]))
        o_ref[rows, :] = (jnp.concatenate(outs, axis=1) * go_ref[rows, :].astype(F32)).astype(BF16)
        return 0

    lax.fori_loop(0, n_chunk, chunk, 0)


def _gla(gqk, gv, la, go, tri, gn, batch, seq):
    spec = lambda w: pl.BlockSpec((seq, w), lambda b: (b, 0))
    return pl.pallas_call(
        _gla_kernel,
        grid=(batch,),
        in_specs=[spec(512), spec(512), spec(256), spec(512), _const_spec((GLA_CHUNK, GLA_CHUNK)),
                  _const_spec((1, GLA_DV))],
        out_specs=spec(512),
        out_shape=jax.ShapeDtypeStruct((batch * seq, 512), BF16),
        scratch_shapes=[pltpu.VMEM((N_HEADS, GLA_DV, GLA_DK), F32)],
        compiler_params=_params(("arbitrary",)),
        name="gla",
    )(gqk, gv, la, go, tri, gn)


def _merge_kernel(x_ref, an_ref, om_ref, on_ref, og_ref, wm_ref, pm_ref, pn_ref, pg_ref, wo_ref, o_ref):
    x = x_ref[...]
    h = _rms_rows(x, an_ref[...]).astype(BF16)
    z = jax.nn.sigmoid(_dot(h, wm_ref[:, 0:D_MODEL])) * _dot(om_ref[...], pm_ref[...])
    z += jax.nn.sigmoid(_dot(h, wm_ref[:, D_MODEL:2 * D_MODEL])) * _dot(on_ref[...], pn_ref[...])
    z += jax.nn.sigmoid(_dot(h, wm_ref[:, 2 * D_MODEL:3 * D_MODEL])) * _dot(og_ref[...], pg_ref[...])
    o_ref[...] = x + _dot(z.astype(BF16), wo_ref[...])


def _merge(x, an, om, on, og, wm, pm, pn, pg, wo, tm=512):
    n = x.shape[0]
    row = lambda w: pl.BlockSpec((tm, w), lambda i: (i, 0))
    return pl.pallas_call(
        _merge_kernel,
        grid=(n // tm,),
        in_specs=[row(D_MODEL), _const_spec((1, D_MODEL)), row(256), row(256), row(512),
                  _const_spec((D_MODEL, 3 * D_MODEL)), _const_spec((256, D_MODEL)), _const_spec((256, D_MODEL)),
                  _const_spec((512, D_MODEL)), _const_spec((D_MODEL, D_MODEL))],
        out_specs=row(D_MODEL),
        out_shape=jax.ShapeDtypeStruct((n, D_MODEL), F32),
        compiler_params=_params(("arbitrary",)),
        name="merge",
    )(x, an, om, on, og, wm, pm, pn, pg, wo)


FFN_CHUNK = 256


def _ffn_kernel(x_ref, fn_ref, wa_ref, wg_ref, cw_ref, cb_ref, wd_ref, o_ref, carry_ref, *, tiles_per_seq):
    i = pl.program_id(0)
    tm = x_ref.shape[0]
    x = x_ref[...]
    h = _rms_rows(x, fn_ref[...]).astype(BF16)
    row = lax.broadcasted_iota(jnp.int32, (tm, FFN_CHUNK), 0)

    @pl.when((i % tiles_per_seq) == 0)
    def _():
        carry_ref[...] = jnp.zeros_like(carry_ref)

    o_ref[...] = x
    for c in range(D_FF // FFN_CHUNK):
        cs = slice(c * FFN_CHUNK, (c + 1) * FFN_CHUNK)
        a = _dot(h, wa_ref[:, cs])
        g = _dot(h, wg_ref[:, cs])
        prev = carry_ref[:, cs]
        p1 = prev[7:8, :]
        p2 = prev[6:7, :]
        a1 = jnp.where(row == 0, p1, pltpu.roll(a, 1, axis=0))
        a2 = jnp.where(row == 0, p2, jnp.where(row == 1, p1, pltpu.roll(a, 2, axis=0)))
        carry_ref[:, cs] = a[tm - 8:tm, :]
        w = cw_ref[:, cs]
        conv = w[0:1, :] * a2 + w[1:2, :] * a1 + w[2:3, :] * a + cb_ref[:, cs]
        act = (jax.nn.gelu(conv, approximate=True) * g).astype(BF16)
        o_ref[...] += _dot(act, wd_ref[cs, :])


def _ffn(x, fn, wa, wg, cw, cb, wd, seq, tm=256):
    n = x.shape[0]
    row = pl.BlockSpec((tm, D_MODEL), lambda i: (i, 0))
    return pl.pallas_call(
        functools.partial(_ffn_kernel, tiles_per_seq=seq // tm),
        grid=(n // tm,),
        in_specs=[row, _const_spec((1, D_MODEL)), _const_spec((D_MODEL, D_FF)), _const_spec((D_MODEL, D_FF)),
                  _const_spec((8, D_FF)), _const_spec((1, D_FF)), _const_spec((D_FF, D_MODEL))],
        out_specs=row,
        out_shape=jax.ShapeDtypeStruct((n, D_MODEL), F32),
        scratch_shapes=[pltpu.VMEM((8, D_FF), F32)],
        compiler_params=_params(("arbitrary",)),
        name="ffn",
    )(x, fn, wa, wg, cw, cb, wd)


def _rel_bucket(dist):
    n = jnp.maximum(dist, 0)
    max_exact = REL_BUCKETS // 2
    nf = jnp.maximum(n, 1).astype(F32)
    large = max_exact + (jnp.log(nf / max_exact) / math.log(REL_MAX_DIST / max_exact)
                         * (REL_BUCKETS - max_exact)).astype(jnp.int32)
    return jnp.where(n < max_exact, n, jnp.minimum(large, REL_BUCKETS - 1))


def _bias_tiles(rel_tab):
    t = ATT_TILE
    d0 = jnp.arange(t)[None, :] - jnp.arange(t)[:, None]
    bucket = jnp.stack([_rel_bucket(d0 + k * t) for k in range(3)])
    out = jnp.zeros((rel_tab.shape[1],) + bucket.shape, F32)
    for b in range(REL_BUCKETS):
        out = jnp.where(bucket[None] == b, rel_tab[b][:, None, None, None], out)
    return out


def _block_diag_mean():
    g = jnp.arange(256) // HEAD_DIM
    return jnp.where(g[:, None] == g[None, :], 1.0 / HEAD_DIM, 0.0).astype(BF16)


def _overlap_t(n_cmp_pad, n_cmp, n_blk):
    tok = jnp.arange(n_blk * NSA_SEL_LEN)
    starts = jnp.arange(n_cmp_pad) * NSA_CMP_STRIDE
    inside = (tok[None, :] >= starts[:, None]) & (tok[None, :] < starts[:, None] + NSA_CMP_LEN)
    m = inside.reshape(n_cmp_pad, n_blk, NSA_SEL_LEN).sum(-1).astype(F32) / NSA_CMP_LEN
    m = jnp.where(jnp.arange(n_cmp_pad)[:, None] < n_cmp, m, 0.0)
    return m.T.astype(BF16)


def _sel_expand_t(seq, n_blk):
    t = ATT_TILE
    key_blk = (jnp.arange(seq) // NSA_SEL_LEN).reshape(seq // t, t, 1)
    return (jnp.arange(n_blk)[None, None, :] == key_blk).astype(BF16)


def _tile_gain(g, reps):
    return jnp.tile(g.astype(F32), reps)[None, :]


def kernel(x, rel_bias, attn_norm, w_in, moba_q_norm, moba_k_norm, nsa_q_norm, nsa_k_norm, cmp_pos_k, cmp_pos_v,
           cmp_k_w1, cmp_k_w2, cmp_v_w1, cmp_v_w2, gla_gate_w, gla_gate_b, gla_out_norm, w_branch_moba,
           w_branch_nsa, w_branch_gla, w_out, ffn_norm, w_up, conv_w, conv_b, w_down):
    batch, seq, _ = x.shape
    depth = w_in.shape[0]
    n_cmp = seq // NSA_CMP_STRIDE - NSA_CMP_LEN // NSA_CMP_STRIDE + 1
    n_cmp_pad = seq // NSA_CMP_STRIDE
    n_blk = seq // NSA_SEL_LEN

    tiles = _bias_tiles(rel_bias.astype(F32))
    bias_moba = tiles[:N_HEADS]
    bias_nsa = jnp.concatenate([tiles[N_HEADS + h] for h in range(N_HEADS)], axis=2)
    bd = _block_diag_mean()
    ovt = _overlap_t(n_cmp_pad, n_cmp, n_blk)
    e_sel = _sel_expand_t(seq, n_blk)
    tri = (jnp.arange(GLA_CHUNK)[:, None] >= jnp.arange(GLA_CHUNK)[None, :]).astype(BF16)
    ones64 = jnp.ones((HEAD_DIM,), F32)

    xf = x.reshape(batch * seq, D_MODEL)
    for l in range(depth):
        w = w_in[l]
        sp = [0, 256, 512, 768, 1024, 1408, 1420, 1676, 1932, 2444, 2460, 2972, 6044]
        mq, mk, mv, nq, nkv, ngate, gq, gk, gv, glr, gout, wmerge = (w[:, sp[i]:sp[i + 1]] for i in range(12))
        pad = jnp.zeros((D_MODEL, LANE - GLA_GATE_RANK - 3 * N_HEADS), w.dtype)
        w_a = jnp.concatenate([mq, mk, mv, nq, nkv, gq, gk, gv, gout, glr, ngate, pad], axis=1).astype(BF16)
        gw = jnp.pad(gla_gate_w[l], ((0, LANE - GLA_GATE_RANK), (0, 0))).astype(BF16)

        mqt, mk_a, mvt, nqt, kvc, nkv_a, nkvt, gqk, gv_a, la, go, ngt = _proj(
            xf, attn_norm[l][None, :], w_a, bd,
            _tile_gain(moba_q_norm[l], 4), _tile_gain(moba_k_norm[l], 4), _tile_gain(nsa_q_norm[l], 4),
            jnp.concatenate([nsa_k_norm[l, 1], ones64])[None, :], jnp.concatenate([nsa_k_norm[l, 2], ones64])[None, :],
            gw, gla_gate_b[l][None, :])

        o_moba = _moba(mqt, mk_a, mvt, bias_moba, batch, seq)

        half = NSA_CMP_STRIDE * HEAD_DIM
        zero = jnp.zeros((NSA_CMP_STRIDE, HEAD_DIM, HEAD_DIM), F32)

        def w1_part(part):
            wk = cmp_k_w1[l][part * half:(part + 1) * half].reshape(NSA_CMP_STRIDE, HEAD_DIM, HEAD_DIM)
            wv = cmp_v_w1[l][part * half:(part + 1) * half].reshape(NSA_CMP_STRIDE, HEAD_DIM, HEAD_DIM)
            top = jnp.concatenate([wk, zero], axis=2)
            bot = jnp.concatenate([zero, wv], axis=2)
            return jnp.concatenate([top, bot], axis=1).reshape(NSA_CMP_STRIDE * LANE, LANE).astype(BF16)

        def pos_part(part):
            pk = cmp_pos_k[l][part * NSA_CMP_STRIDE:(part + 1) * NSA_CMP_STRIDE]
            pv = cmp_pos_v[l][part * NSA_CMP_STRIDE:(part + 1) * NSA_CMP_STRIDE]
            return jnp.concatenate([pk, pv], axis=1).reshape(1, NSA_CMP_STRIDE * LANE).astype(F32)

        z64 = jnp.zeros((HEAD_DIM, HEAD_DIM), F32)
        w2 = jnp.concatenate([jnp.concatenate([cmp_k_w2[l], z64], axis=1),
                              jnp.concatenate([z64, cmp_v_w2[l]], axis=1)], axis=0).astype(BF16)
        kcv, kcvt = _compress(kvc.reshape(batch, n_cmp_pad, NSA_CMP_STRIDE * LANE), pos_part(0), pos_part(1),
                              w1_part(0), w1_part(1), w2, jnp.concatenate([nsa_k_norm[l, 0], ones64])[None, :])

        o_nsa = _nsa(nqt, kcv, kcvt, nkv_a, nkvt, ngt, bias_nsa, ovt, e_sel, batch, seq)
        o_gla = _gla(gqk, gv_a, la, go, tri, gla_out_norm[l][None, :].astype(F32), batch, seq)

        xf = _merge(xf, attn_norm[l][None, :], o_moba, o_nsa, o_gla, wmerge.astype(BF16),
                    w_branch_moba[l].astype(BF16), w_branch_nsa[l].astype(BF16), w_branch_gla[l].astype(BF16),
                    w_out[l].astype(BF16))

        cw = jnp.pad(conv_w[l], ((0, 8 - conv_w.shape[1]), (0, 0)))
        xf = _ffn(xf, ffn_norm[l][None, :], w_up[l][:, :D_FF].astype(BF16), w_up[l][:, D_FF:].astype(BF16),
                  cw, conv_b[l][None, :], w_down[l].astype(BF16), seq)
    return xf.reshape(batch, seq, D_MODEL)
```

```python
import functools
import math

import jax
import jax.numpy as jnp
from jax import lax
from jax.experimental import pallas as pl
from jax.experimental.pallas import tpu as pltpu

F32 = jnp.float32
BF16 = jnp.bfloat16

D_MODEL = 1024
HEAD_DIM = 64
N_HEADS = 4
MOBA_BLOCK = 256
MOBA_TOPK = 3
NSA_CMP_LEN = 32
NSA_CMP_STRIDE = 16
NSA_SEL_LEN = 64
NSA_SEL_TOPN = 16
NSA_WINDOW = 512
NSA_FORCE_BONUS = 1e4
GLA_DK = 64
GLA_DV = 128
GLA_GATE_RANK = 16
GLA_GATE_NORM = 16.0
GLA_CHUNK = 64
GLA_SUB = 16
GLA_UNROLL = 2
D_FF = 2816
REL_BUCKETS = 32
REL_MAX_DIST = 128
NORM_EPS = 1e-6
NEG_INF = -1e30

ATT_TILE = 256
LANE = 128
VMEM_LIMIT = 56 * 1024 * 1024

_C_MQ, _C_MK, _C_MV, _C_NQ = 0, 256, 512, 768
_C_KVC, _C_KSVS, _C_KWVW, _C_A_END = 1024, 1152, 1280, 1408
_MISC_NGATE = GLA_GATE_RANK


def _dot(a, b):
    return jnp.dot(a, b, preferred_element_type=F32)


def _dot_nt(a, b):
    return lax.dot_general(a, b, (((1,), (1,)), ((), ())), preferred_element_type=F32)


def _dot_tn(a, b):
    return lax.dot_general(a, b, (((0,), (0,)), ((), ())), preferred_element_type=F32)


def _rms_rows(x, gain):
    ms = jnp.mean(x * x, axis=-1, keepdims=True)
    return x * lax.rsqrt(ms + NORM_EPS) * gain


def _split_dot(x, w):
    hi = x.astype(BF16)
    lo = (x - hi.astype(F32)).astype(BF16)
    return _dot(hi, w) + _dot(lo, w)


def _params(sem):
    return pltpu.CompilerParams(dimension_semantics=sem, vmem_limit_bytes=VMEM_LIMIT)


def _const_spec(shape):
    return pl.BlockSpec(shape, lambda *_: (0,) * len(shape))


def _weight_spec(shape):
    return pl.BlockSpec(shape, lambda *_: (0,) * len(shape), pipeline_mode=pl.Buffered(1))


_W_NG, _W_GLA, _W_GLR, _W_GO, _W_MERGE, _W_END = 1408, 1420, 2444, 2460, 2972, 6044


def _wsplit_kernel(w_ref, att_ref, gla_ref, go_ref, glr_ref, ng_ref, merge_ref):
    w = w_ref[0]
    att_ref[...] = w[:, 0:_W_NG].astype(BF16)
    ng_ref[...] = w[:, _W_NG:_W_GLA].astype(BF16)
    gla_ref[...] = w[:, _W_GLA:_W_GLR].astype(BF16)
    glr_ref[...] = w[:, _W_GLR:_W_GO].astype(BF16)
    go_ref[...] = w[:, _W_GO:_W_MERGE].astype(BF16)
    merge_ref[...] = w[:, _W_MERGE:_W_END].astype(BF16)


def _wsplit(w_in, layer, tr=256):
    widths = [_W_NG, _W_GLR - _W_GLA, _W_MERGE - _W_GO, _W_GO - _W_GLR, _W_GLA - _W_NG, _W_END - _W_MERGE]
    return pl.pallas_call(
        _wsplit_kernel,
        grid=(D_MODEL // tr,),
        in_specs=[pl.BlockSpec((1, tr, _W_END), lambda i: (layer, i, 0))],
        out_specs=[pl.BlockSpec((tr, w), lambda i: (i, 0)) for w in widths],
        out_shape=[jax.ShapeDtypeStruct((D_MODEL, w), BF16) for w in widths],
        compiler_params=_params(("arbitrary",)),
        name="wsplit",
    )(w_in)


def _proj_kernel(x_ref, an_ref, wa_ref, wg_ref, wo_ref, wm_ref, bd_ref, gmq_ref, gmk_ref, gnq_ref, gks_ref, gkw_ref,
                 gw_ref, gb_ref,
                 mqt_ref, mk_ref, mvt_ref, nqt_ref, kvc_ref, nkv_ref, nkvt_ref, gqk_ref, gv_ref, la_ref, go_ref,
                 ngt_ref):
    h = _rms_rows(x_ref[...], an_ref[...]).astype(BF16)

    def sec(w_ref, lo, hi):
        return _dot(h, w_ref[:, lo:hi])

    def head_norm(y, gain):
        w = y.shape[1]
        ms = _dot((y * y).astype(BF16), bd_ref[0:w, 0:w])
        return y * lax.rsqrt(ms + NORM_EPS) * gain

    scale = HEAD_DIM ** -0.5
    mqt_ref[...] = (head_norm(sec(wa_ref, _C_MQ, _C_MK), gmq_ref[...]) * scale).T.astype(BF16)
    mk_ref[...] = head_norm(sec(wa_ref, _C_MK, _C_MV), gmk_ref[...]).astype(BF16)
    mvt_ref[...] = sec(wa_ref, _C_MV, _C_NQ).T.astype(BF16)
    nqt_ref[...] = (head_norm(sec(wa_ref, _C_NQ, _C_KVC), gnq_ref[...]) * scale).T.astype(BF16)
    kvc_ref[...] = sec(wa_ref, _C_KVC, _C_KSVS).astype(BF16)
    first_half = lax.broadcasted_iota(jnp.int32, (h.shape[0], LANE), 1) < HEAD_DIM
    ksvs = sec(wa_ref, _C_KSVS, _C_KWVW)
    ksvs = jnp.where(first_half, head_norm(ksvs, gks_ref[...]), ksvs)
    nkv_ref[:, 0:128] = ksvs.astype(BF16)
    nkvt_ref[0:128, :] = ksvs.T.astype(BF16)
    kwvw = sec(wa_ref, _C_KWVW, _C_A_END)
    kwvw = jnp.where(first_half, head_norm(kwvw, gkw_ref[...]), kwvw)
    nkv_ref[:, 128:256] = kwvw.astype(BF16)
    nkvt_ref[128:256, :] = kwvw.T.astype(BF16)
    gqk_ref[:, 0:256] = (sec(wg_ref, 0, 256) * (GLA_DK ** -0.5)).astype(BF16)
    gqk_ref[:, 256:512] = sec(wg_ref, 256, 512).astype(BF16)
    gv_ref[...] = sec(wg_ref, 512, 1024).astype(BF16)
    go_ref[...] = jax.nn.silu(_dot(h, wo_ref[...])).astype(BF16)
    misc = _dot(h, wm_ref[...])
    ngt_ref[...] = jax.nn.sigmoid(misc).T
    pre = _dot(misc.astype(BF16), gw_ref[...]) + gb_ref[...]
    la_ref[...] = (jnp.minimum(pre, 0.0) - jnp.log(1.0 + jnp.exp(-jnp.abs(pre)))) * (1.0 / GLA_GATE_NORM)


def _proj(x, an, w_att, w_gla, w_go, w_misc, bd, gmq, gmk, gnq, gks, gkw, gw, gb, tm=512):
    n = x.shape[0]
    row = lambda w: pl.BlockSpec((tm, w), lambda i: (i, 0))
    col = lambda w: pl.BlockSpec((w, tm), lambda i: (0, i))
    outs = [(256, BF16, True), (256, BF16, False), (256, BF16, True), (256, BF16, True), (128, BF16, False),
            (256, BF16, False), (256, BF16, True), (512, BF16, False), (512, BF16, False), (256, F32, False),
            (512, BF16, False), (128, F32, True)]
    return pl.pallas_call(
        _proj_kernel,
        grid=(n // tm,),
        in_specs=[row(D_MODEL), _const_spec((1, D_MODEL)), _weight_spec(w_att.shape), _weight_spec(w_gla.shape),
                  _weight_spec(w_go.shape), _weight_spec(w_misc.shape), _const_spec((256, 256)),
                  _const_spec((1, 256)), _const_spec((1, 256)), _const_spec((1, 256)), _const_spec((1, 128)),
                  _const_spec((1, 128)), _const_spec((LANE, 256)), _const_spec((1, 256))],
        out_specs=[col(w) if tr else row(w) for w, _, tr in outs],
        out_shape=[jax.ShapeDtypeStruct((w, n) if tr else (n, w), dt) for w, dt, tr in outs],
        compiler_params=_params(("arbitrary",)),
        name="proj",
    )(x, an, w_att, w_gla, w_go, w_misc, bd, gmq, gmk, gnq, gks, gkw, gw, gb)


def _softmax_step(carry, s, vt):
    m, l, acc = carry
    m_new = jnp.maximum(m, jnp.max(s, axis=0, keepdims=True))
    alpha = jnp.exp(m - m_new)
    p = jnp.exp(s - m_new)
    l = alpha * l + jnp.sum(p, axis=0, keepdims=True)
    acc = alpha * acc + _dot(vt, p.astype(BF16))
    return m_new, l, acc


def _softmax_first(s, vt):
    m = jnp.max(s, axis=0, keepdims=True)
    p = jnp.exp(s - m)
    return m, jnp.sum(p, axis=0, keepdims=True), _dot(vt, p.astype(BF16))


def _rank_rows(score, n):
    idx = lax.broadcasted_iota(jnp.int32, score.shape, 0)
    rank = jnp.zeros(score.shape, F32)
    for m in range(n):
        sm = score[m:m + 1, :]
        rank += jnp.where((sm > score) | ((sm == score) & (idx > m)), 1.0, 0.0)
    return rank


def _moba_kernel(qt_ref, k_ref, vt_ref, bias_ref, o_ref, kmean_ref, sel_ref):
    t = ATT_TILE
    nb = k_ref.shape[0] // t
    qi = pl.program_id(1)

    @pl.when(qi == 0)
    def _():
        for h in range(N_HEADS):
            kh = k_ref[:, h * HEAD_DIM:(h + 1) * HEAD_DIM].astype(F32)
            kmean_ref[h] = jnp.mean(kh.reshape(nb, t, HEAD_DIM), axis=1)

    causal = lax.broadcasted_iota(jnp.int32, (t, t), 0) <= lax.broadcasted_iota(jnp.int32, (t, t), 1)
    past = lax.broadcasted_iota(jnp.int32, (nb, t), 0) < qi
    own = pl.multiple_of(qi * t, t)
    heads = [slice(h * HEAD_DIM, (h + 1) * HEAD_DIM) for h in range(N_HEADS)]
    qts = [qt_ref[hs, :] for hs in heads]
    carries = []
    for h, hs in enumerate(heads):
        gate = jnp.where(past, _dot(kmean_ref[h].astype(BF16), qts[h]), NEG_INF)
        sel_ref[h] = jnp.where(past & (_rank_rows(gate, nb) < MOBA_TOPK), 1.0, 0.0)
        s = _dot(k_ref[pl.ds(own, t), hs], qts[h]) + bias_ref[h, 0]
        carries.append(_softmax_first(jnp.where(causal, s, NEG_INF), vt_ref[hs, pl.ds(own, t)]))

    def body(j, carries):
        keys = pl.ds(pl.multiple_of(j * t, t), t)
        apart = jnp.minimum(qi - j, 2)
        out = []
        for h, hs in enumerate(heads):
            s = _dot(k_ref[keys, hs], qts[h]) + bias_ref[h, apart]
            s = jnp.where(sel_ref[h, pl.ds(j, 1), :] > 0.5, s, NEG_INF)
            out.append(_softmax_step(carries[h], s, vt_ref[hs, keys]))
        return tuple(out)

    carries = lax.fori_loop(0, qi, body, tuple(carries))
    o_ref[...] = jnp.concatenate([acc / l for _, l, acc in carries], axis=0).T.astype(BF16)


def _moba(mqt, mk, mvt, bias, batch, seq):
    t = ATT_TILE
    nq = seq // t
    return pl.pallas_call(
        _moba_kernel,
        grid=(batch, nq),
        in_specs=[pl.BlockSpec((256, t), lambda b, i: (0, b * nq + i)),
                  pl.BlockSpec((seq, 256), lambda b, i: (b, 0)),
                  pl.BlockSpec((256, seq), lambda b, i: (0, b)),
                  _const_spec((N_HEADS, 3, t, t))],
        out_specs=pl.BlockSpec((t, 256), lambda b, i: (b * nq + i, 0)),
        out_shape=jax.ShapeDtypeStruct((batch * seq, 256), BF16),
        scratch_shapes=[pltpu.VMEM((N_HEADS, nq, HEAD_DIM), F32), pltpu.VMEM((N_HEADS, nq, t), F32)],
        compiler_params=_params(("arbitrary", "arbitrary")),
        name="moba",
    )(mqt, mk, mvt, bias)


def _compress_kernel(x_ref, pa_ref, pb_ref, w1a_ref, w1b_ref, w2_ref, gk_ref, o_ref, ot_ref):
    x = x_ref[0].astype(F32)
    u = _dot((x + pa_ref[...]).astype(BF16), w1a_ref[...])
    v = _dot((x + pb_ref[...]).astype(BF16), w1b_ref[...])
    n = u.shape[0]
    hid = u + pltpu.roll(v, n - 1, axis=0)
    y = _dot(jax.nn.gelu(hid, approximate=True).astype(BF16), w2_ref[...])
    is_k = lax.broadcasted_iota(jnp.int32, y.shape, 1) < HEAD_DIM
    ms = jnp.sum(jnp.where(is_k, y * y, 0.0), axis=1, keepdims=True) * (1.0 / HEAD_DIM)
    y = jnp.where(is_k, y * lax.rsqrt(ms + NORM_EPS) * gk_ref[...], y)
    o_ref[0] = y.astype(BF16)
    ot_ref[0] = y.T.astype(BF16)


def _compress(kvc3, pa, pb, w1a, w1b, w2, gk):
    batch, n, w = kvc3.shape
    return pl.pallas_call(
        _compress_kernel,
        grid=(batch,),
        in_specs=[pl.BlockSpec((1, n, w), lambda b: (b, 0, 0)), _const_spec((1, w)), _const_spec((1, w)),
                  _const_spec((w, LANE)), _const_spec((w, LANE)), _const_spec((LANE, LANE)), _const_spec((1, LANE))],
        out_specs=[pl.BlockSpec((1, n, LANE), lambda b: (b, 0, 0)), pl.BlockSpec((1, LANE, n), lambda b: (b, 0, 0))],
        out_shape=[jax.ShapeDtypeStruct((batch, n, LANE), BF16), jax.ShapeDtypeStruct((batch, LANE, n), BF16)],
        compiler_params=_params(("arbitrary",)),
        name="compress",
    )(kvc3, pa, pb, w1a, w1b, w2, gk)


def _nsa_kernel(qt_ref, kc_ref, kct_ref, kv_ref, kvt_ref, ngt_ref, bias_ref, ovt_ref, et_ref, o_ref):
    t = ATT_TILE
    nh = N_HEADS
    qi = pl.program_id(1)
    qt_all = qt_ref[...]
    qs = jnp.concatenate([qt_all[h * HEAD_DIM:(h + 1) * HEAD_DIM, :] for h in range(nh)], axis=1)

    kc = kc_ref[0][:, 0:HEAD_DIM]
    vct = kct_ref[0][HEAD_DIM:2 * HEAD_DIM, :]
    n_cmp = kc.shape[0]
    pos = qi * t + (lax.broadcasted_iota(jnp.int32, (n_cmp, nh * t), 1) & (t - 1))
    cend = lax.broadcasted_iota(jnp.int32, (n_cmp, nh * t), 0) * NSA_CMP_STRIDE + (NSA_CMP_LEN - 1)
    vis = cend <= pos
    sc = jnp.where(vis, _dot(kc, qs), NEG_INF)
    e = jnp.where(vis, jnp.exp(sc - jnp.max(sc, axis=0, keepdims=True)), 0.0)
    den = jnp.sum(e, axis=0, keepdims=True)
    p = e / jnp.where(den > 0.0, den, 1.0)
    o_cmp = _dot(vct, p.astype(BF16))

    p_sum = p[:, 0:t] + p[:, t:2 * t] + p[:, 2 * t:3 * t] + p[:, 3 * t:4 * t]
    p_hi = p_sum.astype(BF16)
    p_lo = (p_sum - p_hi.astype(F32)).astype(BF16)
    ovt = ovt_ref[...]
    n_blk = ovt.shape[0]
    imp = _dot(ovt, p_hi) + _dot(ovt, p_lo)
    blk = lax.broadcasted_iota(jnp.int32, (n_blk, t), 0)
    cur = (qi * t + lax.broadcasted_iota(jnp.int32, (n_blk, t), 1)) // NSA_SEL_LEN
    forced = (blk == 0) | (blk == cur) | (blk == cur - 1)
    valid = blk <= cur
    imp = jnp.where(valid, imp + jnp.where(forced, NSA_FORCE_BONUS, 0.0), NEG_INF)
    sel = jnp.where(valid & (_rank_rows(imp, n_blk) < NSA_SEL_TOPN), 1.0, 0.0).astype(BF16)

    dist0 = ((lax.broadcasted_iota(jnp.int32, (t, nh * t), 1) & (t - 1))
             - lax.broadcasted_iota(jnp.int32, (t, nh * t), 0))

    def sel_tile(j, causal):
        keys = pl.ds(pl.multiple_of(j * t, t), t)
        s = _dot(kv_ref[keys, 0:64], qs) + bias_ref[jnp.minimum(qi - j, 2)]
        picked = _dot(et_ref[j], sel)
        mask = jnp.concatenate([picked] * nh, axis=1) > 0.5
        if causal:
            mask = mask & (dist0 >= 0)
        return jnp.where(mask, s, NEG_INF), kvt_ref[64:128, keys]

    carry = _softmax_first(*sel_tile(qi, True))
    _, l, acc = lax.fori_loop(0, qi, lambda j, c: _softmax_step(c, *sel_tile(j, False)), carry)
    o_slc = acc / l

    def win_tile(d):
        keys = pl.ds(pl.multiple_of((qi - d) * t, t), t)
        s = _dot(kv_ref[keys, 128:192], qs) + bias_ref[d]
        dist = dist0 + d * t
        return jnp.where((dist >= 0) & (dist < NSA_WINDOW), s, NEG_INF), kvt_ref[192:256, keys]

    carry = _softmax_first(*win_tile(0))
    n_win = jnp.minimum(qi, NSA_WINDOW // t) + 1
    _, l, acc = lax.fori_loop(1, n_win, lambda d, c: _softmax_step(c, *win_tile(d)), carry)
    o_win = acc / l

    outs = []
    for h in range(nh):
        cs = slice(h * t, (h + 1) * t)
        g = [ngt_ref[_MISC_NGATE + 3 * h + i:_MISC_NGATE + 3 * h + i + 1, :] for i in range(3)]
        outs.append(g[0] * o_cmp[:, cs] + g[1] * o_slc[:, cs] + g[2] * o_win[:, cs])
    o_ref[...] = jnp.concatenate(outs, axis=0).T.astype(BF16)


def _nsa(nqt, kc, kct, nkv, nkvt, ngt, bias, ovt, et, batch, seq):
    t = ATT_TILE
    nq = seq // t
    n_cmp = kc.shape[1]
    n_blk = ovt.shape[0]
    return pl.pallas_call(
        _nsa_kernel,
        grid=(batch, nq),
        in_specs=[pl.BlockSpec((256, t), lambda b, i: (0, b * nq + i)),
                  pl.BlockSpec((1, n_cmp, LANE), lambda b, i: (b, 0, 0)),
                  pl.BlockSpec((1, LANE, n_cmp), lambda b, i: (b, 0, 0)),
                  pl.BlockSpec((seq, 256), lambda b, i: (b, 0)),
                  pl.BlockSpec((256, seq), lambda b, i: (0, b)),
                  pl.BlockSpec((LANE, t), lambda b, i: (0, b * nq + i)),
                  _const_spec((3, t, N_HEADS * t)),
                  _const_spec((n_blk, n_cmp)),
                  _const_spec((nq, t, n_blk))],
        out_specs=pl.BlockSpec((t, 256), lambda b, i: (b * nq + i, 0)),
        out_shape=jax.ShapeDtypeStruct((batch * seq, 256), BF16),
        compiler_params=_params(("arbitrary", "arbitrary")),
        name="nsa",
    )(nqt, kc, kct, nkv, nkvt, ngt, bias, ovt, et)


def _gla_kernel(qk_ref, v_ref, la_ref, go_ref, tri_ref, gn_ref, o_ref, st_ref):
    c = GLA_CHUNK
    nh = N_HEADS
    n_chunk = qk_ref.shape[0] // c
    st_ref[...] = jnp.zeros_like(st_ref)
    tri = tri_ref[...]
    row = lax.broadcasted_iota(jnp.int32, (c, nh * GLA_DK), 0)
    sub_causal = (lax.broadcasted_iota(jnp.int32, (GLA_SUB, c), 0)
                  - lax.broadcasted_iota(jnp.int32, (GLA_SUB, c), 1))

    def chunk(ci, _):
        rows = pl.ds(pl.multiple_of(ci * c, c), c)
        g = la_ref[rows, :]
        g1 = g.astype(BF16)
        r1 = g - g1.astype(F32)
        g2 = r1.astype(BF16)
        g3 = (r1 - g2.astype(F32)).astype(BF16)
        b = _dot(tri, g1) + _dot(tri, g2) + _dot(tri, g3)
        q = qk_ref[rows, 0:256].astype(F32)
        k = qk_ref[rows, 256:512].astype(F32)
        v = v_ref[rows, :]
        b_last = b[c - 1:c, :]
        q_inter = (q * jnp.exp(b)).astype(BF16)
        k_state = (k * jnp.exp(b_last - b)).astype(BF16)
        decay_state = jnp.exp(b_last)

        a_rows = [[] for _ in range(nh)]
        for i in range(c // GLA_SUB):
            lo, hi = i * GLA_SUB, (i + 1) * GLA_SUB
            ref_b = b[lo:lo + 1, :]
            k_i = (k * jnp.exp(jnp.where(row < hi, ref_b - b, 0.0))).astype(BF16)
            q_i = (q[lo:hi] * jnp.exp(b[lo:hi] - ref_b)).astype(BF16)
            for h in range(nh):
                hs = slice(h * GLA_DK, (h + 1) * GLA_DK)
                a = _dot_nt(q_i[:, hs], k_i[:, hs])
                a_rows[h].append(jnp.where(sub_causal + lo >= 0, a, 0.0))

        outs = []
        for h in range(nh):
            hs = slice(h * GLA_DK, (h + 1) * GLA_DK)
            vs = slice(h * GLA_DV, (h + 1) * GLA_DV)
            a = jnp.concatenate(a_rows[h], axis=0).astype(BF16)
            st = st_ref[h]
            o = _dot(a, v[:, vs]) + _dot_nt(q_inter[:, hs], st.astype(BF16))
            st_ref[h] = st * decay_state[:, hs] + _dot_tn(v[:, vs], k_state[:, hs])
            outs.append(_rms_rows(o, gn_ref[...]))
        o_ref[rows, :] = (jnp.concatenate(outs, axis=1) * go_ref[rows, :].astype(F32)).astype(BF16)
        return 0

    lax.fori_loop(0, n_chunk, chunk, 0, unroll=GLA_UNROLL)


def _gla(gqk, gv, la, go, tri, gn, batch, seq):
    spec = lambda w: pl.BlockSpec((seq, w), lambda b: (b, 0))
    return pl.pallas_call(
        _gla_kernel,
        grid=(batch,),
        in_specs=[spec(512), spec(512), spec(256), spec(512), _const_spec((GLA_CHUNK, GLA_CHUNK)),
                  _const_spec((1, GLA_DV))],
        out_specs=spec(512),
        out_shape=jax.ShapeDtypeStruct((batch * seq, 512), BF16),
        scratch_shapes=[pltpu.VMEM((N_HEADS, GLA_DV, GLA_DK), F32)],
        compiler_params=_params(("arbitrary",)),
        name="gla",
    )(gqk, gv, la, go, tri, gn)


def _merge_kernel(x_ref, an_ref, om_ref, on_ref, og_ref, wm_ref, pm_ref, pn_ref, pg_ref, wo_ref, o_ref):
    x = x_ref[...]
    h = _rms_rows(x, an_ref[...]).astype(BF16)
    z = jax.nn.sigmoid(_dot(h, wm_ref[:, 0:D_MODEL])) * _dot(om_ref[...], pm_ref[...])
    z += jax.nn.sigmoid(_dot(h, wm_ref[:, D_MODEL:2 * D_MODEL])) * _dot(on_ref[...], pn_ref[...])
    z += jax.nn.sigmoid(_dot(h, wm_ref[:, 2 * D_MODEL:3 * D_MODEL])) * _dot(og_ref[...], pg_ref[...])
    o_ref[...] = x + _dot(z.astype(BF16), wo_ref[...])


def _merge(x, an, om, on, og, wm, pm, pn, pg, wo, tm=512):
    n = x.shape[0]
    row = lambda w: pl.BlockSpec((tm, w), lambda i: (i, 0))
    return pl.pallas_call(
        _merge_kernel,
        grid=(n // tm,),
        in_specs=[row(D_MODEL), _const_spec((1, D_MODEL)), row(256), row(256), row(512),
                  _weight_spec((D_MODEL, 3 * D_MODEL)), _weight_spec((256, D_MODEL)), _weight_spec((256, D_MODEL)),
                  _weight_spec((512, D_MODEL)), _weight_spec((D_MODEL, D_MODEL))],
        out_specs=row(D_MODEL),
        out_shape=jax.ShapeDtypeStruct((n, D_MODEL), F32),
        compiler_params=_params(("arbitrary",)),
        name="merge",
    )(x, an, om, on, og, wm, pm, pn, pg, wo)


FFN_CHUNK = 256


def _ffn_kernel(x_ref, fn_ref, wa_ref, wg_ref, cw_ref, cb_ref, wd_ref, o_ref, carry_ref, act_ref, *, tiles_per_seq):
    i = pl.program_id(0)
    tm = x_ref.shape[0]
    x = x_ref[...]
    h = _rms_rows(x, fn_ref[...]).astype(BF16)
    row = lax.broadcasted_iota(jnp.int32, (tm, FFN_CHUNK), 0)

    @pl.when((i % tiles_per_seq) == 0)
    def _():
        carry_ref[...] = jnp.zeros_like(carry_ref)

    for c in range(D_FF // FFN_CHUNK):
        cs = slice(c * FFN_CHUNK, (c + 1) * FFN_CHUNK)
        a = _dot(h, wa_ref[:, cs])
        g = _dot(h, wg_ref[:, cs])
        prev = carry_ref[:, cs]
        p1 = prev[7:8, :]
        p2 = prev[6:7, :]
        a1 = jnp.where(row == 0, p1, pltpu.roll(a, 1, axis=0))
        a2 = jnp.where(row == 0, p2, jnp.where(row == 1, p1, pltpu.roll(a, 2, axis=0)))
        carry_ref[:, cs] = a[tm - 8:tm, :]
        w = cw_ref[:, cs]
        conv = w[0:1, :] * a2 + w[1:2, :] * a1 + w[2:3, :] * a + cb_ref[:, cs]
        act_ref[:, cs] = (jax.nn.gelu(conv, approximate=True) * g).astype(BF16)
    o_ref[...] = x + _dot(act_ref[...], wd_ref[...])


def _ffn(x, fn, wa, wg, cw, cb, wd, seq, tm=512):
    n = x.shape[0]
    row = pl.BlockSpec((tm, D_MODEL), lambda i: (i, 0))
    return pl.pallas_call(
        functools.partial(_ffn_kernel, tiles_per_seq=seq // tm),
        grid=(n // tm,),
        in_specs=[row, _const_spec((1, D_MODEL)), _weight_spec((D_MODEL, D_FF)), _weight_spec((D_MODEL, D_FF)),
                  _const_spec((8, D_FF)), _const_spec((1, D_FF)), _weight_spec((D_FF, D_MODEL))],
        out_specs=row,
        out_shape=jax.ShapeDtypeStruct((n, D_MODEL), F32),
        scratch_shapes=[pltpu.VMEM((8, D_FF), F32), pltpu.VMEM((tm, D_FF), BF16)],
        compiler_params=_params(("arbitrary",)),
        name="ffn",
    )(x, fn, wa, wg, cw, cb, wd)


def _rel_bucket(dist):
    n = jnp.maximum(dist, 0)
    max_exact = REL_BUCKETS // 2
    nf = jnp.maximum(n, 1).astype(F32)
    large = max_exact + (jnp.log(nf / max_exact) / math.log(REL_MAX_DIST / max_exact)
                         * (REL_BUCKETS - max_exact)).astype(jnp.int32)
    return jnp.where(n < max_exact, n, jnp.minimum(large, REL_BUCKETS - 1))


def _bias_tiles(rel_tab):
    t = ATT_TILE
    d0 = jnp.arange(t)[None, :] - jnp.arange(t)[:, None]
    bucket = jnp.stack([_rel_bucket(d0 + k * t) for k in range(3)])
    out = jnp.zeros((rel_tab.shape[1],) + bucket.shape, F32)
    for b in range(REL_BUCKETS):
        out = jnp.where(bucket[None] == b, rel_tab[b][:, None, None, None], out)
    return out


def _block_diag_mean():
    g = jnp.arange(256) // HEAD_DIM
    return jnp.where(g[:, None] == g[None, :], 1.0 / HEAD_DIM, 0.0).astype(BF16)


def _overlap_t(n_cmp_pad, n_cmp, n_blk):
    tok = jnp.arange(n_blk * NSA_SEL_LEN)
    starts = jnp.arange(n_cmp_pad) * NSA_CMP_STRIDE
    inside = (tok[None, :] >= starts[:, None]) & (tok[None, :] < starts[:, None] + NSA_CMP_LEN)
    m = inside.reshape(n_cmp_pad, n_blk, NSA_SEL_LEN).sum(-1).astype(F32) / NSA_CMP_LEN
    m = jnp.where(jnp.arange(n_cmp_pad)[:, None] < n_cmp, m, 0.0)
    return m.T.astype(BF16)


def _sel_expand_t(seq, n_blk):
    t = ATT_TILE
    key_blk = (jnp.arange(seq) // NSA_SEL_LEN).reshape(seq // t, t, 1)
    return (jnp.arange(n_blk)[None, None, :] == key_blk).astype(BF16)


def _tile_gain(g, reps):
    return jnp.tile(g.astype(F32), reps)[None, :]


def kernel(x, rel_bias, attn_norm, w_in, moba_q_norm, moba_k_norm, nsa_q_norm, nsa_k_norm, cmp_pos_k, cmp_pos_v,
           cmp_k_w1, cmp_k_w2, cmp_v_w1, cmp_v_w2, gla_gate_w, gla_gate_b, gla_out_norm, w_branch_moba,
           w_branch_nsa, w_branch_gla, w_out, ffn_norm, w_up, conv_w, conv_b, w_down):
    batch, seq, _ = x.shape
    depth = w_in.shape[0]
    n_cmp = seq // NSA_CMP_STRIDE - NSA_CMP_LEN // NSA_CMP_STRIDE + 1
    n_cmp_pad = seq // NSA_CMP_STRIDE
    n_blk = seq // NSA_SEL_LEN

    tiles = _bias_tiles(rel_bias.astype(F32))
    bias_moba = tiles[:N_HEADS]
    bias_nsa = jnp.concatenate([tiles[N_HEADS + h] for h in range(N_HEADS)], axis=2)
    bd = _block_diag_mean()
    ovt = _overlap_t(n_cmp_pad, n_cmp, n_blk)
    e_sel = _sel_expand_t(seq, n_blk)
    tri = (jnp.arange(GLA_CHUNK)[:, None] >= jnp.arange(GLA_CHUNK)[None, :]).astype(BF16)
    ones64 = jnp.ones((HEAD_DIM,), F32)

    xf = x.reshape(batch * seq, D_MODEL)
    for l in range(depth):
        w_att, w_gla, w_go, w_glr, w_ng, wmerge = _wsplit(w_in, l)
        w_misc = jnp.pad(jnp.concatenate([w_glr, w_ng], axis=1), ((0, 0), (0, LANE - GLA_GATE_RANK - 3 * N_HEADS)))
        gw = jnp.pad(gla_gate_w[l], ((0, LANE - GLA_GATE_RANK), (0, 0))).astype(BF16)

        mqt, mk_a, mvt, nqt, kvc, nkv_a, nkvt, gqk, gv_a, la, go, ngt = _proj(
            xf, attn_norm[l][None, :], w_att, w_gla, w_go, w_misc, bd,
            _tile_gain(moba_q_norm[l], 4), _tile_gain(moba_k_norm[l], 4), _tile_gain(nsa_q_norm[l], 4),
            jnp.concatenate([nsa_k_norm[l, 1], ones64])[None, :], jnp.concatenate([nsa_k_norm[l, 2], ones64])[None, :],
            gw, gla_gate_b[l][None, :])

        o_moba = _moba(mqt, mk_a, mvt, bias_moba, batch, seq)

        half = NSA_CMP_STRIDE * HEAD_DIM
        zero = jnp.zeros((NSA_CMP_STRIDE, HEAD_DIM, HEAD_DIM), F32)

        def w1_part(part):
            wk = cmp_k_w1[l][part * half:(part + 1) * half].reshape(NSA_CMP_STRIDE, HEAD_DIM, HEAD_DIM)
            wv = cmp_v_w1[l][part * half:(part + 1) * half].reshape(NSA_CMP_STRIDE, HEAD_DIM, HEAD_DIM)
            top = jnp.concatenate([wk, zero], axis=2)
            bot = jnp.concatenate([zero, wv], axis=2)
            return jnp.concatenate([top, bot], axis=1).reshape(NSA_CMP_STRIDE * LANE, LANE).astype(BF16)

        def pos_part(part):
            pk = cmp_pos_k[l][part * NSA_CMP_STRIDE:(part + 1) * NSA_CMP_STRIDE]
            pv = cmp_pos_v[l][part * NSA_CMP_STRIDE:(part + 1) * NSA_CMP_STRIDE]
            return jnp.concatenate([pk, pv], axis=1).reshape(1, NSA_CMP_STRIDE * LANE).astype(F32)

        z64 = jnp.zeros((HEAD_DIM, HEAD_DIM), F32)
        w2 = jnp.concatenate([jnp.concatenate([cmp_k_w2[l], z64], axis=1),
                              jnp.concatenate([z64, cmp_v_w2[l]], axis=1)], axis=0).astype(BF16)
        kcv, kcvt = _compress(kvc.reshape(batch, n_cmp_pad, NSA_CMP_STRIDE * LANE), pos_part(0), pos_part(1),
                              w1_part(0), w1_part(1), w2, jnp.concatenate([nsa_k_norm[l, 0], ones64])[None, :])

        o_nsa = _nsa(nqt, kcv, kcvt, nkv_a, nkvt, ngt, bias_nsa, ovt, e_sel, batch, seq)
        o_gla = _gla(gqk, gv_a, la, go, tri, gla_out_norm[l][None, :].astype(F32), batch, seq)

        xf = _merge(xf, attn_norm[l][None, :], o_moba, o_nsa, o_gla, wmerge.astype(BF16),
                    w_branch_moba[l].astype(BF16), w_branch_nsa[l].astype(BF16), w_branch_gla[l].astype(BF16),
                    w_out[l].astype(BF16))

        cw = jnp.pad(conv_w[l], ((0, 8 - conv_w.shape[1]), (0, 0)))
        xf = _ffn(xf, ffn_norm[l][None, :], w_up[l][:, :D_FF].astype(BF16), w_up[l][:, D_FF:].astype(BF16),
                  cw, conv_b[l][None, :], w_down[l].astype(BF16), seq)
    return xf.reshape(batch, seq, D_MODEL)
```

```python
import functools
import math

import jax
import jax.numpy as jnp
from jax import lax
from jax.experimental import pallas as pl
from jax.experimental.pallas import tpu as pltpu

F32 = jnp.float32
BF16 = jnp.bfloat16

D_MODEL = 1024
HEAD_DIM = 64
N_HEADS = 4
MOBA_BLOCK = 256
MOBA_TOPK = 3
NSA_CMP_LEN = 32
NSA_CMP_STRIDE = 16
NSA_SEL_LEN = 64
NSA_SEL_TOPN = 16
NSA_WINDOW = 512
NSA_FORCE_BONUS = 1e4
GLA_DK = 64
GLA_DV = 128
GLA_GATE_RANK = 16
GLA_GATE_NORM = 16.0
GLA_CHUNK = 64
GLA_SUB = 16
GLA_UNROLL = 4
D_FF = 2816
REL_BUCKETS = 32
REL_MAX_DIST = 128
NORM_EPS = 1e-6
NEG_INF = -1e30

ATT_TILE = 256
LANE = 128
VMEM_LIMIT = 56 * 1024 * 1024

_C_MQ, _C_MK, _C_MV, _C_NQ = 0, 256, 512, 768
_C_KVC, _C_KSVS, _C_KWVW, _C_A_END = 1024, 1152, 1280, 1408
_MISC_NGATE = GLA_GATE_RANK


def _dot(a, b):
    return jnp.dot(a, b, preferred_element_type=F32)


def _dot_nt(a, b):
    return lax.dot_general(a, b, (((1,), (1,)), ((), ())), preferred_element_type=F32)


def _dot_tn(a, b):
    return lax.dot_general(a, b, (((0,), (0,)), ((), ())), preferred_element_type=F32)


def _rms_rows(x, gain):
    ms = jnp.mean(x * x, axis=-1, keepdims=True)
    return x * lax.rsqrt(ms + NORM_EPS) * gain


def _split_dot(x, w):
    hi = x.astype(BF16)
    lo = (x - hi.astype(F32)).astype(BF16)
    return _dot(hi, w) + _dot(lo, w)


def _params(sem):
    return pltpu.CompilerParams(dimension_semantics=sem, vmem_limit_bytes=VMEM_LIMIT)


def _const_spec(shape):
    return pl.BlockSpec(shape, lambda *_: (0,) * len(shape))


def _weight_spec(shape):
    return pl.BlockSpec(shape, lambda *_: (0,) * len(shape), pipeline_mode=pl.Buffered(1))


_W_NG, _W_GLA, _W_GLR, _W_GO, _W_MERGE, _W_END = 1408, 1420, 2444, 2460, 2972, 6044


def _wsplit_kernel(w_ref, att_ref, gla_ref, go_ref, glr_ref, ng_ref, merge_ref):
    w = w_ref[0]
    att_ref[...] = w[:, 0:_W_NG].astype(BF16)
    ng_ref[...] = w[:, _W_NG:_W_GLA].astype(BF16)
    gla_ref[...] = w[:, _W_GLA:_W_GLR].astype(BF16)
    glr_ref[...] = w[:, _W_GLR:_W_GO].astype(BF16)
    go_ref[...] = w[:, _W_GO:_W_MERGE].astype(BF16)
    merge_ref[...] = w[:, _W_MERGE:_W_END].astype(BF16)


def _wsplit(w_in, layer, tr=256):
    widths = [_W_NG, _W_GLR - _W_GLA, _W_MERGE - _W_GO, _W_GO - _W_GLR, _W_GLA - _W_NG, _W_END - _W_MERGE]
    return pl.pallas_call(
        _wsplit_kernel,
        grid=(D_MODEL // tr,),
        in_specs=[pl.BlockSpec((1, tr, _W_END), lambda i: (layer, i, 0))],
        out_specs=[pl.BlockSpec((tr, w), lambda i: (i, 0)) for w in widths],
        out_shape=[jax.ShapeDtypeStruct((D_MODEL, w), BF16) for w in widths],
        compiler_params=_params(("arbitrary",)),
        name="wsplit",
    )(w_in)


def _proj_kernel(x_ref, an_ref, wa_ref, wg_ref, wo_ref, wm_ref, bd_ref, gmq_ref, gmk_ref, gnq_ref, gks_ref, gkw_ref,
                 gw_ref, gb_ref,
                 mqt_ref, mk_ref, mvt_ref, nqt_ref, kvc_ref, nkv_ref, nkvt_ref, gqk_ref, gv_ref, la_ref, go_ref,
                 ngt_ref, *, seq):
    tm = x_ref.shape[0]
    h = _rms_rows(x_ref[...], an_ref[...]).astype(BF16)
    pos = (pl.program_id(0) * tm + lax.broadcasted_iota(jnp.int32, (tm, LANE), 0)) % seq
    lane = lax.broadcasted_iota(jnp.int32, (tm, LANE), 1)

    def sec(w_ref, lo, hi):
        return _dot(h, w_ref[:, lo:hi])

    def head_norm(y, gain):
        w = y.shape[1]
        ms = _dot((y * y).astype(BF16), bd_ref[0:w, 0:w])
        return y * lax.rsqrt(ms + NORM_EPS) * gain

    scale = HEAD_DIM ** -0.5
    mqt_ref[...] = (head_norm(sec(wa_ref, _C_MQ, _C_MK), gmq_ref[...]) * scale).T.astype(BF16)
    mk = head_norm(sec(wa_ref, _C_MK, _C_MV), gmk_ref[...])
    pos_h = (pl.program_id(0) * tm + lax.broadcasted_iota(jnp.int32, (tm, HEAD_DIM), 0)) % seq
    lane_h = lax.broadcasted_iota(jnp.int32, (tm, HEAD_DIM), 1)
    moba_onehot = jnp.where(lane_h == pos_h // MOBA_BLOCK, 1.0, 0.0)
    for hd in range(N_HEADS):
        mk_ref[:, hd * LANE:(hd + 1) * LANE] = jnp.concatenate(
            [mk[:, hd * HEAD_DIM:(hd + 1) * HEAD_DIM], moba_onehot], axis=1).astype(BF16)
    mvt_ref[...] = sec(wa_ref, _C_MV, _C_NQ).T.astype(BF16)
    nqt_ref[...] = (head_norm(sec(wa_ref, _C_NQ, _C_KVC), gnq_ref[...]) * scale).T.astype(BF16)
    kvc_ref[...] = sec(wa_ref, _C_KVC, _C_KSVS).astype(BF16)
    first_half = lane < HEAD_DIM
    ksvs = sec(wa_ref, _C_KSVS, _C_KWVW)
    ksvs = jnp.where(first_half, head_norm(ksvs, gks_ref[...]), ksvs)
    nkvt_ref[0:128, :] = ksvs.T.astype(BF16)
    sel_onehot = jnp.where(lane - HEAD_DIM == pos // NSA_SEL_LEN, 1.0, 0.0)
    nkv_ref[:, 0:128] = jnp.where(first_half, ksvs, sel_onehot).astype(BF16)
    kwvw = sec(wa_ref, _C_KWVW, _C_A_END)
    kwvw = jnp.where(first_half, head_norm(kwvw, gkw_ref[...]), kwvw)
    nkv_ref[:, 128:256] = kwvw.astype(BF16)
    nkvt_ref[128:256, :] = kwvw.T.astype(BF16)
    gqk_ref[:, 0:256] = (sec(wg_ref, 0, 256) * (GLA_DK ** -0.5)).astype(BF16)
    gqk_ref[:, 256:512] = sec(wg_ref, 256, 512).astype(BF16)
    gv_ref[...] = sec(wg_ref, 512, 1024).astype(BF16)
    go_ref[...] = jax.nn.silu(_dot(h, wo_ref[...])).astype(BF16)
    misc = _dot(h, wm_ref[...])
    ngt_ref[...] = jax.nn.sigmoid(misc).T
    pre = _dot(misc.astype(BF16), gw_ref[...]) + gb_ref[...]
    la_ref[...] = (jnp.minimum(pre, 0.0) - jnp.log(1.0 + jnp.exp(-jnp.abs(pre)))) * (1.0 / GLA_GATE_NORM)


def _proj(x, an, w_att, w_gla, w_go, w_misc, bd, gmq, gmk, gnq, gks, gkw, gw, gb, seq, tm=512):
    n = x.shape[0]
    row = lambda w: pl.BlockSpec((tm, w), lambda i: (i, 0))
    col = lambda w: pl.BlockSpec((w, tm), lambda i: (0, i))
    outs = [(256, BF16, True), (N_HEADS * LANE, BF16, False), (256, BF16, True), (256, BF16, True),
            (128, BF16, False), (256, BF16, False), (256, BF16, True), (512, BF16, False), (512, BF16, False),
            (256, F32, False), (512, BF16, False), (128, F32, True)]
    return pl.pallas_call(
        functools.partial(_proj_kernel, seq=seq),
        grid=(n // tm,),
        in_specs=[row(D_MODEL), _const_spec((1, D_MODEL)), _weight_spec(w_att.shape), _weight_spec(w_gla.shape),
                  _weight_spec(w_go.shape), _weight_spec(w_misc.shape), _const_spec((256, 256)),
                  _const_spec((1, 256)), _const_spec((1, 256)), _const_spec((1, 256)), _const_spec((1, 128)),
                  _const_spec((1, 128)), _const_spec((LANE, 256)), _const_spec((1, 256))],
        out_specs=[col(w) if tr else row(w) for w, _, tr in outs],
        out_shape=[jax.ShapeDtypeStruct((w, n) if tr else (n, w), dt) for w, dt, tr in outs],
        compiler_params=_params(("arbitrary",)),
        name="proj",
    )(x, an, w_att, w_gla, w_go, w_misc, bd, gmq, gmk, gnq, gks, gkw, gw, gb)


def _attend_init(dv, nq):
    return (jnp.full((1, nq), NEG_INF, F32), jnp.zeros((1, nq), F32), jnp.zeros((dv, nq), F32))


def _attend(carries, scores, vts, shifts):
    stats = []
    for (m, _, _), s, shift in zip(carries, scores, shifts):
        tile_max = jnp.max(s, axis=0, keepdims=True)
        if shift is not None:
            tile_max = tile_max + shift
        m_new = jnp.maximum(m, tile_max)
        p = jnp.exp(s - (m_new if shift is None else m_new - shift))
        stats.append((m_new, jnp.exp(m - m_new), p))
    pvs = [_dot(vt, p.astype(BF16)) for vt, (_, _, p) in zip(vts, stats)]
    return tuple((m_new, alpha * l + jnp.sum(p, axis=0, keepdims=True), alpha * acc + pv)
                 for (_, l, acc), (m_new, alpha, p), pv in zip(carries, stats, pvs))


def _rank_rows(score, n):
    idx = lax.broadcasted_iota(jnp.int32, score.shape, 0)
    rank = jnp.zeros(score.shape, F32)
    for m in range(n):
        sm = score[m:m + 1, :]
        rank += jnp.where((sm > score) | ((sm == score) & (idx > m)), 1.0, 0.0)
    return rank


def _moba_kernel(qt_ref, k_ref, vt_ref, bias_ref, o_ref, kmean_ref):
    t = ATT_TILE
    nh = N_HEADS
    nb = k_ref.shape[0] // t
    qi = pl.program_id(1)
    kcols = [slice(h * LANE, (h + 1) * LANE) for h in range(nh)]
    vrows = [slice(h * HEAD_DIM, (h + 1) * HEAD_DIM) for h in range(nh)]

    @pl.when(qi == 0)
    def _():
        for h in range(nh):
            kh = k_ref[:, h * LANE:h * LANE + HEAD_DIM].astype(F32)
            kmean_ref[h] = jnp.mean(kh.reshape(nb, t, HEAD_DIM), axis=1)

    causal = lax.broadcasted_iota(jnp.int32, (t, t), 0) <= lax.broadcasted_iota(jnp.int32, (t, t), 1)
    past = lax.broadcasted_iota(jnp.int32, (nb, t), 0) < qi
    q_aug = []
    for h in range(nh):
        qt = qt_ref[vrows[h], :]
        gate = jnp.where(past, _dot(kmean_ref[h].astype(BF16), qt), NEG_INF)
        dropped = past & (_rank_rows(gate, nb) >= MOBA_TOPK)
        pen = jnp.concatenate([jnp.where(dropped, NEG_INF, 0.0), jnp.zeros((16 - nb, t), F32)], axis=0)
        q_aug.append(jnp.concatenate([qt, pen.astype(BF16), jnp.zeros((LANE - HEAD_DIM - 16, t), BF16)], axis=0))

    def tile(j, carries, bias_idx, mask):
        keys = pl.ds(pl.multiple_of(j * t, t), t)
        scores = [_dot(k_ref[keys, kcols[h]], q_aug[h]) for h in range(nh)]
        if bias_idx is None:
            shifts = [bias_ref[h, 2, 0:1, :] for h in range(nh)]
        else:
            scores = [s + bias_ref[h, bias_idx] for h, s in enumerate(scores)]
            shifts = [None] * nh
        if mask is not None:
            scores = [jnp.where(mask, s, NEG_INF) for s in scores]
        return _attend(carries, scores, [vt_ref[vrows[h], keys] for h in range(nh)], shifts)

    carries = tile(qi, tuple(_attend_init(HEAD_DIM, t) for _ in range(nh)), 0, causal)
    carries = lax.cond(qi >= 1, lambda c: tile(jnp.maximum(qi - 1, 0), c, 1, None), lambda c: c, carries)
    carries = lax.fori_loop(0, jnp.maximum(qi - 1, 0), lambda j, c: tile(j, c, None, None), carries)
    o_ref[...] = jnp.concatenate([acc / l for _, l, acc in carries], axis=0).T.astype(BF16)


def _moba(mqt, mk, mvt, bias, batch, seq):
    t = ATT_TILE
    nq = seq // t
    return pl.pallas_call(
        _moba_kernel,
        grid=(batch, nq),
        in_specs=[pl.BlockSpec((256, t), lambda b, i: (0, b * nq + i)),
                  pl.BlockSpec((seq, N_HEADS * LANE), lambda b, i: (b, 0)),
                  pl.BlockSpec((256, seq), lambda b, i: (0, b)),
                  _const_spec((N_HEADS, 3, t, t))],
        out_specs=pl.BlockSpec((t, 256), lambda b, i: (b * nq + i, 0)),
        out_shape=jax.ShapeDtypeStruct((batch * seq, 256), BF16),
        scratch_shapes=[pltpu.VMEM((N_HEADS, nq, HEAD_DIM), F32)],
        compiler_params=_params(("arbitrary", "arbitrary")),
        name="moba",
    )(mqt, mk, mvt, bias)


def _compress_kernel(x_ref, pa_ref, pb_ref, w1a_ref, w1b_ref, w2_ref, gk_ref, o_ref, ot_ref):
    x = x_ref[0].astype(F32)
    u = _dot((x + pa_ref[...]).astype(BF16), w1a_ref[...])
    v = _dot((x + pb_ref[...]).astype(BF16), w1b_ref[...])
    n = u.shape[0]
    hid = u + pltpu.roll(v, n - 1, axis=0)
    y = _dot(jax.nn.gelu(hid, approximate=True).astype(BF16), w2_ref[...])
    is_k = lax.broadcasted_iota(jnp.int32, y.shape, 1) < HEAD_DIM
    ms = jnp.sum(jnp.where(is_k, y * y, 0.0), axis=1, keepdims=True) * (1.0 / HEAD_DIM)
    y = jnp.where(is_k, y * lax.rsqrt(ms + NORM_EPS) * gk_ref[...], y)
    o_ref[0] = y.astype(BF16)
    ot_ref[0] = y.T.astype(BF16)


def _compress(kvc3, pa, pb, w1a, w1b, w2, gk):
    batch, n, w = kvc3.shape
    return pl.pallas_call(
        _compress_kernel,
        grid=(batch,),
        in_specs=[pl.BlockSpec((1, n, w), lambda b: (b, 0, 0)), _const_spec((1, w)), _const_spec((1, w)),
                  _const_spec((w, LANE)), _const_spec((w, LANE)), _const_spec((LANE, LANE)), _const_spec((1, LANE))],
        out_specs=[pl.BlockSpec((1, n, LANE), lambda b: (b, 0, 0)), pl.BlockSpec((1, LANE, n), lambda b: (b, 0, 0))],
        out_shape=[jax.ShapeDtypeStruct((batch, n, LANE), BF16), jax.ShapeDtypeStruct((batch, LANE, n), BF16)],
        compiler_params=_params(("arbitrary",)),
        name="compress",
    )(kvc3, pa, pb, w1a, w1b, w2, gk)


def _nsa_kernel(qt_ref, kc_ref, kct_ref, kv_ref, kvt_ref, ngt_ref, bias_ref, ovt_ref, o_ref):
    t = ATT_TILE
    nh = N_HEADS
    qi = pl.program_id(1)
    qt_all = qt_ref[...]
    qs = jnp.concatenate([qt_all[h * HEAD_DIM:(h + 1) * HEAD_DIM, :] for h in range(nh)], axis=1)

    kc = kc_ref[0][:, 0:HEAD_DIM]
    vct = kct_ref[0][HEAD_DIM:2 * HEAD_DIM, :]
    n_cmp = kc.shape[0]
    pos = qi * t + (lax.broadcasted_iota(jnp.int32, (n_cmp, nh * t), 1) & (t - 1))
    cend = lax.broadcasted_iota(jnp.int32, (n_cmp, nh * t), 0) * NSA_CMP_STRIDE + (NSA_CMP_LEN - 1)
    vis = cend <= pos
    sc = jnp.where(vis, _dot(kc, qs), NEG_INF)
    e = jnp.where(vis, jnp.exp(sc - jnp.max(sc, axis=0, keepdims=True)), 0.0)
    den = jnp.sum(e, axis=0, keepdims=True)
    p = e / jnp.where(den > 0.0, den, 1.0)
    o_cmp = _dot(vct, p.astype(BF16))

    p_sum = p[:, 0:t] + p[:, t:2 * t] + p[:, 2 * t:3 * t] + p[:, 3 * t:4 * t]
    p_hi = p_sum.astype(BF16)
    p_lo = (p_sum - p_hi.astype(F32)).astype(BF16)
    ovt = ovt_ref[...]
    n_blk = ovt.shape[0]
    imp = _dot(ovt, p_hi) + _dot(ovt, p_lo)
    blk = lax.broadcasted_iota(jnp.int32, (n_blk, t), 0)
    cur = (qi * t + lax.broadcasted_iota(jnp.int32, (n_blk, t), 1)) // NSA_SEL_LEN
    forced = (blk == 0) | (blk == cur) | (blk == cur - 1)
    valid = blk <= cur
    imp = jnp.where(valid, imp + jnp.where(forced, NSA_FORCE_BONUS, 0.0), NEG_INF)
    keep = valid & (_rank_rows(imp, n_blk) < NSA_SEL_TOPN)
    pen = jnp.where(keep, 0.0, NEG_INF).astype(BF16)
    q_aug = jnp.concatenate([qs, jnp.concatenate([pen] * nh, axis=1),
                             jnp.zeros((LANE - HEAD_DIM - n_blk, nh * t), BF16)], axis=0)

    dist0 = ((lax.broadcasted_iota(jnp.int32, (t, nh * t), 1) & (t - 1))
             - lax.broadcasted_iota(jnp.int32, (t, nh * t), 0))
    far_bias = bias_ref[2, 0:1, :]
    init = (_attend_init(HEAD_DIM, nh * t),)

    def tile(j, carry, k_lanes, q, v_rows, bias_idx, mask):
        keys = pl.ds(pl.multiple_of(j * t, t), t)
        s = _dot(kv_ref[keys, k_lanes], q)
        if bias_idx is not None:
            s = s + bias_ref[bias_idx]
        if mask is not None:
            s = jnp.where(mask, s, NEG_INF)
        return _attend(carry, [s], [kvt_ref[v_rows, keys]], [far_bias if bias_idx is None else None])

    def if_tile(pred, carry, fn):
        return lax.cond(pred, fn, lambda c: c, carry)

    sel_k, sel_v = slice(0, LANE), slice(HEAD_DIM, 2 * HEAD_DIM)
    carry = tile(qi, init, sel_k, q_aug, sel_v, 0, dist0 >= 0)
    carry = if_tile(qi >= 1, carry, lambda c: tile(jnp.maximum(qi - 1, 0), c, sel_k, q_aug, sel_v, 1, None))
    (_, l, acc), = lax.fori_loop(0, jnp.maximum(qi - 1, 0),
                                 lambda j, c: tile(j, c, sel_k, q_aug, sel_v, None, None), carry)
    o_slc = acc / l

    win_k, win_v = slice(LANE, LANE + HEAD_DIM), slice(LANE + HEAD_DIM, 2 * LANE)
    carry = tile(qi, init, win_k, qs, win_v, 0, dist0 >= 0)
    carry = if_tile(qi >= 1, carry, lambda c: tile(jnp.maximum(qi - 1, 0), c, win_k, qs, win_v, 1, None))
    (_, l, acc), = if_tile(qi >= 2, carry,
                           lambda c: tile(jnp.maximum(qi - 2, 0), c, win_k, qs, win_v, None, dist0 < 0))
    o_win = acc / l

    outs = []
    for h in range(nh):
        cs = slice(h * t, (h + 1) * t)
        g = [ngt_ref[_MISC_NGATE + 3 * h + i:_MISC_NGATE + 3 * h + i + 1, :] for i in range(3)]
        outs.append(g[0] * o_cmp[:, cs] + g[1] * o_slc[:, cs] + g[2] * o_win[:, cs])
    o_ref[...] = jnp.concatenate(outs, axis=0).T.astype(BF16)


def _nsa(nqt, kc, kct, nkv, nkvt, ngt, bias, ovt, batch, seq):
    t = ATT_TILE
    nq = seq // t
    assert NSA_WINDOW == 2 * t and seq // NSA_SEL_LEN <= LANE - HEAD_DIM
    n_cmp = kc.shape[1]
    n_blk = ovt.shape[0]
    return pl.pallas_call(
        _nsa_kernel,
        grid=(batch, nq),
        in_specs=[pl.BlockSpec((256, t), lambda b, i: (0, b * nq + i)),
                  pl.BlockSpec((1, n_cmp, LANE), lambda b, i: (b, 0, 0)),
                  pl.BlockSpec((1, LANE, n_cmp), lambda b, i: (b, 0, 0)),
                  pl.BlockSpec((seq, 256), lambda b, i: (b, 0)),
                  pl.BlockSpec((256, seq), lambda b, i: (0, b)),
                  pl.BlockSpec((LANE, t), lambda b, i: (0, b * nq + i)),
                  _const_spec((3, t, N_HEADS * t)),
                  _const_spec((n_blk, n_cmp))],
        out_specs=pl.BlockSpec((t, 256), lambda b, i: (b * nq + i, 0)),
        out_shape=jax.ShapeDtypeStruct((batch * seq, 256), BF16),
        compiler_params=_params(("arbitrary", "arbitrary")),
        name="nsa",
    )(nqt, kc, kct, nkv, nkvt, ngt, bias, ovt)


def _gla_kernel(qk_ref, v_ref, la_ref, go_ref, tri_ref, gn_ref, o_ref, st_ref):
    c = GLA_CHUNK
    nh = N_HEADS
    n_chunk = qk_ref.shape[0] // c
    st_ref[...] = jnp.zeros_like(st_ref)
    tri = tri_ref[...]
    row = lax.broadcasted_iota(jnp.int32, (c, nh * GLA_DK), 0)
    sub_causal = (lax.broadcasted_iota(jnp.int32, (GLA_SUB, c), 0)
                  - lax.broadcasted_iota(jnp.int32, (GLA_SUB, c), 1))

    def chunk(ci, _):
        rows = pl.ds(pl.multiple_of(ci * c, c), c)
        g = la_ref[rows, :]
        g1 = g.astype(BF16)
        r1 = g - g1.astype(F32)
        g2 = r1.astype(BF16)
        g3 = (r1 - g2.astype(F32)).astype(BF16)
        b = _dot(tri, g1) + _dot(tri, g2) + _dot(tri, g3)
        q = qk_ref[rows, 0:256].astype(F32)
        k = qk_ref[rows, 256:512].astype(F32)
        v = v_ref[rows, :]
        b_last = b[c - 1:c, :]
        q_inter = (q * jnp.exp(b)).astype(BF16)
        k_state = (k * jnp.exp(b_last - b)).astype(BF16)
        decay_state = jnp.exp(b_last)

        a_rows = [[] for _ in range(nh)]
        for i in range(c // GLA_SUB):
            lo, hi = i * GLA_SUB, (i + 1) * GLA_SUB
            ref_b = b[lo:lo + 1, :]
            k_i = (k * jnp.exp(jnp.where(row < hi, ref_b - b, 0.0))).astype(BF16)
            q_i = (q[lo:hi] * jnp.exp(b[lo:hi] - ref_b)).astype(BF16)
            for h in range(nh):
                hs = slice(h * GLA_DK, (h + 1) * GLA_DK)
                a = _dot_nt(q_i[:, hs], k_i[:, hs])
                a_rows[h].append(jnp.where(sub_causal + lo >= 0, a, 0.0))

        outs = []
        for h in range(nh):
            hs = slice(h * GLA_DK, (h + 1) * GLA_DK)
            vs = slice(h * GLA_DV, (h + 1) * GLA_DV)
            a = jnp.concatenate(a_rows[h], axis=0).astype(BF16)
            st = st_ref[h]
            o = _dot(a, v[:, vs]) + _dot_nt(q_inter[:, hs], st.astype(BF16))
            st_ref[h] = st * decay_state[:, hs] + _dot_tn(v[:, vs], k_state[:, hs])
            outs.append(_rms_rows(o, gn_ref[...]))
        o_ref[rows, :] = (jnp.concatenate(outs, axis=1) * go_ref[rows, :].astype(F32)).astype(BF16)
        return 0

    lax.fori_loop(0, n_chunk, chunk, 0, unroll=GLA_UNROLL)


def _gla(gqk, gv, la, go, tri, gn, batch, seq):
    spec = lambda w: pl.BlockSpec((seq, w), lambda b: (b, 0))
    return pl.pallas_call(
        _gla_kernel,
        grid=(batch,),
        in_specs=[spec(512), spec(512), spec(256), spec(512), _const_spec((GLA_CHUNK, GLA_CHUNK)),
                  _const_spec((1, GLA_DV))],
        out_specs=spec(512),
        out_shape=jax.ShapeDtypeStruct((batch * seq, 512), BF16),
        scratch_shapes=[pltpu.VMEM((N_HEADS, GLA_DV, GLA_DK), F32)],
        compiler_params=_params(("arbitrary",)),
        name="gla",
    )(gqk, gv, la, go, tri, gn)


def _merge_kernel(x_ref, an_ref, om_ref, on_ref, og_ref, wm_ref, pm_ref, pn_ref, pg_ref, wo_ref, o_ref):
    x = x_ref[...]
    h = _rms_rows(x, an_ref[...]).astype(BF16)
    z = jax.nn.sigmoid(_dot(h, wm_ref[:, 0:D_MODEL])) * _dot(om_ref[...], pm_ref[...])
    z += jax.nn.sigmoid(_dot(h, wm_ref[:, D_MODEL:2 * D_MODEL])) * _dot(on_ref[...], pn_ref[...])
    z += jax.nn.sigmoid(_dot(h, wm_ref[:, 2 * D_MODEL:3 * D_MODEL])) * _dot(og_ref[...], pg_ref[...])
    o_ref[...] = x + _dot(z.astype(BF16), wo_ref[...])


def _merge(x, an, om, on, og, wm, pm, pn, pg, wo, tm=512):
    n = x.shape[0]
    row = lambda w: pl.BlockSpec((tm, w), lambda i: (i, 0))
    return pl.pallas_call(
        _merge_kernel,
        grid=(n // tm,),
        in_specs=[row(D_MODEL), _const_spec((1, D_MODEL)), row(256), row(256), row(512),
                  _weight_spec((D_MODEL, 3 * D_MODEL)), _weight_spec((256, D_MODEL)), _weight_spec((256, D_MODEL)),
                  _weight_spec((512, D_MODEL)), _weight_spec((D_MODEL, D_MODEL))],
        out_specs=row(D_MODEL),
        out_shape=jax.ShapeDtypeStruct((n, D_MODEL), F32),
        compiler_params=_params(("arbitrary",)),
        name="merge",
    )(x, an, om, on, og, wm, pm, pn, pg, wo)


FFN_CHUNK = 256


def _ffn_kernel(x_ref, fn_ref, wa_ref, wg_ref, cw_ref, cb_ref, wd_ref, o_ref, carry_ref, act_ref, *, tiles_per_seq):
    i = pl.program_id(0)
    tm = x_ref.shape[0]
    x = x_ref[...]
    h = _rms_rows(x, fn_ref[...]).astype(BF16)
    row = lax.broadcasted_iota(jnp.int32, (tm, FFN_CHUNK), 0)

    @pl.when((i % tiles_per_seq) == 0)
    def _():
        carry_ref[...] = jnp.zeros_like(carry_ref)

    for c in range(D_FF // FFN_CHUNK):
        cs = slice(c * FFN_CHUNK, (c + 1) * FFN_CHUNK)
        a = _dot(h, wa_ref[:, cs])
        g = _dot(h, wg_ref[:, cs])
        prev = carry_ref[:, cs]
        p1 = prev[7:8, :]
        p2 = prev[6:7, :]
        a1 = jnp.where(row == 0, p1, pltpu.roll(a, 1, axis=0))
        a2 = jnp.where(row == 0, p2, jnp.where(row == 1, p1, pltpu.roll(a, 2, axis=0)))
        carry_ref[:, cs] = a[tm - 8:tm, :]
        w = cw_ref[:, cs]
        conv = w[0:1, :] * a2 + w[1:2, :] * a1 + w[2:3, :] * a + cb_ref[:, cs]
        act_ref[:, cs] = (jax.nn.gelu(conv, approximate=True) * g).astype(BF16)
    o_ref[...] = x + _dot(act_ref[...], wd_ref[...])


def _ffn(x, fn, wa, wg, cw, cb, wd, seq, tm=512):
    n = x.shape[0]
    row = pl.BlockSpec((tm, D_MODEL), lambda i: (i, 0))
    return pl.pallas_call(
        functools.partial(_ffn_kernel, tiles_per_seq=seq // tm),
        grid=(n // tm,),
        in_specs=[row, _const_spec((1, D_MODEL)), _weight_spec((D_MODEL, D_FF)), _weight_spec((D_MODEL, D_FF)),
                  _const_spec((8, D_FF)), _const_spec((1, D_FF)), _weight_spec((D_FF, D_MODEL))],
        out_specs=row,
        out_shape=jax.ShapeDtypeStruct((n, D_MODEL), F32),
        scratch_shapes=[pltpu.VMEM((8, D_FF), F32), pltpu.VMEM((tm, D_FF), BF16)],
        compiler_params=_params(("arbitrary",)),
        name="ffn",
    )(x, fn, wa, wg, cw, cb, wd)


def _rel_bucket(dist):
    n = jnp.maximum(dist, 0)
    max_exact = REL_BUCKETS // 2
    nf = jnp.maximum(n, 1).astype(F32)
    large = max_exact + (jnp.log(nf / max_exact) / math.log(REL_MAX_DIST / max_exact)
                         * (REL_BUCKETS - max_exact)).astype(jnp.int32)
    return jnp.where(n < max_exact, n, jnp.minimum(large, REL_BUCKETS - 1))


def _bias_tiles(rel_tab):
    t = ATT_TILE
    d0 = jnp.arange(t)[None, :] - jnp.arange(t)[:, None]
    bucket = jnp.stack([_rel_bucket(d0 + k * t) for k in range(3)])
    out = jnp.zeros((rel_tab.shape[1],) + bucket.shape, F32)
    for b in range(REL_BUCKETS):
        out = jnp.where(bucket[None] == b, rel_tab[b][:, None, None, None], out)
    return out


def _block_diag_mean():
    g = jnp.arange(256) // HEAD_DIM
    return jnp.where(g[:, None] == g[None, :], 1.0 / HEAD_DIM, 0.0).astype(BF16)


def _overlap_t(n_cmp_pad, n_cmp, n_blk):
    tok = jnp.arange(n_blk * NSA_SEL_LEN)
    starts = jnp.arange(n_cmp_pad) * NSA_CMP_STRIDE
    inside = (tok[None, :] >= starts[:, None]) & (tok[None, :] < starts[:, None] + NSA_CMP_LEN)
    m = inside.reshape(n_cmp_pad, n_blk, NSA_SEL_LEN).sum(-1).astype(F32) / NSA_CMP_LEN
    m = jnp.where(jnp.arange(n_cmp_pad)[:, None] < n_cmp, m, 0.0)
    return m.T.astype(BF16)


def _tile_gain(g, reps):
    return jnp.tile(g.astype(F32), reps)[None, :]


def kernel(x, rel_bias, attn_norm, w_in, moba_q_norm, moba_k_norm, nsa_q_norm, nsa_k_norm, cmp_pos_k, cmp_pos_v,
           cmp_k_w1, cmp_k_w2, cmp_v_w1, cmp_v_w2, gla_gate_w, gla_gate_b, gla_out_norm, w_branch_moba,
           w_branch_nsa, w_branch_gla, w_out, ffn_norm, w_up, conv_w, conv_b, w_down):
    batch, seq, _ = x.shape
    depth = w_in.shape[0]
    n_cmp = seq // NSA_CMP_STRIDE - NSA_CMP_LEN // NSA_CMP_STRIDE + 1
    n_cmp_pad = seq // NSA_CMP_STRIDE
    n_blk = seq // NSA_SEL_LEN

    tiles = _bias_tiles(rel_bias.astype(F32))
    bias_moba = tiles[:N_HEADS]
    bias_nsa = jnp.concatenate([tiles[N_HEADS + h] for h in range(N_HEADS)], axis=2)
    bd = _block_diag_mean()
    ovt = _overlap_t(n_cmp_pad, n_cmp, n_blk)
    tri = (jnp.arange(GLA_CHUNK)[:, None] >= jnp.arange(GLA_CHUNK)[None, :]).astype(BF16)
    ones64 = jnp.ones((HEAD_DIM,), F32)

    xf = x.reshape(batch * seq, D_MODEL)
    for l in range(depth):
        w_att, w_gla, w_go, w_glr, w_ng, wmerge = _wsplit(w_in, l)
        w_misc = jnp.pad(jnp.concatenate([w_glr, w_ng], axis=1), ((0, 0), (0, LANE - GLA_GATE_RANK - 3 * N_HEADS)))
        gw = jnp.pad(gla_gate_w[l], ((0, LANE - GLA_GATE_RANK), (0, 0))).astype(BF16)

        mqt, mk_a, mvt, nqt, kvc, nkv_a, nkvt, gqk, gv_a, la, go, ngt = _proj(
            xf, attn_norm[l][None, :], w_att, w_gla, w_go, w_misc, bd,
            _tile_gain(moba_q_norm[l], 4), _tile_gain(moba_k_norm[l], 4), _tile_gain(nsa_q_norm[l], 4),
            jnp.concatenate([nsa_k_norm[l, 1], ones64])[None, :], jnp.concatenate([nsa_k_norm[l, 2], ones64])[None, :],
            gw, gla_gate_b[l][None, :], seq)

        o_moba = _moba(mqt, mk_a, mvt, bias_moba, batch, seq)

        half = NSA_CMP_STRIDE * HEAD_DIM
        zero = jnp.zeros((NSA_CMP_STRIDE, HEAD_DIM, HEAD_DIM), F32)

        def w1_part(part):
            wk = cmp_k_w1[l][part * half:(part + 1) * half].reshape(NSA_CMP_STRIDE, HEAD_DIM, HEAD_DIM)
            wv = cmp_v_w1[l][part * half:(part + 1) * half].reshape(NSA_CMP_STRIDE, HEAD_DIM, HEAD_DIM)
            top = jnp.concatenate([wk, zero], axis=2)
            bot = jnp.concatenate([zero, wv], axis=2)
            return jnp.concatenate([top, bot], axis=1).reshape(NSA_CMP_STRIDE * LANE, LANE).astype(BF16)

        def pos_part(part):
            pk = cmp_pos_k[l][part * NSA_CMP_STRIDE:(part + 1) * NSA_CMP_STRIDE]
            pv = cmp_pos_v[l][part * NSA_CMP_STRIDE:(part + 1) * NSA_CMP_STRIDE]
            return jnp.concatenate([pk, pv], axis=1).reshape(1, NSA_CMP_STRIDE * LANE).astype(F32)

        z64 = jnp.zeros((HEAD_DIM, HEAD_DIM), F32)
        w2 = jnp.concatenate([jnp.concatenate([cmp_k_w2[l], z64], axis=1),
                              jnp.concatenate([z64, cmp_v_w2[l]], axis=1)], axis=0).astype(BF16)
        kcv, kcvt = _compress(kvc.reshape(batch, n_cmp_pad, NSA_CMP_STRIDE * LANE), pos_part(0), pos_part(1),
                              w1_part(0), w1_part(1), w2, jnp.concatenate([nsa_k_norm[l, 0], ones64])[None, :])

        o_nsa = _nsa(nqt, kcv, kcvt, nkv_a, nkvt, ngt, bias_nsa, ovt, batch, seq)
        o_gla = _gla(gqk, gv_a, la, go, tri, gla_out_norm[l][None, :].astype(F32), batch, seq)

        xf = _merge(xf, attn_norm[l][None, :], o_moba, o_nsa, o_gla, wmerge.astype(BF16),
                    w_branch_moba[l].astype(BF16), w_branch_nsa[l].astype(BF16), w_branch_gla[l].astype(BF16),
                    w_out[l].astype(BF16))

        cw = jnp.pad(conv_w[l], ((0, 8 - conv_w.shape[1]), (0, 0)))
        xf = _ffn(xf, ffn_norm[l][None, :], w_up[l][:, :D_FF].astype(BF16), w_up[l][:, D_FF:].astype(BF16),
                  cw, conv_b[l][None, :], w_down[l].astype(BF16), seq)
    return xf.reshape(batch, seq, D_MODEL)
```

```python
import functools
import math

import jax
import jax.numpy as jnp
from jax import lax
from jax.experimental import pallas as pl
from jax.experimental.pallas import tpu as pltpu

F32 = jnp.float32
BF16 = jnp.bfloat16

D_MODEL = 1024
HEAD_DIM = 64
N_HEADS = 4
MOBA_BLOCK = 256
MOBA_TOPK = 3
NSA_CMP_LEN = 32
NSA_CMP_STRIDE = 16
NSA_SEL_LEN = 64
NSA_SEL_TOPN = 16
NSA_WINDOW = 512
NSA_FORCE_BONUS = 1e4
GLA_DK = 64
GLA_DV = 128
GLA_GATE_RANK = 16
GLA_GATE_NORM = 16.0
GLA_CHUNK = 64
GLA_SUB = 16
GLA_UNROLL = 8
D_FF = 2816
REL_BUCKETS = 32
REL_MAX_DIST = 128
NORM_EPS = 1e-6
NEG_INF = -1e30
LOG2_E = math.log2(math.e)

ATT_TILE = 256
LANE = 128
VMEM_LIMIT = 56 * 1024 * 1024

_C_MQ, _C_MK, _C_MV, _C_NQ = 0, 256, 512, 768
_C_KVC, _C_KSVS, _C_KWVW, _C_A_END = 1024, 1152, 1280, 1408
_MISC_NGATE = GLA_GATE_RANK


def _dot(a, b):
    return jnp.dot(a, b, preferred_element_type=F32)


def _dot_nt(a, b):
    return lax.dot_general(a, b, (((1,), (1,)), ((), ())), preferred_element_type=F32)


def _dot_tn(a, b):
    return lax.dot_general(a, b, (((0,), (0,)), ((), ())), preferred_element_type=F32)


def _rms_rows(x, gain):
    ms = jnp.mean(x * x, axis=-1, keepdims=True)
    return x * lax.rsqrt(ms + NORM_EPS) * gain


def _split_dot(x, w):
    hi = x.astype(BF16)
    lo = (x - hi.astype(F32)).astype(BF16)
    return _dot(hi, w) + _dot(lo, w)


def _params(sem):
    return pltpu.CompilerParams(dimension_semantics=sem, vmem_limit_bytes=VMEM_LIMIT)


def _const_spec(shape):
    return pl.BlockSpec(shape, lambda *_: (0,) * len(shape))


def _weight_spec(shape):
    return pl.BlockSpec(shape, lambda *_: (0,) * len(shape), pipeline_mode=pl.Buffered(1))


_W_NG, _W_GLA, _W_GLR, _W_GO, _W_MERGE, _W_END = 1408, 1420, 2444, 2460, 2972, 6044


def _wsplit_kernel(w_ref, att_ref, gla_ref, go_ref, glr_ref, ng_ref, merge_ref):
    w = w_ref[0]
    att_ref[...] = w[:, 0:_W_NG].astype(BF16)
    ng_ref[...] = w[:, _W_NG:_W_GLA].astype(BF16)
    gla_ref[...] = w[:, _W_GLA:_W_GLR].astype(BF16)
    glr_ref[...] = w[:, _W_GLR:_W_GO].astype(BF16)
    go_ref[...] = w[:, _W_GO:_W_MERGE].astype(BF16)
    merge_ref[...] = w[:, _W_MERGE:_W_END].astype(BF16)


def _wsplit(w_in, layer, tr=256):
    widths = [_W_NG, _W_GLR - _W_GLA, _W_MERGE - _W_GO, _W_GO - _W_GLR, _W_GLA - _W_NG, _W_END - _W_MERGE]
    return pl.pallas_call(
        _wsplit_kernel,
        grid=(D_MODEL // tr,),
        in_specs=[pl.BlockSpec((1, tr, _W_END), lambda i: (layer, i, 0))],
        out_specs=[pl.BlockSpec((tr, w), lambda i: (i, 0)) for w in widths],
        out_shape=[jax.ShapeDtypeStruct((D_MODEL, w), BF16) for w in widths],
        compiler_params=_params(("arbitrary",)),
        name="wsplit",
    )(w_in)


def _proj_kernel(x_ref, an_ref, wa_ref, wg_ref, wo_ref, wm_ref, bd_ref, gmq_ref, gmk_ref, gnq_ref, gks_ref, gkw_ref,
                 gw_ref, gb_ref,
                 mqt_ref, mk_ref, mvt_ref, nqt_ref, kvc_ref, nkv_ref, nkvt_ref, gqk_ref, gv_ref, la_ref, go_ref,
                 ngt_ref, *, seq):
    tm = x_ref.shape[0]
    h = _rms_rows(x_ref[...], an_ref[...]).astype(BF16)
    pos = (pl.program_id(0) * tm + lax.broadcasted_iota(jnp.int32, (tm, LANE), 0)) % seq
    lane = lax.broadcasted_iota(jnp.int32, (tm, LANE), 1)

    def sec(w_ref, lo, hi):
        return _dot(h, w_ref[:, lo:hi])

    def head_norm(y, gain):
        w = y.shape[1]
        ms = _dot((y * y).astype(BF16), bd_ref[0:w, 0:w])
        return y * lax.rsqrt(ms + NORM_EPS) * gain

    scale = HEAD_DIM ** -0.5 * LOG2_E
    mqt_ref[...] = (head_norm(sec(wa_ref, _C_MQ, _C_MK), gmq_ref[...]) * scale).T.astype(BF16)
    mk = head_norm(sec(wa_ref, _C_MK, _C_MV), gmk_ref[...])
    pos_h = (pl.program_id(0) * tm + lax.broadcasted_iota(jnp.int32, (tm, HEAD_DIM), 0)) % seq
    lane_h = lax.broadcasted_iota(jnp.int32, (tm, HEAD_DIM), 1)
    moba_onehot = jnp.where(lane_h == pos_h // MOBA_BLOCK, 1.0, 0.0)
    for hd in range(N_HEADS):
        mk_ref[:, hd * LANE:(hd + 1) * LANE] = jnp.concatenate(
            [mk[:, hd * HEAD_DIM:(hd + 1) * HEAD_DIM], moba_onehot], axis=1).astype(BF16)
    mvt_ref[...] = sec(wa_ref, _C_MV, _C_NQ).T.astype(BF16)
    nqt_ref[...] = (head_norm(sec(wa_ref, _C_NQ, _C_KVC), gnq_ref[...]) * scale).T.astype(BF16)
    kvc_ref[...] = sec(wa_ref, _C_KVC, _C_KSVS).astype(BF16)
    first_half = lane < HEAD_DIM
    ksvs = sec(wa_ref, _C_KSVS, _C_KWVW)
    ksvs = jnp.where(first_half, head_norm(ksvs, gks_ref[...]), ksvs)
    nkvt_ref[0:128, :] = ksvs.T.astype(BF16)
    sel_onehot = jnp.where(lane - HEAD_DIM == pos // NSA_SEL_LEN, 1.0, 0.0)
    nkv_ref[:, 0:128] = jnp.where(first_half, ksvs, sel_onehot).astype(BF16)
    kwvw = sec(wa_ref, _C_KWVW, _C_A_END)
    kwvw = jnp.where(first_half, head_norm(kwvw, gkw_ref[...]), kwvw)
    nkv_ref[:, 128:256] = kwvw.astype(BF16)
    nkvt_ref[128:256, :] = kwvw.T.astype(BF16)
    gqk_ref[:, 0:256] = (sec(wg_ref, 0, 256) * (GLA_DK ** -0.5)).astype(BF16)
    gqk_ref[:, 256:512] = sec(wg_ref, 256, 512).astype(BF16)
    gv_ref[...] = sec(wg_ref, 512, 1024).astype(BF16)
    go_ref[...] = jax.nn.silu(_dot(h, wo_ref[...])).astype(BF16)
    misc = _dot(h, wm_ref[...])
    ngt_ref[...] = jax.nn.sigmoid(misc).T
    pre = _dot(misc.astype(BF16), gw_ref[...]) + gb_ref[...]
    la_ref[...] = (jnp.minimum(pre, 0.0) - jnp.log(1.0 + jnp.exp(-jnp.abs(pre)))) * (1.0 / GLA_GATE_NORM)


def _proj(x, an, w_att, w_gla, w_go, w_misc, bd, gmq, gmk, gnq, gks, gkw, gw, gb, seq, tm=512):
    n = x.shape[0]
    row = lambda w: pl.BlockSpec((tm, w), lambda i: (i, 0))
    col = lambda w: pl.BlockSpec((w, tm), lambda i: (0, i))
    outs = [(256, BF16, True), (N_HEADS * LANE, BF16, False), (256, BF16, True), (256, BF16, True),
            (128, BF16, False), (256, BF16, False), (256, BF16, True), (512, BF16, False), (512, BF16, False),
            (256, F32, False), (512, BF16, False), (128, F32, True)]
    return pl.pallas_call(
        functools.partial(_proj_kernel, seq=seq),
        grid=(n // tm,),
        in_specs=[row(D_MODEL), _const_spec((1, D_MODEL)), _weight_spec(w_att.shape), _weight_spec(w_gla.shape),
                  _weight_spec(w_go.shape), _weight_spec(w_misc.shape), _const_spec((256, 256)),
                  _const_spec((1, 256)), _const_spec((1, 256)), _const_spec((1, 256)), _const_spec((1, 128)),
                  _const_spec((1, 128)), _const_spec((LANE, 256)), _const_spec((1, 256))],
        out_specs=[col(w) if tr else row(w) for w, _, tr in outs],
        out_shape=[jax.ShapeDtypeStruct((w, n) if tr else (n, w), dt) for w, dt, tr in outs],
        compiler_params=_params(("arbitrary",)),
        name="proj",
    )(x, an, w_att, w_gla, w_go, w_misc, bd, gmq, gmk, gnq, gks, gkw, gw, gb)


def _attend_init(dv, nq):
    return (jnp.full((1, nq), NEG_INF, F32), jnp.zeros((1, nq), F32), jnp.zeros((dv, nq), F32))


def _attend(carries, scores, vts, shifts):
    stats = []
    for (m, _, _), s, shift in zip(carries, scores, shifts):
        tile_max = jnp.max(s, axis=0, keepdims=True)
        if shift is not None:
            tile_max = tile_max + shift
        m_new = jnp.maximum(m, tile_max)
        p = jnp.exp2(s - (m_new if shift is None else m_new - shift))
        stats.append((m_new, jnp.exp2(m - m_new), p))
    pvs = [_dot(vt, p.astype(BF16)) for vt, (_, _, p) in zip(vts, stats)]
    return tuple((m_new, alpha * l + jnp.sum(p, axis=0, keepdims=True), alpha * acc + pv)
                 for (_, l, acc), (m_new, alpha, p), pv in zip(carries, stats, pvs))


def _rank_rows(score, n):
    idx = lax.broadcasted_iota(jnp.int32, score.shape, 0)
    rank = jnp.zeros(score.shape, F32)
    for m in range(n):
        sm = score[m:m + 1, :]
        rank += jnp.where((sm > score) | ((sm == score) & (idx > m)), 1.0, 0.0)
    return rank


def _moba_kernel(qt_ref, k_ref, vt_ref, bias_ref, o_ref, kmean_ref):
    t = ATT_TILE
    nh = N_HEADS
    nb = k_ref.shape[0] // t
    qi = pl.program_id(1)
    kcols = [slice(h * LANE, (h + 1) * LANE) for h in range(nh)]
    vrows = [slice(h * HEAD_DIM, (h + 1) * HEAD_DIM) for h in range(nh)]

    @pl.when(qi == 0)
    def _():
        for h in range(nh):
            kh = k_ref[:, h * LANE:h * LANE + HEAD_DIM].astype(F32)
            kmean_ref[h] = jnp.mean(kh.reshape(nb, t, HEAD_DIM), axis=1)

    past = lax.broadcasted_iota(jnp.int32, (nb, t), 0) < qi
    q_aug = []
    for h in range(nh):
        qt = qt_ref[vrows[h], :]
        gate = jnp.where(past, _dot(kmean_ref[h].astype(BF16), qt), NEG_INF)
        dropped = past & (_rank_rows(gate, nb) >= MOBA_TOPK)
        pen = jnp.concatenate([jnp.where(dropped, NEG_INF, 0.0), jnp.zeros((16 - nb, t), F32)], axis=0)
        q_aug.append(jnp.concatenate([qt, pen.astype(BF16), jnp.zeros((LANE - HEAD_DIM - 16, t), BF16)], axis=0))

    def tile(j, carries, bias_idx):
        keys = pl.ds(pl.multiple_of(j * t, t), t)
        scores = [_dot(k_ref[keys, kcols[h]], q_aug[h]) for h in range(nh)]
        if bias_idx is None:
            shifts = [bias_ref[h, 2, 0:1, :] for h in range(nh)]
        else:
            scores = [s + bias_ref[h, bias_idx] for h, s in enumerate(scores)]
            shifts = [None] * nh
        return _attend(carries, scores, [vt_ref[vrows[h], keys] for h in range(nh)], shifts)

    carries = tile(qi, tuple(_attend_init(HEAD_DIM, t) for _ in range(nh)), 0)
    carries = lax.cond(qi >= 1, lambda c: tile(jnp.maximum(qi - 1, 0), c, 1), lambda c: c, carries)
    carries = lax.fori_loop(0, jnp.maximum(qi - 1, 0), lambda j, c: tile(j, c, None), carries)
    o_ref[...] = jnp.concatenate([acc / l for _, l, acc in carries], axis=0).T.astype(BF16)


def _moba(mqt, mk, mvt, bias, batch, seq):
    t = ATT_TILE
    nq = seq // t
    return pl.pallas_call(
        _moba_kernel,
        grid=(batch, nq),
        in_specs=[pl.BlockSpec((256, t), lambda b, i: (0, b * nq + i)),
                  pl.BlockSpec((seq, N_HEADS * LANE), lambda b, i: (b, 0)),
                  pl.BlockSpec((256, seq), lambda b, i: (0, b)),
                  _const_spec((N_HEADS, 3, t, t))],
        out_specs=pl.BlockSpec((t, 256), lambda b, i: (b * nq + i, 0)),
        out_shape=jax.ShapeDtypeStruct((batch * seq, 256), BF16),
        scratch_shapes=[pltpu.VMEM((N_HEADS, nq, HEAD_DIM), F32)],
        compiler_params=_params(("arbitrary", "arbitrary")),
        name="moba",
    )(mqt, mk, mvt, bias)


def _compress_kernel(x_ref, pa_ref, pb_ref, w1a_ref, w1b_ref, w2_ref, gk_ref, o_ref, ot_ref):
    x = x_ref[0].astype(F32)
    u = _dot((x + pa_ref[...]).astype(BF16), w1a_ref[...])
    v = _dot((x + pb_ref[...]).astype(BF16), w1b_ref[...])
    n = u.shape[0]
    hid = u + pltpu.roll(v, n - 1, axis=0)
    y = _dot(jax.nn.gelu(hid, approximate=True).astype(BF16), w2_ref[...])
    is_k = lax.broadcasted_iota(jnp.int32, y.shape, 1) < HEAD_DIM
    ms = jnp.sum(jnp.where(is_k, y * y, 0.0), axis=1, keepdims=True) * (1.0 / HEAD_DIM)
    y = jnp.where(is_k, y * lax.rsqrt(ms + NORM_EPS) * gk_ref[...], y)
    o_ref[0] = y.astype(BF16)
    ot_ref[0] = y.T.astype(BF16)


def _compress(kvc3, pa, pb, w1a, w1b, w2, gk):
    batch, n, w = kvc3.shape
    return pl.pallas_call(
        _compress_kernel,
        grid=(batch,),
        in_specs=[pl.BlockSpec((1, n, w), lambda b: (b, 0, 0)), _const_spec((1, w)), _const_spec((1, w)),
                  _const_spec((w, LANE)), _const_spec((w, LANE)), _const_spec((LANE, LANE)), _const_spec((1, LANE))],
        out_specs=[pl.BlockSpec((1, n, LANE), lambda b: (b, 0, 0)), pl.BlockSpec((1, LANE, n), lambda b: (b, 0, 0))],
        out_shape=[jax.ShapeDtypeStruct((batch, n, LANE), BF16), jax.ShapeDtypeStruct((batch, LANE, n), BF16)],
        compiler_params=_params(("arbitrary",)),
        name="compress",
    )(kvc3, pa, pb, w1a, w1b, w2, gk)


def _nsa_kernel(qt_ref, kc_ref, kct_ref, kv_ref, kvt_ref, ngt_ref, bias_ref, ovt_ref, o_ref):
    t = ATT_TILE
    nh = N_HEADS
    qi = pl.program_id(1)
    qt_all = qt_ref[...]
    qs = jnp.concatenate([qt_all[h * HEAD_DIM:(h + 1) * HEAD_DIM, :] for h in range(nh)], axis=1)

    kc = kc_ref[0][:, 0:HEAD_DIM]
    vct = kct_ref[0][HEAD_DIM:2 * HEAD_DIM, :]
    n_cmp = kc.shape[0]
    pos = qi * t + (lax.broadcasted_iota(jnp.int32, (n_cmp, nh * t), 1) & (t - 1))
    cend = lax.broadcasted_iota(jnp.int32, (n_cmp, nh * t), 0) * NSA_CMP_STRIDE + (NSA_CMP_LEN - 1)
    vis = cend <= pos
    sc = jnp.where(vis, _dot(kc, qs), NEG_INF)
    e = jnp.where(vis, jnp.exp2(sc - jnp.max(sc, axis=0, keepdims=True)), 0.0)
    den = jnp.sum(e, axis=0, keepdims=True)
    p = e / jnp.where(den > 0.0, den, 1.0)
    o_cmp = _dot(vct, p.astype(BF16))

    p_sum = p[:, 0:t] + p[:, t:2 * t] + p[:, 2 * t:3 * t] + p[:, 3 * t:4 * t]
    p_hi = p_sum.astype(BF16)
    p_lo = (p_sum - p_hi.astype(F32)).astype(BF16)
    ovt = ovt_ref[...]
    n_blk = ovt.shape[0]
    imp = _dot(ovt, p_hi) + _dot(ovt, p_lo)
    blk = lax.broadcasted_iota(jnp.int32, (n_blk, t), 0)
    cur = (qi * t + lax.broadcasted_iota(jnp.int32, (n_blk, t), 1)) // NSA_SEL_LEN
    forced = (blk == 0) | (blk == cur) | (blk == cur - 1)
    valid = blk <= cur
    imp = jnp.where(valid, imp + jnp.where(forced, NSA_FORCE_BONUS, 0.0), NEG_INF)
    keep = valid & (_rank_rows(imp, n_blk) < NSA_SEL_TOPN)
    pen = jnp.where(keep, 0.0, NEG_INF).astype(BF16)
    q_aug = jnp.concatenate([qs, jnp.concatenate([pen] * nh, axis=1),
                             jnp.zeros((LANE - HEAD_DIM - n_blk, nh * t), BF16)], axis=0)

    init = (_attend_init(HEAD_DIM, nh * t),)

    def tile(j, carries, k_lanes, q, v_rows, bias_idx):
        keys = pl.ds(pl.multiple_of(j * t, t), t)
        s = _dot(kv_ref[keys, k_lanes], q)
        if bias_idx is None:
            shift = bias_ref[2, 0:1, :]
        else:
            s, shift = s + bias_ref[bias_idx], None
        return _attend(carries, [s], [kvt_ref[v_rows, keys]], [shift])

    def if_tile(pred, carries, fn):
        return lax.cond(pred, fn, lambda c: c, carries)

    sel_k, sel_v = slice(0, LANE), slice(HEAD_DIM, 2 * HEAD_DIM)
    carries = tile(qi, init, sel_k, q_aug, sel_v, 0)
    carries = if_tile(qi >= 1, carries, lambda c: tile(jnp.maximum(qi - 1, 0), c, sel_k, q_aug, sel_v, 1))
    (_, l, acc), = lax.fori_loop(0, jnp.maximum(qi - 1, 0),
                                 lambda j, c: tile(j, c, sel_k, q_aug, sel_v, None), carries)
    o_slc = acc / l

    win_k, win_v = slice(LANE, LANE + HEAD_DIM), slice(LANE + HEAD_DIM, 2 * LANE)
    carries = tile(qi, init, win_k, qs, win_v, 0)
    carries = if_tile(qi >= 1, carries, lambda c: tile(jnp.maximum(qi - 1, 0), c, win_k, qs, win_v, 1))
    (_, l, acc), = if_tile(qi >= 2, carries, lambda c: tile(jnp.maximum(qi - 2, 0), c, win_k, qs, win_v, 3))
    o_win = acc / l

    outs = []
    for h in range(nh):
        cs = slice(h * t, (h + 1) * t)
        g = [ngt_ref[_MISC_NGATE + 3 * h + i:_MISC_NGATE + 3 * h + i + 1, :] for i in range(3)]
        outs.append(g[0] * o_cmp[:, cs] + g[1] * o_slc[:, cs] + g[2] * o_win[:, cs])
    o_ref[...] = jnp.concatenate(outs, axis=0).T.astype(BF16)


def _nsa(nqt, kc, kct, nkv, nkvt, ngt, bias, ovt, batch, seq):
    t = ATT_TILE
    nq = seq // t
    assert NSA_WINDOW == 2 * t and seq // NSA_SEL_LEN <= LANE - HEAD_DIM
    n_cmp = kc.shape[1]
    n_blk = ovt.shape[0]
    return pl.pallas_call(
        _nsa_kernel,
        grid=(batch, nq),
        in_specs=[pl.BlockSpec((256, t), lambda b, i: (0, b * nq + i)),
                  pl.BlockSpec((1, n_cmp, LANE), lambda b, i: (b, 0, 0)),
                  pl.BlockSpec((1, LANE, n_cmp), lambda b, i: (b, 0, 0)),
                  pl.BlockSpec((seq, 256), lambda b, i: (b, 0)),
                  pl.BlockSpec((256, seq), lambda b, i: (0, b)),
                  pl.BlockSpec((LANE, t), lambda b, i: (0, b * nq + i)),
                  _weight_spec((4, t, N_HEADS * t)),
                  _const_spec((n_blk, n_cmp))],
        out_specs=pl.BlockSpec((t, 256), lambda b, i: (b * nq + i, 0)),
        out_shape=jax.ShapeDtypeStruct((batch * seq, 256), BF16),
        compiler_params=_params(("arbitrary", "arbitrary")),
        name="nsa",
    )(nqt, kc, kct, nkv, nkvt, ngt, bias, ovt)


def _gla_kernel(qk_ref, v_ref, la_ref, go_ref, tri_ref, gn_ref, o_ref, st_ref):
    c = GLA_CHUNK
    nh = N_HEADS
    n_chunk = qk_ref.shape[0] // c
    st_ref[...] = jnp.zeros_like(st_ref)
    tri = tri_ref[...]
    row = lax.broadcasted_iota(jnp.int32, (c, nh * GLA_DK), 0)
    sub_causal = (lax.broadcasted_iota(jnp.int32, (GLA_SUB, c), 0)
                  - lax.broadcasted_iota(jnp.int32, (GLA_SUB, c), 1))

    hks = [slice(h * GLA_DK, (h + 1) * GLA_DK) for h in range(nh)]
    hvs = [slice(h * GLA_DV, (h + 1) * GLA_DV) for h in range(nh)]
    group_size = GLA_UNROLL

    def group(gi, _):
        rows = [pl.ds(pl.multiple_of((gi * group_size + u) * c, c), c) for u in range(group_size)]
        units = range(group_size)

        bs = []
        for u in units:
            g = la_ref[rows[u], :]
            g1 = g.astype(BF16)
            r1 = g - g1.astype(F32)
            g2 = r1.astype(BF16)
            g3 = (r1 - g2.astype(F32)).astype(BF16)
            bs.append(_dot(tri, g1) + _dot(tri, g2) + _dot(tri, g3))

        q_inter, k_state, decay, q_sub, k_sub, vs_ = [], [], [], [], [], []
        for u in units:
            b = bs[u]
            q = qk_ref[rows[u], 0:256].astype(F32)
            k = qk_ref[rows[u], 256:512].astype(F32)
            b_last = b[c - 1:c, :]
            q_inter.append((q * jnp.exp(b)).astype(BF16))
            k_state.append((k * jnp.exp(b_last - b)).astype(BF16))
            decay.append(jnp.exp(b_last))
            qs_u, ks_u = [], []
            for i in range(c // GLA_SUB):
                lo, hi = i * GLA_SUB, (i + 1) * GLA_SUB
                ref_b = b[lo:lo + 1, :]
                ks_u.append((k * jnp.exp(jnp.where(row < hi, ref_b - b, 0.0))).astype(BF16))
                qs_u.append((q[lo:hi] * jnp.exp(b[lo:hi] - ref_b)).astype(BF16))
            q_sub.append(qs_u)
            k_sub.append(ks_u)
            vs_.append(v_ref[rows[u], :])

        o_intra, kv = [], []
        for u in units:
            a_h = []
            for h in range(nh):
                blocks = [jnp.where(sub_causal + i * GLA_SUB >= 0, _dot_nt(q_sub[u][i][:, hks[h]], k_sub[u][i][:, hks[h]]), 0.0)
                          for i in range(c // GLA_SUB)]
                a_h.append(jnp.concatenate(blocks, axis=0).astype(BF16))
            o_intra.append([_dot(a_h[h], vs_[u][:, hvs[h]]) for h in range(nh)])
            kv.append([_dot_tn(vs_[u][:, hvs[h]], k_state[u][:, hks[h]]) for h in range(nh)])

        st = [st_ref[h] for h in range(nh)]
        for u in units:
            outs = []
            for h in range(nh):
                o = o_intra[u][h] + _dot_nt(q_inter[u][:, hks[h]], st[h].astype(BF16))
                st[h] = st[h] * decay[u][:, hks[h]] + kv[u][h]
                outs.append(_rms_rows(o, gn_ref[...]))
            o_ref[rows[u], :] = (jnp.concatenate(outs, axis=1) * go_ref[rows[u], :].astype(F32)).astype(BF16)
        for h in range(nh):
            st_ref[h] = st[h]
        return 0

    lax.fori_loop(0, n_chunk // group_size, group, 0)


def _gla(gqk, gv, la, go, tri, gn, batch, seq):
    spec = lambda w: pl.BlockSpec((seq, w), lambda b: (b, 0))
    return pl.pallas_call(
        _gla_kernel,
        grid=(batch,),
        in_specs=[spec(512), spec(512), spec(256), spec(512), _const_spec((GLA_CHUNK, GLA_CHUNK)),
                  _const_spec((1, GLA_DV))],
        out_specs=spec(512),
        out_shape=jax.ShapeDtypeStruct((batch * seq, 512), BF16),
        scratch_shapes=[pltpu.VMEM((N_HEADS, GLA_DV, GLA_DK), F32)],
        compiler_params=_params(("arbitrary",)),
        name="gla",
    )(gqk, gv, la, go, tri, gn)


def _merge_kernel(x_ref, an_ref, om_ref, on_ref, og_ref, wm_ref, pm_ref, pn_ref, pg_ref, wo_ref, o_ref):
    x = x_ref[...]
    h = _rms_rows(x, an_ref[...]).astype(BF16)
    z = jax.nn.sigmoid(_dot(h, wm_ref[:, 0:D_MODEL])) * _dot(om_ref[...], pm_ref[...])
    z += jax.nn.sigmoid(_dot(h, wm_ref[:, D_MODEL:2 * D_MODEL])) * _dot(on_ref[...], pn_ref[...])
    z += jax.nn.sigmoid(_dot(h, wm_ref[:, 2 * D_MODEL:3 * D_MODEL])) * _dot(og_ref[...], pg_ref[...])
    o_ref[...] = x + _dot(z.astype(BF16), wo_ref[...])


def _merge(x, an, om, on, og, wm, pm, pn, pg, wo, tm=512):
    n = x.shape[0]
    row = lambda w: pl.BlockSpec((tm, w), lambda i: (i, 0))
    return pl.pallas_call(
        _merge_kernel,
        grid=(n // tm,),
        in_specs=[row(D_MODEL), _const_spec((1, D_MODEL)), row(256), row(256), row(512),
                  _weight_spec((D_MODEL, 3 * D_MODEL)), _weight_spec((256, D_MODEL)), _weight_spec((256, D_MODEL)),
                  _weight_spec((512, D_MODEL)), _weight_spec((D_MODEL, D_MODEL))],
        out_specs=row(D_MODEL),
        out_shape=jax.ShapeDtypeStruct((n, D_MODEL), F32),
        compiler_params=_params(("arbitrary",)),
        name="merge",
    )(x, an, om, on, og, wm, pm, pn, pg, wo)


FFN_CHUNK = 256


def _ffn_kernel(x_ref, fn_ref, wa_ref, wg_ref, cw_ref, cb_ref, wd_ref, o_ref, carry_ref, act_ref, *, tiles_per_seq):
    i = pl.program_id(0)
    tm = x_ref.shape[0]
    x = x_ref[...]
    h = _rms_rows(x, fn_ref[...]).astype(BF16)
    row = lax.broadcasted_iota(jnp.int32, (tm, FFN_CHUNK), 0)

    @pl.when((i % tiles_per_seq) == 0)
    def _():
        carry_ref[...] = jnp.zeros_like(carry_ref)

    for c in range(D_FF // FFN_CHUNK):
        cs = slice(c * FFN_CHUNK, (c + 1) * FFN_CHUNK)
        a = _dot(h, wa_ref[:, cs])
        g = _dot(h, wg_ref[:, cs])
        prev = carry_ref[:, cs]
        p1 = prev[7:8, :]
        p2 = prev[6:7, :]
        a1 = jnp.where(row == 0, p1, pltpu.roll(a, 1, axis=0))
        a2 = jnp.where(row == 0, p2, jnp.where(row == 1, p1, pltpu.roll(a, 2, axis=0)))
        carry_ref[:, cs] = a[tm - 8:tm, :]
        w = cw_ref[:, cs]
        conv = w[0:1, :] * a2 + w[1:2, :] * a1 + w[2:3, :] * a + cb_ref[:, cs]
        act_ref[:, cs] = (jax.nn.gelu(conv, approximate=True) * g).astype(BF16)
    o_ref[...] = x + _dot(act_ref[...], wd_ref[...])


def _ffn(x, fn, wa, wg, cw, cb, wd, seq, tm=512):
    n = x.shape[0]
    row = pl.BlockSpec((tm, D_MODEL), lambda i: (i, 0))
    return pl.pallas_call(
        functools.partial(_ffn_kernel, tiles_per_seq=seq // tm),
        grid=(n // tm,),
        in_specs=[row, _const_spec((1, D_MODEL)), _weight_spec((D_MODEL, D_FF)), _weight_spec((D_MODEL, D_FF)),
                  _const_spec((8, D_FF)), _const_spec((1, D_FF)), _weight_spec((D_FF, D_MODEL))],
        out_specs=row,
        out_shape=jax.ShapeDtypeStruct((n, D_MODEL), F32),
        scratch_shapes=[pltpu.VMEM((8, D_FF), F32), pltpu.VMEM((tm, D_FF), BF16)],
        compiler_params=_params(("arbitrary",)),
        name="ffn",
    )(x, fn, wa, wg, cw, cb, wd)


def _rel_bucket(dist):
    n = jnp.maximum(dist, 0)
    max_exact = REL_BUCKETS // 2
    nf = jnp.maximum(n, 1).astype(F32)
    large = max_exact + (jnp.log(nf / max_exact) / math.log(REL_MAX_DIST / max_exact)
                         * (REL_BUCKETS - max_exact)).astype(jnp.int32)
    return jnp.where(n < max_exact, n, jnp.minimum(large, REL_BUCKETS - 1))


def _bias_tiles(rel_tab):
    t = ATT_TILE
    d0 = jnp.arange(t)[None, :] - jnp.arange(t)[:, None]
    bucket = jnp.stack([_rel_bucket(d0 + k * t) for k in range(3)])
    out = jnp.zeros((rel_tab.shape[1],) + bucket.shape, F32)
    for b in range(REL_BUCKETS):
        out = jnp.where(bucket[None] == b, rel_tab[b][:, None, None, None], out)
    return out


def _block_diag_mean():
    g = jnp.arange(256) // HEAD_DIM
    return jnp.where(g[:, None] == g[None, :], 1.0 / HEAD_DIM, 0.0).astype(BF16)


def _overlap_t(n_cmp_pad, n_cmp, n_blk):
    tok = jnp.arange(n_blk * NSA_SEL_LEN)
    starts = jnp.arange(n_cmp_pad) * NSA_CMP_STRIDE
    inside = (tok[None, :] >= starts[:, None]) & (tok[None, :] < starts[:, None] + NSA_CMP_LEN)
    m = inside.reshape(n_cmp_pad, n_blk, NSA_SEL_LEN).sum(-1).astype(F32) / NSA_CMP_LEN
    m = jnp.where(jnp.arange(n_cmp_pad)[:, None] < n_cmp, m, 0.0)
    return m.T.astype(BF16)


def _tile_gain(g, reps):
    return jnp.tile(g.astype(F32), reps)[None, :]


def kernel(x, rel_bias, attn_norm, w_in, moba_q_norm, moba_k_norm, nsa_q_norm, nsa_k_norm, cmp_pos_k, cmp_pos_v,
           cmp_k_w1, cmp_k_w2, cmp_v_w1, cmp_v_w2, gla_gate_w, gla_gate_b, gla_out_norm, w_branch_moba,
           w_branch_nsa, w_branch_gla, w_out, ffn_norm, w_up, conv_w, conv_b, w_down):
    batch, seq, _ = x.shape
    depth = w_in.shape[0]
    n_cmp = seq // NSA_CMP_STRIDE - NSA_CMP_LEN // NSA_CMP_STRIDE + 1
    n_cmp_pad = seq // NSA_CMP_STRIDE
    n_blk = seq // NSA_SEL_LEN

    tiles = _bias_tiles(rel_bias.astype(F32)) * LOG2_E
    key_le_query = jnp.arange(ATT_TILE)[:, None] <= jnp.arange(ATT_TILE)[None, :]
    own = jnp.where(key_le_query, tiles[:, 0], NEG_INF)
    band = jnp.where(key_le_query, NEG_INF, tiles[:, 2])
    bias_moba = jnp.stack([own[:N_HEADS], tiles[:N_HEADS, 1], tiles[:N_HEADS, 2]], axis=1)
    bias_nsa = jnp.stack([jnp.concatenate(list(kind[N_HEADS:]), axis=1)
                          for kind in (own, tiles[:, 1], tiles[:, 2], band)])
    bd = _block_diag_mean()
    ovt = _overlap_t(n_cmp_pad, n_cmp, n_blk)
    tri = (jnp.arange(GLA_CHUNK)[:, None] >= jnp.arange(GLA_CHUNK)[None, :]).astype(BF16)
    ones64 = jnp.ones((HEAD_DIM,), F32)

    xf = x.reshape(batch * seq, D_MODEL)
    for l in range(depth):
        w_att, w_gla, w_go, w_glr, w_ng, wmerge = _wsplit(w_in, l)
        w_misc = jnp.pad(jnp.concatenate([w_glr, w_ng], axis=1), ((0, 0), (0, LANE - GLA_GATE_RANK - 3 * N_HEADS)))
        gw = jnp.pad(gla_gate_w[l], ((0, LANE - GLA_GATE_RANK), (0, 0))).astype(BF16)

        mqt, mk_a, mvt, nqt, kvc, nkv_a, nkvt, gqk, gv_a, la, go, ngt = _proj(
            xf, attn_norm[l][None, :], w_att, w_gla, w_go, w_misc, bd,
            _tile_gain(moba_q_norm[l], 4), _tile_gain(moba_k_norm[l], 4), _tile_gain(nsa_q_norm[l], 4),
            jnp.concatenate([nsa_k_norm[l, 1], ones64])[None, :], jnp.concatenate([nsa_k_norm[l, 2], ones64])[None, :],
            gw, gla_gate_b[l][None, :], seq)

        o_moba = _moba(mqt, mk_a, mvt, bias_moba, batch, seq)

        half = NSA_CMP_STRIDE * HEAD_DIM
        zero = jnp.zeros((NSA_CMP_STRIDE, HEAD_DIM, HEAD_DIM), F32)

        def w1_part(part):
            wk = cmp_k_w1[l][part * half:(part + 1) * half].reshape(NSA_CMP_STRIDE, HEAD_DIM, HEAD_DIM)
            wv = cmp_v_w1[l][part * half:(part + 1) * half].reshape(NSA_CMP_STRIDE, HEAD_DIM, HEAD_DIM)
            top = jnp.concatenate([wk, zero], axis=2)
            bot = jnp.concatenate([zero, wv], axis=2)
            return jnp.concatenate([top, bot], axis=1).reshape(NSA_CMP_STRIDE * LANE, LANE).astype(BF16)

        def pos_part(part):
            pk = cmp_pos_k[l][part * NSA_CMP_STRIDE:(part + 1) * NSA_CMP_STRIDE]
            pv = cmp_pos_v[l][part * NSA_CMP_STRIDE:(part + 1) * NSA_CMP_STRIDE]
            return jnp.concatenate([pk, pv], axis=1).reshape(1, NSA_CMP_STRIDE * LANE).astype(F32)

        z64 = jnp.zeros((HEAD_DIM, HEAD_DIM), F32)
        w2 = jnp.concatenate([jnp.concatenate([cmp_k_w2[l], z64], axis=1),
                              jnp.concatenate([z64, cmp_v_w2[l]], axis=1)], axis=0).astype(BF16)
        kcv, kcvt = _compress(kvc.reshape(batch, n_cmp_pad, NSA_CMP_STRIDE * LANE), pos_part(0), pos_part(1),
                              w1_part(0), w1_part(1), w2, jnp.concatenate([nsa_k_norm[l, 0], ones64])[None, :])

        o_nsa = _nsa(nqt, kcv, kcvt, nkv_a, nkvt, ngt, bias_nsa, ovt, batch, seq)
        o_gla = _gla(gqk, gv_a, la, go, tri, gla_out_norm[l][None, :].astype(F32), batch, seq)

        xf = _merge(xf, attn_norm[l][None, :], o_moba, o_nsa, o_gla, wmerge.astype(BF16),
                    w_branch_moba[l].astype(BF16), w_branch_nsa[l].astype(BF16), w_branch_gla[l].astype(BF16),
                    w_out[l].astype(BF16))

        cw = jnp.pad(conv_w[l], ((0, 8 - conv_w.shape[1]), (0, 0)))
        xf = _ffn(xf, ffn_norm[l][None, :], w_up[l][:, :D_FF].astype(BF16), w_up[l][:, D_FF:].astype(BF16),
                  cw, conv_b[l][None, :], w_down[l].astype(BF16), seq)
    return xf.reshape(batch, seq, D_MODEL)
```

```python
import functools
import math

import jax
import jax.numpy as jnp
from jax import lax
from jax.experimental import pallas as pl
from jax.experimental.pallas import tpu as pltpu

F32 = jnp.float32
BF16 = jnp.bfloat16

D_MODEL = 1024
HEAD_DIM = 64
N_HEADS = 4
MOBA_BLOCK = 256
MOBA_TOPK = 3
NSA_CMP_LEN = 32
NSA_CMP_STRIDE = 16
NSA_SEL_LEN = 64
NSA_SEL_TOPN = 16
NSA_WINDOW = 512
NSA_FORCE_BONUS = 1e4
GLA_DK = 64
GLA_DV = 128
GLA_GATE_RANK = 16
GLA_GATE_NORM = 16.0
GLA_CHUNK = 64
GLA_SUB = 16
GLA_UNROLL = 8
D_FF = 2816
REL_BUCKETS = 32
REL_MAX_DIST = 128
NORM_EPS = 1e-6
NEG_INF = -1e30
LOG2_E = math.log2(math.e)

ATT_TILE = 256
LANE = 128
VMEM_LIMIT = 56 * 1024 * 1024

_C_MQ, _C_MK, _C_MV, _C_NQ = 0, 256, 512, 768
_C_KVC, _C_KSVS, _C_KWVW, _C_A_END = 1024, 1152, 1280, 1408
_MISC_NGATE = GLA_GATE_RANK


def _dot(a, b):
    return jnp.dot(a, b, preferred_element_type=F32)


def _dot_nt(a, b):
    return lax.dot_general(a, b, (((1,), (1,)), ((), ())), preferred_element_type=F32)


def _dot_tn(a, b):
    return lax.dot_general(a, b, (((0,), (0,)), ((), ())), preferred_element_type=F32)


def _rms_rows(x, gain):
    ms = jnp.mean(x * x, axis=-1, keepdims=True)
    return x * lax.rsqrt(ms + NORM_EPS) * gain


def _split_dot(x, w):
    hi = x.astype(BF16)
    lo = (x - hi.astype(F32)).astype(BF16)
    return _dot(hi, w) + _dot(lo, w)


def _params(sem):
    return pltpu.CompilerParams(dimension_semantics=sem, vmem_limit_bytes=VMEM_LIMIT)


def _const_spec(shape):
    return pl.BlockSpec(shape, lambda *_: (0,) * len(shape))


def _weight_spec(shape):
    return pl.BlockSpec(shape, lambda *_: (0,) * len(shape), pipeline_mode=pl.Buffered(1))


_W_NG, _W_GLA, _W_GLR, _W_GO, _W_MERGE, _W_END = 1408, 1420, 2444, 2460, 2972, 6044


def _wsplit_kernel(w_ref, att_ref, gla_ref, go_ref, glr_ref, ng_ref, merge_ref):
    w = w_ref[0]
    att_ref[...] = w[:, 0:_W_NG].astype(BF16)
    ng_ref[...] = w[:, _W_NG:_W_GLA].astype(BF16)
    gla_ref[...] = w[:, _W_GLA:_W_GLR].astype(BF16)
    glr_ref[...] = w[:, _W_GLR:_W_GO].astype(BF16)
    go_ref[...] = w[:, _W_GO:_W_MERGE].astype(BF16)
    merge_ref[...] = w[:, _W_MERGE:_W_END].astype(BF16)


def _wsplit(w_in, layer, tr=256):
    widths = [_W_NG, _W_GLR - _W_GLA, _W_MERGE - _W_GO, _W_GO - _W_GLR, _W_GLA - _W_NG, _W_END - _W_MERGE]
    return pl.pallas_call(
        _wsplit_kernel,
        grid=(D_MODEL // tr,),
        in_specs=[pl.BlockSpec((1, tr, _W_END), lambda i: (layer, i, 0))],
        out_specs=[pl.BlockSpec((tr, w), lambda i: (i, 0)) for w in widths],
        out_shape=[jax.ShapeDtypeStruct((D_MODEL, w), BF16) for w in widths],
        compiler_params=_params(("arbitrary",)),
        name="wsplit",
    )(w_in)


def _proj_kernel(x_ref, an_ref, wa_ref, wg_ref, wo_ref, wm_ref, bd_ref, gmq_ref, gmk_ref, gnq_ref, gks_ref, gkw_ref,
                 gw_ref, gb_ref,
                 mqt_ref, mk_ref, mvt_ref, nqt_ref, kvc_ref, nkv_ref, nkvt_ref, gqk_ref, gv_ref, la_ref, go_ref,
                 ngt_ref, *, seq):
    tm = x_ref.shape[0]
    h = _rms_rows(x_ref[...], an_ref[...]).astype(BF16)
    pos = (pl.program_id(0) * tm + lax.broadcasted_iota(jnp.int32, (tm, LANE), 0)) % seq
    lane = lax.broadcasted_iota(jnp.int32, (tm, LANE), 1)

    def sec(w_ref, lo, hi):
        return _dot(h, w_ref[:, lo:hi])

    def head_norm(y, gain):
        w = y.shape[1]
        ms = _dot((y * y).astype(BF16), bd_ref[0:w, 0:w])
        return y * lax.rsqrt(ms + NORM_EPS) * gain

    scale = HEAD_DIM ** -0.5 * LOG2_E
    mqt_ref[...] = (head_norm(sec(wa_ref, _C_MQ, _C_MK), gmq_ref[...]) * scale).T.astype(BF16)
    mk = head_norm(sec(wa_ref, _C_MK, _C_MV), gmk_ref[...])
    pos_h = (pl.program_id(0) * tm + lax.broadcasted_iota(jnp.int32, (tm, HEAD_DIM), 0)) % seq
    lane_h = lax.broadcasted_iota(jnp.int32, (tm, HEAD_DIM), 1)
    moba_onehot = jnp.where(lane_h == pos_h // MOBA_BLOCK, 1.0, 0.0)
    for hd in range(N_HEADS):
        mk_ref[:, hd * LANE:(hd + 1) * LANE] = jnp.concatenate(
            [mk[:, hd * HEAD_DIM:(hd + 1) * HEAD_DIM], moba_onehot], axis=1).astype(BF16)
    mvt_ref[...] = sec(wa_ref, _C_MV, _C_NQ).T.astype(BF16)
    nqt_ref[...] = (head_norm(sec(wa_ref, _C_NQ, _C_KVC), gnq_ref[...]) * scale).T.astype(BF16)
    kvc_ref[...] = sec(wa_ref, _C_KVC, _C_KSVS).astype(BF16)
    first_half = lane < HEAD_DIM
    ksvs = sec(wa_ref, _C_KSVS, _C_KWVW)
    ksvs = jnp.where(first_half, head_norm(ksvs, gks_ref[...]), ksvs)
    nkvt_ref[0:128, :] = ksvs.T.astype(BF16)
    sel_onehot = jnp.where(lane - HEAD_DIM == pos // NSA_SEL_LEN, 1.0, 0.0)
    nkv_ref[:, 0:128] = jnp.where(first_half, ksvs, sel_onehot).astype(BF16)
    kwvw = sec(wa_ref, _C_KWVW, _C_A_END)
    kwvw = jnp.where(first_half, head_norm(kwvw, gkw_ref[...]), kwvw)
    nkv_ref[:, 128:256] = kwvw.astype(BF16)
    nkvt_ref[128:256, :] = kwvw.T.astype(BF16)
    gqk_ref[:, 0:256] = (sec(wg_ref, 0, 256) * (GLA_DK ** -0.5)).astype(BF16)
    gqk_ref[:, 256:512] = sec(wg_ref, 256, 512).astype(BF16)
    gv_ref[...] = sec(wg_ref, 512, 1024).astype(BF16)
    go_ref[...] = jax.nn.silu(_dot(h, wo_ref[...])).astype(BF16)
    misc = _dot(h, wm_ref[...])
    ngt_ref[...] = jax.nn.sigmoid(misc).T
    pre = _dot(misc.astype(BF16), gw_ref[...]) + gb_ref[...]
    la_ref[...] = (jnp.minimum(pre, 0.0) - jnp.log(1.0 + jnp.exp(-jnp.abs(pre)))) * (1.0 / GLA_GATE_NORM)


def _proj(x, an, w_att, w_gla, w_go, w_misc, bd, gmq, gmk, gnq, gks, gkw, gw, gb, seq, tm=512):
    n = x.shape[0]
    row = lambda w: pl.BlockSpec((tm, w), lambda i: (i, 0))
    col = lambda w: pl.BlockSpec((w, tm), lambda i: (0, i))
    outs = [(256, BF16, True), (N_HEADS * LANE, BF16, False), (256, BF16, True), (256, BF16, True),
            (128, BF16, False), (256, BF16, False), (256, BF16, True), (512, BF16, False), (512, BF16, False),
            (256, F32, False), (512, BF16, False), (128, F32, True)]
    return pl.pallas_call(
        functools.partial(_proj_kernel, seq=seq),
        grid=(n // tm,),
        in_specs=[row(D_MODEL), _const_spec((1, D_MODEL)), _weight_spec(w_att.shape), _weight_spec(w_gla.shape),
                  _weight_spec(w_go.shape), _weight_spec(w_misc.shape), _const_spec((256, 256)),
                  _const_spec((1, 256)), _const_spec((1, 256)), _const_spec((1, 256)), _const_spec((1, 128)),
                  _const_spec((1, 128)), _const_spec((LANE, 256)), _const_spec((1, 256))],
        out_specs=[col(w) if tr else row(w) for w, _, tr in outs],
        out_shape=[jax.ShapeDtypeStruct((w, n) if tr else (n, w), dt) for w, dt, tr in outs],
        compiler_params=_params(("arbitrary",)),
        name="proj",
    )(x, an, w_att, w_gla, w_go, w_misc, bd, gmq, gmk, gnq, gks, gkw, gw, gb)


def _attend_init(dv, nq):
    return (jnp.full((1, nq), NEG_INF, F32), jnp.zeros((1, nq), F32), jnp.zeros((dv, nq), F32))


def _attend(carries, scores, vts, shifts):
    stats = []
    for (m, _, _), s, shift in zip(carries, scores, shifts):
        tile_max = jnp.max(s, axis=0, keepdims=True)
        if shift is not None:
            tile_max = tile_max + shift
        m_new = jnp.maximum(m, tile_max)
        p = jnp.exp2(s - (m_new if shift is None else m_new - shift))
        stats.append((m_new, jnp.exp2(m - m_new), p))
    pvs = [_dot(vt, p.astype(BF16)) for vt, (_, _, p) in zip(vts, stats)]
    return tuple((m_new, alpha * l + jnp.sum(p, axis=0, keepdims=True), alpha * acc + pv)
                 for (_, l, acc), (m_new, alpha, p), pv in zip(carries, stats, pvs))


FAR_GROUP = 4


def _far_tiles(n, carries, tile_fn):
    groups = n // FAR_GROUP
    carries = lax.fori_loop(0, groups, lambda g, c: tile_fn(g * FAR_GROUP, c, FAR_GROUP), carries)
    done = groups * FAR_GROUP
    width = FAR_GROUP // 2
    while width >= 1:
        take = ((n - done) // width) % 2 == 1
        carries = lax.cond(take, lambda c, done=done, width=width: tile_fn(done, c, width), lambda c: c, carries)
        done = done + jnp.where(take, width, 0)
        width //= 2
    return carries


def _rank_rows(score, n):
    idx = lax.broadcasted_iota(jnp.int32, score.shape, 0)
    rank = jnp.zeros(score.shape, F32)
    for m in range(n):
        sm = score[m:m + 1, :]
        rank += jnp.where((sm > score) | ((sm == score) & (idx > m)), 1.0, 0.0)
    return rank


def _moba_kernel(qt_ref, k_ref, vt_ref, bias_ref, o_ref, kmean_ref):
    t = ATT_TILE
    nh = N_HEADS
    nb = k_ref.shape[0] // t
    qi = pl.program_id(1)
    kcols = [slice(h * LANE, (h + 1) * LANE) for h in range(nh)]
    vrows = [slice(h * HEAD_DIM, (h + 1) * HEAD_DIM) for h in range(nh)]

    @pl.when(qi == 0)
    def _():
        for h in range(nh):
            kh = k_ref[:, h * LANE:h * LANE + HEAD_DIM].astype(F32)
            kmean_ref[h] = jnp.mean(kh.reshape(nb, t, HEAD_DIM), axis=1)

    past = lax.broadcasted_iota(jnp.int32, (nb, t), 0) < qi
    q_aug = []
    for h in range(nh):
        qt = qt_ref[vrows[h], :]
        gate = jnp.where(past, _dot(kmean_ref[h].astype(BF16), qt), NEG_INF)
        dropped = past & (_rank_rows(gate, nb) >= MOBA_TOPK)
        pen = jnp.concatenate([jnp.where(dropped, NEG_INF, 0.0), jnp.zeros((16 - nb, t), F32)], axis=0)
        q_aug.append(jnp.concatenate([qt, pen.astype(BF16), jnp.zeros((LANE - HEAD_DIM - 16, t), BF16)], axis=0))

    def tile(j, carries, bias_idx, width=1):
        keys = pl.ds(pl.multiple_of(j * t, t), width * t)
        scores = [_dot(k_ref[keys, kcols[h]], q_aug[h]) for h in range(nh)]
        if bias_idx is None:
            shifts = [bias_ref[h, 2, 0:1, :] for h in range(nh)]
        else:
            scores = [s + bias_ref[h, bias_idx] for h, s in enumerate(scores)]
            shifts = [None] * nh
        return _attend(carries, scores, [vt_ref[vrows[h], keys] for h in range(nh)], shifts)

    carries = tile(qi, tuple(_attend_init(HEAD_DIM, t) for _ in range(nh)), 0)
    carries = lax.cond(qi >= 1, lambda c: tile(jnp.maximum(qi - 1, 0), c, 1), lambda c: c, carries)
    carries = _far_tiles(jnp.maximum(qi - 1, 0), carries, lambda j, c, width: tile(j, c, None, width))
    o_ref[...] = jnp.concatenate([acc / l for _, l, acc in carries], axis=0).T.astype(BF16)


def _moba(mqt, mk, mvt, bias, batch, seq):
    t = ATT_TILE
    nq = seq // t
    return pl.pallas_call(
        _moba_kernel,
        grid=(batch, nq),
        in_specs=[pl.BlockSpec((256, t), lambda b, i: (0, b * nq + i)),
                  pl.BlockSpec((seq, N_HEADS * LANE), lambda b, i: (b, 0)),
                  pl.BlockSpec((256, seq), lambda b, i: (0, b)),
                  _const_spec((N_HEADS, 3, t, t))],
        out_specs=pl.BlockSpec((t, 256), lambda b, i: (b * nq + i, 0)),
        out_shape=jax.ShapeDtypeStruct((batch * seq, 256), BF16),
        scratch_shapes=[pltpu.VMEM((N_HEADS, nq, HEAD_DIM), F32)],
        compiler_params=_params(("arbitrary", "arbitrary")),
        name="moba",
    )(mqt, mk, mvt, bias)


def _compress_kernel(x_ref, pa_ref, pb_ref, w1a_ref, w1b_ref, w2_ref, gk_ref, o_ref, ot_ref):
    x = x_ref[0].astype(F32)
    u = _dot((x + pa_ref[...]).astype(BF16), w1a_ref[...])
    v = _dot((x + pb_ref[...]).astype(BF16), w1b_ref[...])
    n = u.shape[0]
    hid = u + pltpu.roll(v, n - 1, axis=0)
    y = _dot(jax.nn.gelu(hid, approximate=True).astype(BF16), w2_ref[...])
    is_k = lax.broadcasted_iota(jnp.int32, y.shape, 1) < HEAD_DIM
    ms = jnp.sum(jnp.where(is_k, y * y, 0.0), axis=1, keepdims=True) * (1.0 / HEAD_DIM)
    y = jnp.where(is_k, y * lax.rsqrt(ms + NORM_EPS) * gk_ref[...], y)
    o_ref[0] = y.astype(BF16)
    ot_ref[0] = y.T.astype(BF16)


def _compress(kvc3, pa, pb, w1a, w1b, w2, gk):
    batch, n, w = kvc3.shape
    return pl.pallas_call(
        _compress_kernel,
        grid=(batch,),
        in_specs=[pl.BlockSpec((1, n, w), lambda b: (b, 0, 0)), _const_spec((1, w)), _const_spec((1, w)),
                  _const_spec((w, LANE)), _const_spec((w, LANE)), _const_spec((LANE, LANE)), _const_spec((1, LANE))],
        out_specs=[pl.BlockSpec((1, n, LANE), lambda b: (b, 0, 0)), pl.BlockSpec((1, LANE, n), lambda b: (b, 0, 0))],
        out_shape=[jax.ShapeDtypeStruct((batch, n, LANE), BF16), jax.ShapeDtypeStruct((batch, LANE, n), BF16)],
        compiler_params=_params(("arbitrary",)),
        name="compress",
    )(kvc3, pa, pb, w1a, w1b, w2, gk)


def _nsa_kernel(qt_ref, kc_ref, kct_ref, kv_ref, kvt_ref, ngt_ref, bias_ref, ovt_ref, o_ref):
    t = ATT_TILE
    nh = N_HEADS
    qi = pl.program_id(1)
    qt_all = qt_ref[...]
    qs = jnp.concatenate([qt_all[h * HEAD_DIM:(h + 1) * HEAD_DIM, :] for h in range(nh)], axis=1)

    kc = kc_ref[0][:, 0:HEAD_DIM]
    vct = kct_ref[0][HEAD_DIM:2 * HEAD_DIM, :]
    n_cmp = kc.shape[0]
    pos = qi * t + (lax.broadcasted_iota(jnp.int32, (n_cmp, nh * t), 1) & (t - 1))
    cend = lax.broadcasted_iota(jnp.int32, (n_cmp, nh * t), 0) * NSA_CMP_STRIDE + (NSA_CMP_LEN - 1)
    vis = cend <= pos
    sc = jnp.where(vis, _dot(kc, qs), NEG_INF)
    e = jnp.where(vis, jnp.exp2(sc - jnp.max(sc, axis=0, keepdims=True)), 0.0)
    den = jnp.sum(e, axis=0, keepdims=True)
    p = e / jnp.where(den > 0.0, den, 1.0)
    o_cmp = _dot(vct, p.astype(BF16))

    p_sum = p[:, 0:t] + p[:, t:2 * t] + p[:, 2 * t:3 * t] + p[:, 3 * t:4 * t]
    p_hi = p_sum.astype(BF16)
    p_lo = (p_sum - p_hi.astype(F32)).astype(BF16)
    ovt = ovt_ref[...]
    n_blk = ovt.shape[0]
    imp = _dot(ovt, p_hi) + _dot(ovt, p_lo)
    blk = lax.broadcasted_iota(jnp.int32, (n_blk, t), 0)
    cur = (qi * t + lax.broadcasted_iota(jnp.int32, (n_blk, t), 1)) // NSA_SEL_LEN
    forced = (blk == 0) | (blk == cur) | (blk == cur - 1)
    valid = blk <= cur
    imp = jnp.where(valid, imp + jnp.where(forced, NSA_FORCE_BONUS, 0.0), NEG_INF)
    keep = valid & (_rank_rows(imp, n_blk) < NSA_SEL_TOPN)
    pen = jnp.where(keep, 0.0, NEG_INF).astype(BF16)
    q_aug = jnp.concatenate([qs, jnp.concatenate([pen] * nh, axis=1),
                             jnp.zeros((LANE - HEAD_DIM - n_blk, nh * t), BF16)], axis=0)

    sel = (slice(0, LANE), q_aug, slice(HEAD_DIM, 2 * HEAD_DIM))
    win = (slice(LANE, LANE + HEAD_DIM), qs, slice(LANE + HEAD_DIM, 2 * LANE))

    def tiles(j, carries, branches, bias_idx, width=1):
        keys = pl.ds(pl.multiple_of(j * t, t), width * t)
        scores = [_dot(kv_ref[keys, k_lanes], q) for k_lanes, q, _ in branches]
        if bias_idx is None:
            shifts = [bias_ref[2, 0:1, :]] * len(branches)
        else:
            scores = [s + bias_ref[bias_idx] for s in scores]
            shifts = [None] * len(branches)
        return _attend(carries, scores, [kvt_ref[v_rows, keys] for _, _, v_rows in branches], shifts)

    def if_tiles(pred, carries, fn):
        return lax.cond(pred, fn, lambda c: c, carries)

    init = _attend_init(HEAD_DIM, nh * t)
    both = tiles(qi, (init, init), [sel, win], 0)
    c_sel, c_win = if_tiles(qi >= 1, both, lambda c: tiles(jnp.maximum(qi - 1, 0), c, [sel, win], 1))
    (_, l, acc), = _far_tiles(jnp.maximum(qi - 1, 0), (c_sel,), lambda j, c, width: tiles(j, c, [sel], None, width))
    o_slc = acc / l
    (_, l, acc), = if_tiles(qi >= 2, (c_win,), lambda c: tiles(jnp.maximum(qi - 2, 0), c, [win], 3))
    o_win = acc / l

    outs = []
    for h in range(nh):
        cs = slice(h * t, (h + 1) * t)
        g = [ngt_ref[_MISC_NGATE + 3 * h + i:_MISC_NGATE + 3 * h + i + 1, :] for i in range(3)]
        outs.append(g[0] * o_cmp[:, cs] + g[1] * o_slc[:, cs] + g[2] * o_win[:, cs])
    o_ref[...] = jnp.concatenate(outs, axis=0).T.astype(BF16)


def _nsa(nqt, kc, kct, nkv, nkvt, ngt, bias, ovt, batch, seq):
    t = ATT_TILE
    nq = seq // t
    assert NSA_WINDOW == 2 * t and seq // NSA_SEL_LEN <= LANE - HEAD_DIM
    n_cmp = kc.shape[1]
    n_blk = ovt.shape[0]
    return pl.pallas_call(
        _nsa_kernel,
        grid=(batch, nq),
        in_specs=[pl.BlockSpec((256, t), lambda b, i: (0, b * nq + i)),
                  pl.BlockSpec((1, n_cmp, LANE), lambda b, i: (b, 0, 0)),
                  pl.BlockSpec((1, LANE, n_cmp), lambda b, i: (b, 0, 0)),
                  pl.BlockSpec((seq, 256), lambda b, i: (b, 0)),
                  pl.BlockSpec((256, seq), lambda b, i: (0, b)),
                  pl.BlockSpec((LANE, t), lambda b, i: (0, b * nq + i)),
                  _weight_spec((4, t, N_HEADS * t)),
                  _const_spec((n_blk, n_cmp))],
        out_specs=pl.BlockSpec((t, 256), lambda b, i: (b * nq + i, 0)),
        out_shape=jax.ShapeDtypeStruct((batch * seq, 256), BF16),
        compiler_params=_params(("arbitrary", "arbitrary")),
        name="nsa",
    )(nqt, kc, kct, nkv, nkvt, ngt, bias, ovt)


def _gla_kernel(qk_ref, v_ref, la_ref, go_ref, tri_ref, gn_ref, o_ref, st_ref):
    c = GLA_CHUNK
    nh = N_HEADS
    n_chunk = qk_ref.shape[0] // c
    st_ref[...] = jnp.zeros_like(st_ref)
    tri = tri_ref[...]
    row = lax.broadcasted_iota(jnp.int32, (c, nh * GLA_DK), 0)
    sub_causal = (lax.broadcasted_iota(jnp.int32, (GLA_SUB, c), 0)
                  - lax.broadcasted_iota(jnp.int32, (GLA_SUB, c), 1))

    hks = [slice(h * GLA_DK, (h + 1) * GLA_DK) for h in range(nh)]
    hvs = [slice(h * GLA_DV, (h + 1) * GLA_DV) for h in range(nh)]
    group_size = GLA_UNROLL

    def group(gi, _):
        rows = [pl.ds(pl.multiple_of((gi * group_size + u) * c, c), c) for u in range(group_size)]
        units = range(group_size)

        bs = []
        for u in units:
            g = la_ref[rows[u], :]
            g1 = g.astype(BF16)
            r1 = g - g1.astype(F32)
            g2 = r1.astype(BF16)
            g3 = (r1 - g2.astype(F32)).astype(BF16)
            bs.append(_dot(tri, g1) + _dot(tri, g2) + _dot(tri, g3))

        q_inter, k_state, decay, q_sub, k_sub, vs_ = [], [], [], [], [], []
        for u in units:
            b = bs[u]
            q = qk_ref[rows[u], 0:256].astype(F32)
            k = qk_ref[rows[u], 256:512].astype(F32)
            b_last = b[c - 1:c, :]
            q_inter.append((q * jnp.exp(b)).astype(BF16))
            k_state.append((k * jnp.exp(b_last - b)).astype(BF16))
            decay.append(jnp.exp(b_last))
            qs_u, ks_u = [], []
            for i in range(c // GLA_SUB):
                lo, hi = i * GLA_SUB, (i + 1) * GLA_SUB
                ref_b = b[lo:lo + 1, :]
                ks_u.append((k * jnp.exp(jnp.where(row < hi, ref_b - b, 0.0))).astype(BF16))
                qs_u.append((q[lo:hi] * jnp.exp(b[lo:hi] - ref_b)).astype(BF16))
            q_sub.append(qs_u)
            k_sub.append(ks_u)
            vs_.append(v_ref[rows[u], :])

        o_intra, kv = [], []
        for u in units:
            a_h = []
            for h in range(nh):
                blocks = [jnp.where(sub_causal + i * GLA_SUB >= 0, _dot_nt(q_sub[u][i][:, hks[h]], k_sub[u][i][:, hks[h]]), 0.0)
                          for i in range(c // GLA_SUB)]
                a_h.append(jnp.concatenate(blocks, axis=0).astype(BF16))
            o_intra.append([_dot(a_h[h], vs_[u][:, hvs[h]]) for h in range(nh)])
            kv.append([_dot_tn(vs_[u][:, hvs[h]], k_state[u][:, hks[h]]) for h in range(nh)])

        st = [st_ref[h] for h in range(nh)]
        for u in units:
            outs = []
            for h in range(nh):
                o = o_intra[u][h] + _dot_nt(q_inter[u][:, hks[h]], st[h].astype(BF16))
                st[h] = st[h] * decay[u][:, hks[h]] + kv[u][h]
                outs.append(_rms_rows(o, gn_ref[...]))
            o_ref[rows[u], :] = (jnp.concatenate(outs, axis=1) * go_ref[rows[u], :].astype(F32)).astype(BF16)
        for h in range(nh):
            st_ref[h] = st[h]
        return 0

    lax.fori_loop(0, n_chunk // group_size, group, 0)


def _gla(gqk, gv, la, go, tri, gn, batch, seq):
    spec = lambda w: pl.BlockSpec((seq, w), lambda b: (b, 0))
    return pl.pallas_call(
        _gla_kernel,
        grid=(batch,),
        in_specs=[spec(512), spec(512), spec(256), spec(512), _const_spec((GLA_CHUNK, GLA_CHUNK)),
                  _const_spec((1, GLA_DV))],
        out_specs=spec(512),
        out_shape=jax.ShapeDtypeStruct((batch * seq, 512), BF16),
        scratch_shapes=[pltpu.VMEM((N_HEADS, GLA_DV, GLA_DK), F32)],
        compiler_params=_params(("arbitrary",)),
        name="gla",
    )(gqk, gv, la, go, tri, gn)


def _merge_kernel(x_ref, an_ref, om_ref, on_ref, og_ref, wm_ref, pm_ref, pn_ref, pg_ref, wo_ref, o_ref):
    x = x_ref[...]
    h = _rms_rows(x, an_ref[...]).astype(BF16)
    z = jax.nn.sigmoid(_dot(h, wm_ref[:, 0:D_MODEL])) * _dot(om_ref[...], pm_ref[...])
    z += jax.nn.sigmoid(_dot(h, wm_ref[:, D_MODEL:2 * D_MODEL])) * _dot(on_ref[...], pn_ref[...])
    z += jax.nn.sigmoid(_dot(h, wm_ref[:, 2 * D_MODEL:3 * D_MODEL])) * _dot(og_ref[...], pg_ref[...])
    o_ref[...] = x + _dot(z.astype(BF16), wo_ref[...])


def _merge(x, an, om, on, og, wm, pm, pn, pg, wo, tm=512):
    n = x.shape[0]
    row = lambda w: pl.BlockSpec((tm, w), lambda i: (i, 0))
    return pl.pallas_call(
        _merge_kernel,
        grid=(n // tm,),
        in_specs=[row(D_MODEL), _const_spec((1, D_MODEL)), row(256), row(256), row(512),
                  _weight_spec((D_MODEL, 3 * D_MODEL)), _weight_spec((256, D_MODEL)), _weight_spec((256, D_MODEL)),
                  _weight_spec((512, D_MODEL)), _weight_spec((D_MODEL, D_MODEL))],
        out_specs=row(D_MODEL),
        out_shape=jax.ShapeDtypeStruct((n, D_MODEL), F32),
        compiler_params=_params(("arbitrary",)),
        name="merge",
    )(x, an, om, on, og, wm, pm, pn, pg, wo)


FFN_CHUNK = 256


def _ffn_kernel(x_ref, fn_ref, wa_ref, wg_ref, cw_ref, cb_ref, wd_ref, o_ref, carry_ref, act_ref, *, tiles_per_seq):
    i = pl.program_id(0)
    tm = x_ref.shape[0]
    x = x_ref[...]
    h = _rms_rows(x, fn_ref[...]).astype(BF16)
    row = lax.broadcasted_iota(jnp.int32, (tm, FFN_CHUNK), 0)

    @pl.when((i % tiles_per_seq) == 0)
    def _():
        carry_ref[...] = jnp.zeros_like(carry_ref)

    for c in range(D_FF // FFN_CHUNK):
        cs = slice(c * FFN_CHUNK, (c + 1) * FFN_CHUNK)
        a = _dot(h, wa_ref[:, cs])
        g = _dot(h, wg_ref[:, cs])
        prev = carry_ref[:, cs]
        p1 = prev[7:8, :]
        p2 = prev[6:7, :]
        a1 = jnp.where(row == 0, p1, pltpu.roll(a, 1, axis=0))
        a2 = jnp.where(row == 0, p2, jnp.where(row == 1, p1, pltpu.roll(a, 2, axis=0)))
        carry_ref[:, cs] = a[tm - 8:tm, :]
        w = cw_ref[:, cs]
        conv = w[0:1, :] * a2 + w[1:2, :] * a1 + w[2:3, :] * a + cb_ref[:, cs]
        act_ref[:, cs] = (jax.nn.gelu(conv, approximate=True) * g).astype(BF16)
    o_ref[...] = x + _dot(act_ref[...], wd_ref[...])


def _ffn(x, fn, wa, wg, cw, cb, wd, seq, tm=1024):
    n = x.shape[0]
    row = pl.BlockSpec((tm, D_MODEL), lambda i: (i, 0))
    return pl.pallas_call(
        functools.partial(_ffn_kernel, tiles_per_seq=seq // tm),
        grid=(n // tm,),
        in_specs=[row, _const_spec((1, D_MODEL)), _weight_spec((D_MODEL, D_FF)), _weight_spec((D_MODEL, D_FF)),
                  _const_spec((8, D_FF)), _const_spec((1, D_FF)), _weight_spec((D_FF, D_MODEL))],
        out_specs=row,
        out_shape=jax.ShapeDtypeStruct((n, D_MODEL), F32),
        scratch_shapes=[pltpu.VMEM((8, D_FF), F32), pltpu.VMEM((tm, D_FF), BF16)],
        compiler_params=_params(("arbitrary",)),
        name="ffn",
    )(x, fn, wa, wg, cw, cb, wd)


def _rel_bucket(dist):
    n = jnp.maximum(dist, 0)
    max_exact = REL_BUCKETS // 2
    nf = jnp.maximum(n, 1).astype(F32)
    large = max_exact + (jnp.log(nf / max_exact) / math.log(REL_MAX_DIST / max_exact)
                         * (REL_BUCKETS - max_exact)).astype(jnp.int32)
    return jnp.where(n < max_exact, n, jnp.minimum(large, REL_BUCKETS - 1))


def _bias_tiles(rel_tab):
    t = ATT_TILE
    d0 = jnp.arange(t)[None, :] - jnp.arange(t)[:, None]
    bucket = jnp.stack([_rel_bucket(d0 + k * t) for k in range(3)])
    out = jnp.zeros((rel_tab.shape[1],) + bucket.shape, F32)
    for b in range(REL_BUCKETS):
        out = jnp.where(bucket[None] == b, rel_tab[b][:, None, None, None], out)
    return out


def _block_diag_mean():
    g = jnp.arange(256) // HEAD_DIM
    return jnp.where(g[:, None] == g[None, :], 1.0 / HEAD_DIM, 0.0).astype(BF16)


def _overlap_t(n_cmp_pad, n_cmp, n_blk):
    tok = jnp.arange(n_blk * NSA_SEL_LEN)
    starts = jnp.arange(n_cmp_pad) * NSA_CMP_STRIDE
    inside = (tok[None, :] >= starts[:, None]) & (tok[None, :] < starts[:, None] + NSA_CMP_LEN)
    m = inside.reshape(n_cmp_pad, n_blk, NSA_SEL_LEN).sum(-1).astype(F32) / NSA_CMP_LEN
    m = jnp.where(jnp.arange(n_cmp_pad)[:, None] < n_cmp, m, 0.0)
    return m.T.astype(BF16)


def _tile_gain(g, reps):
    return jnp.tile(g.astype(F32), reps)[None, :]


def kernel(x, rel_bias, attn_norm, w_in, moba_q_norm, moba_k_norm, nsa_q_norm, nsa_k_norm, cmp_pos_k, cmp_pos_v,
           cmp_k_w1, cmp_k_w2, cmp_v_w1, cmp_v_w2, gla_gate_w, gla_gate_b, gla_out_norm, w_branch_moba,
           w_branch_nsa, w_branch_gla, w_out, ffn_norm, w_up, conv_w, conv_b, w_down):
    batch, seq, _ = x.shape
    depth = w_in.shape[0]
    n_cmp = seq // NSA_CMP_STRIDE - NSA_CMP_LEN // NSA_CMP_STRIDE + 1
    n_cmp_pad = seq // NSA_CMP_STRIDE
    n_blk = seq // NSA_SEL_LEN

    tiles = _bias_tiles(rel_bias.astype(F32)) * LOG2_E
    key_le_query = jnp.arange(ATT_TILE)[:, None] <= jnp.arange(ATT_TILE)[None, :]
    own = jnp.where(key_le_query, tiles[:, 0], NEG_INF)
    band = jnp.where(key_le_query, NEG_INF, tiles[:, 2])
    bias_moba = jnp.stack([own[:N_HEADS], tiles[:N_HEADS, 1], tiles[:N_HEADS, 2]], axis=1)
    bias_nsa = jnp.stack([jnp.concatenate(list(kind[N_HEADS:]), axis=1)
                          for kind in (own, tiles[:, 1], tiles[:, 2], band)])
    bd = _block_diag_mean()
    ovt = _overlap_t(n_cmp_pad, n_cmp, n_blk)
    tri = (jnp.arange(GLA_CHUNK)[:, None] >= jnp.arange(GLA_CHUNK)[None, :]).astype(BF16)
    ones64 = jnp.ones((HEAD_DIM,), F32)

    xf = x.reshape(batch * seq, D_MODEL)
    for l in range(depth):
        w_att, w_gla, w_go, w_glr, w_ng, wmerge = _wsplit(w_in, l)
        w_misc = jnp.pad(jnp.concatenate([w_glr, w_ng], axis=1), ((0, 0), (0, LANE - GLA_GATE_RANK - 3 * N_HEADS)))
        gw = jnp.pad(gla_gate_w[l], ((0, LANE - GLA_GATE_RANK), (0, 0))).astype(BF16)

        mqt, mk_a, mvt, nqt, kvc, nkv_a, nkvt, gqk, gv_a, la, go, ngt = _proj(
            xf, attn_norm[l][None, :], w_att, w_gla, w_go, w_misc, bd,
            _tile_gain(moba_q_norm[l], 4), _tile_gain(moba_k_norm[l], 4), _tile_gain(nsa_q_norm[l], 4),
            jnp.concatenate([nsa_k_norm[l, 1], ones64])[None, :], jnp.concatenate([nsa_k_norm[l, 2], ones64])[None, :],
            gw, gla_gate_b[l][None, :], seq)

        o_moba = _moba(mqt, mk_a, mvt, bias_moba, batch, seq)

        half = NSA_CMP_STRIDE * HEAD_DIM
        zero = jnp.zeros((NSA_CMP_STRIDE, HEAD_DIM, HEAD_DIM), F32)

        def w1_part(part):
            wk = cmp_k_w1[l][part * half:(part + 1) * half].reshape(NSA_CMP_STRIDE, HEAD_DIM, HEAD_DIM)
            wv = cmp_v_w1[l][part * half:(part + 1) * half].reshape(NSA_CMP_STRIDE, HEAD_DIM, HEAD_DIM)
            top = jnp.concatenate([wk, zero], axis=2)
            bot = jnp.concatenate([zero, wv], axis=2)
            return jnp.concatenate([top, bot], axis=1).reshape(NSA_CMP_STRIDE * LANE, LANE).astype(BF16)

        def pos_part(part):
            pk = cmp_pos_k[l][part * NSA_CMP_STRIDE:(part + 1) * NSA_CMP_STRIDE]
            pv = cmp_pos_v[l][part * NSA_CMP_STRIDE:(part + 1) * NSA_CMP_STRIDE]
            return jnp.concatenate([pk, pv], axis=1).reshape(1, NSA_CMP_STRIDE * LANE).astype(F32)

        z64 = jnp.zeros((HEAD_DIM, HEAD_DIM), F32)
        w2 = jnp.concatenate([jnp.concatenate([cmp_k_w2[l], z64], axis=1),
                              jnp.concatenate([z64, cmp_v_w2[l]], axis=1)], axis=0).astype(BF16)
        kcv, kcvt = _compress(kvc.reshape(batch, n_cmp_pad, NSA_CMP_STRIDE * LANE), pos_part(0), pos_part(1),
                              w1_part(0), w1_part(1), w2, jnp.concatenate([nsa_k_norm[l, 0], ones64])[None, :])

        o_nsa = _nsa(nqt, kcv, kcvt, nkv_a, nkvt, ngt, bias_nsa, ovt, batch, seq)
        o_gla = _gla(gqk, gv_a, la, go, tri, gla_out_norm[l][None, :].astype(F32), batch, seq)

        xf = _merge(xf, attn_norm[l][None, :], o_moba, o_nsa, o_gla, wmerge.astype(BF16),
                    w_branch_moba[l].astype(BF16), w_branch_nsa[l].astype(BF16), w_branch_gla[l].astype(BF16),
                    w_out[l].astype(BF16))

        cw = jnp.pad(conv_w[l], ((0, 8 - conv_w.shape[1]), (0, 0)))
        xf = _ffn(xf, ffn_norm[l][None, :], w_up[l][:, :D_FF].astype(BF16), w_up[l][:, D_FF:].astype(BF16),
                  cw, conv_b[l][None, :], w_down[l].astype(BF16), seq)
    return xf.reshape(batch, seq, D_MODEL)
```

```python
import functools
import math

import jax
import jax.numpy as jnp
from jax import lax
from jax.experimental import pallas as pl
from jax.experimental.pallas import tpu as pltpu

F32 = jnp.float32
BF16 = jnp.bfloat16

D_MODEL = 1024
HEAD_DIM = 64
N_HEADS = 4
MOBA_BLOCK = 256
MOBA_TOPK = 3
NSA_CMP_LEN = 32
NSA_CMP_STRIDE = 16
NSA_SEL_LEN = 64
NSA_SEL_TOPN = 16
NSA_WINDOW = 512
NSA_FORCE_BONUS = 1e4
GLA_DK = 64
GLA_DV = 128
GLA_GATE_RANK = 16
GLA_GATE_NORM = 16.0
GLA_CHUNK = 64
GLA_SUB = 16
GLA_UNROLL = 8
D_FF = 2816
REL_BUCKETS = 32
REL_MAX_DIST = 128
NORM_EPS = 1e-6
NEG_INF = -1e30
LOG2_E = math.log2(math.e)

ATT_TILE = 256
LANE = 128
VMEM_LIMIT = 56 * 1024 * 1024

_C_MQ, _C_MK, _C_MV, _C_NQ = 0, 256, 512, 768
_C_KVC, _C_KSVS, _C_KWVW, _C_A_END = 1024, 1152, 1280, 1408
_MISC_NGATE = GLA_GATE_RANK


def _dot(a, b):
    return jnp.dot(a, b, preferred_element_type=F32)


def _dot_nt(a, b):
    return lax.dot_general(a, b, (((1,), (1,)), ((), ())), preferred_element_type=F32)


def _dot_tn(a, b):
    return lax.dot_general(a, b, (((0,), (0,)), ((), ())), preferred_element_type=F32)


def _rms_rows(x, gain):
    ms = jnp.mean(x * x, axis=-1, keepdims=True)
    return x * lax.rsqrt(ms + NORM_EPS) * gain


def _split_dot(x, w):
    hi = x.astype(BF16)
    lo = (x - hi.astype(F32)).astype(BF16)
    return _dot(hi, w) + _dot(lo, w)


def _params(sem):
    return pltpu.CompilerParams(dimension_semantics=sem, vmem_limit_bytes=VMEM_LIMIT)


def _const_spec(shape):
    return pl.BlockSpec(shape, lambda *_: (0,) * len(shape))


def _weight_spec(shape):
    return pl.BlockSpec(shape, lambda *_: (0,) * len(shape), pipeline_mode=pl.Buffered(1))


_W_NG, _W_GLA, _W_GLR, _W_GO, _W_MERGE, _W_END = 1408, 1420, 2444, 2460, 2972, 6044


def _wsplit_kernel(w_ref, att_ref, gla_ref, go_ref, glr_ref, ng_ref, merge_ref):
    w = w_ref[0]
    att_ref[...] = w[:, 0:_W_NG].astype(BF16)
    ng_ref[...] = w[:, _W_NG:_W_GLA].astype(BF16)
    gla_ref[...] = w[:, _W_GLA:_W_GLR].astype(BF16)
    glr_ref[...] = w[:, _W_GLR:_W_GO].astype(BF16)
    go_ref[...] = w[:, _W_GO:_W_MERGE].astype(BF16)
    merge_ref[...] = w[:, _W_MERGE:_W_END].astype(BF16)


def _wsplit(w_in, layer, tr=256):
    widths = [_W_NG, _W_GLR - _W_GLA, _W_MERGE - _W_GO, _W_GO - _W_GLR, _W_GLA - _W_NG, _W_END - _W_MERGE]
    return pl.pallas_call(
        _wsplit_kernel,
        grid=(D_MODEL // tr,),
        in_specs=[pl.BlockSpec((1, tr, _W_END), lambda i: (layer, i, 0))],
        out_specs=[pl.BlockSpec((tr, w), lambda i: (i, 0)) for w in widths],
        out_shape=[jax.ShapeDtypeStruct((D_MODEL, w), BF16) for w in widths],
        compiler_params=_params(("arbitrary",)),
        name="wsplit",
    )(w_in)


def _proj_kernel(x_ref, an_ref, wa_ref, wg_ref, wo_ref, wm_ref, bd_ref, gmq_ref, gmk_ref, gnq_ref, gks_ref, gkw_ref,
                 gw_ref, gb_ref,
                 mqt_ref, mk_ref, mvt_ref, nqt_ref, kvc_ref, nkv_ref, nkvt_ref, gqk_ref, gv_ref, la_ref, go_ref,
                 ngt_ref, *, seq):
    tm = x_ref.shape[0]
    h = _rms_rows(x_ref[...], an_ref[...]).astype(BF16)
    pos = (pl.program_id(0) * tm + lax.broadcasted_iota(jnp.int32, (tm, LANE), 0)) % seq
    lane = lax.broadcasted_iota(jnp.int32, (tm, LANE), 1)

    y_all = {id(w_ref): _dot(h, w_ref[...]) for w_ref in (wa_ref, wg_ref, wo_ref, wm_ref)}

    def sec(w_ref, lo, hi):
        return y_all[id(w_ref)][:, lo:hi]

    def head_norm(y, gain):
        w = y.shape[1]
        ms = _dot((y * y).astype(BF16), bd_ref[0:w, 0:w])
        return y * lax.rsqrt(ms + NORM_EPS) * gain

    scale = HEAD_DIM ** -0.5 * LOG2_E
    mqt_ref[...] = (head_norm(sec(wa_ref, _C_MQ, _C_MK), gmq_ref[...]) * scale).T.astype(BF16)
    mk = head_norm(sec(wa_ref, _C_MK, _C_MV), gmk_ref[...])
    pos_h = (pl.program_id(0) * tm + lax.broadcasted_iota(jnp.int32, (tm, HEAD_DIM), 0)) % seq
    lane_h = lax.broadcasted_iota(jnp.int32, (tm, HEAD_DIM), 1)
    moba_onehot = jnp.where(lane_h == pos_h // MOBA_BLOCK, 1.0, 0.0)
    for hd in range(N_HEADS):
        mk_ref[:, hd * LANE:(hd + 1) * LANE] = jnp.concatenate(
            [mk[:, hd * HEAD_DIM:(hd + 1) * HEAD_DIM], moba_onehot], axis=1).astype(BF16)
    mvt_ref[...] = sec(wa_ref, _C_MV, _C_NQ).T.astype(BF16)
    nqt_ref[...] = (head_norm(sec(wa_ref, _C_NQ, _C_KVC), gnq_ref[...]) * scale).T.astype(BF16)
    kvc_ref[...] = sec(wa_ref, _C_KVC, _C_KSVS).astype(BF16)
    first_half = lane < HEAD_DIM
    ksvs = sec(wa_ref, _C_KSVS, _C_KWVW)
    ksvs = jnp.where(first_half, head_norm(ksvs, gks_ref[...]), ksvs)
    nkvt_ref[0:128, :] = ksvs.T.astype(BF16)
    sel_onehot = jnp.where(lane - HEAD_DIM == pos // NSA_SEL_LEN, 1.0, 0.0)
    nkv_ref[:, 0:128] = jnp.where(first_half, ksvs, sel_onehot).astype(BF16)
    kwvw = sec(wa_ref, _C_KWVW, _C_A_END)
    kwvw = jnp.where(first_half, head_norm(kwvw, gkw_ref[...]), kwvw)
    nkv_ref[:, 128:256] = kwvw.astype(BF16)
    nkvt_ref[128:256, :] = kwvw.T.astype(BF16)
    gqk_ref[:, 0:256] = (sec(wg_ref, 0, 256) * (GLA_DK ** -0.5)).astype(BF16)
    gqk_ref[:, 256:512] = sec(wg_ref, 256, 512).astype(BF16)
    gv_ref[...] = sec(wg_ref, 512, 1024).astype(BF16)
    go_ref[...] = jax.nn.silu(sec(wo_ref, 0, 512)).astype(BF16)
    misc = sec(wm_ref, 0, LANE)
    ngt_ref[...] = jax.nn.sigmoid(misc).T
    pre = _dot(misc.astype(BF16), gw_ref[...]) + gb_ref[...]
    la_ref[...] = (jnp.minimum(pre, 0.0) - jnp.log(1.0 + jnp.exp(-jnp.abs(pre)))) * (1.0 / GLA_GATE_NORM)


def _proj(x, an, w_att, w_gla, w_go, w_misc, bd, gmq, gmk, gnq, gks, gkw, gw, gb, seq, tm=512):
    n = x.shape[0]
    row = lambda w: pl.BlockSpec((tm, w), lambda i: (i, 0))
    col = lambda w: pl.BlockSpec((w, tm), lambda i: (0, i))
    outs = [(256, BF16, True), (N_HEADS * LANE, BF16, False), (256, BF16, True), (256, BF16, True),
            (128, BF16, False), (256, BF16, False), (256, BF16, True), (512, BF16, False), (512, BF16, False),
            (256, F32, False), (512, BF16, False), (128, F32, True)]
    return pl.pallas_call(
        functools.partial(_proj_kernel, seq=seq),
        grid=(n // tm,),
        in_specs=[row(D_MODEL), _const_spec((1, D_MODEL)), _weight_spec(w_att.shape), _weight_spec(w_gla.shape),
                  _weight_spec(w_go.shape), _weight_spec(w_misc.shape), _const_spec((256, 256)),
                  _const_spec((1, 256)), _const_spec((1, 256)), _const_spec((1, 256)), _const_spec((1, 128)),
                  _const_spec((1, 128)), _const_spec((LANE, 256)), _const_spec((1, 256))],
        out_specs=[col(w) if tr else row(w) for w, _, tr in outs],
        out_shape=[jax.ShapeDtypeStruct((w, n) if tr else (n, w), dt) for w, dt, tr in outs],
        compiler_params=_params(("arbitrary",)),
        name="proj",
    )(x, an, w_att, w_gla, w_go, w_misc, bd, gmq, gmk, gnq, gks, gkw, gw, gb)


def _attend_init(dv, nq):
    return (jnp.full((1, nq), NEG_INF, F32), jnp.zeros((1, nq), F32), jnp.zeros((dv, nq), F32))


def _attend(carries, scores, vts, shifts):
    stats = []
    for (m, _, _), s, shift in zip(carries, scores, shifts):
        tile_max = jnp.max(s, axis=0, keepdims=True)
        if shift is not None:
            tile_max = tile_max + shift
        m_new = jnp.maximum(m, tile_max)
        p = jnp.exp2(s - (m_new if shift is None else m_new - shift))
        stats.append((m_new, jnp.exp2(m - m_new), p))
    pvs = [_dot(vt, p.astype(BF16)) for vt, (_, _, p) in zip(vts, stats)]
    return tuple((m_new, alpha * l + jnp.sum(p, axis=0, keepdims=True), alpha * acc + pv)
                 for (_, l, acc), (m_new, alpha, p), pv in zip(carries, stats, pvs))


FAR_GROUP = 4


def _far_tiles(n, carries, tile_fn):
    groups = n // FAR_GROUP
    carries = lax.fori_loop(0, groups, lambda g, c: tile_fn(g * FAR_GROUP, c, FAR_GROUP), carries)
    done = groups * FAR_GROUP
    width = FAR_GROUP // 2
    while width >= 1:
        take = ((n - done) // width) % 2 == 1
        carries = lax.cond(take, lambda c, done=done, width=width: tile_fn(done, c, width), lambda c: c, carries)
        done = done + jnp.where(take, width, 0)
        width //= 2
    return carries


def _rank_rows(score, n):
    idx = lax.broadcasted_iota(jnp.int32, score.shape, 0)
    rank = jnp.zeros(score.shape, F32)
    for m in range(n):
        sm = score[m:m + 1, :]
        rank += jnp.where((sm > score) | ((sm == score) & (idx > m)), 1.0, 0.0)
    return rank


def _moba_kernel(qt_ref, k_ref, vt_ref, bias_ref, far_ref, o_ref, kmean_ref):
    t = ATT_TILE
    nh = N_HEADS
    nb = k_ref.shape[0] // t
    qi = pl.program_id(1)
    kcols = [slice(h * LANE, (h + 1) * LANE) for h in range(nh)]
    vrows = [slice(h * HEAD_DIM, (h + 1) * HEAD_DIM) for h in range(nh)]

    @pl.when(qi == 0)
    def _():
        for h in range(nh):
            kh = k_ref[:, h * LANE:h * LANE + HEAD_DIM].astype(F32)
            kmean_ref[h] = jnp.mean(kh.reshape(nb, t, HEAD_DIM), axis=1)

    past = lax.broadcasted_iota(jnp.int32, (nb, t), 0) < qi
    q_aug = []
    for h in range(nh):
        qt = qt_ref[vrows[h], :]
        gate = jnp.where(past, _dot(kmean_ref[h].astype(BF16), qt), NEG_INF)
        dropped = past & (_rank_rows(gate, nb) >= MOBA_TOPK)
        pen = jnp.concatenate([jnp.where(dropped, NEG_INF, 0.0), jnp.zeros((16 - nb, t), F32)], axis=0)
        q_aug.append(jnp.concatenate([qt, pen.astype(BF16), jnp.zeros((LANE - HEAD_DIM - 16, t), BF16)], axis=0))

    def tile(j, carries, near_rows, width):
        keys = pl.ds(pl.multiple_of(j * t, t), width * t)
        scores = [_dot(k_ref[keys, kcols[h]], q_aug[h]) for h in range(nh)]
        if near_rows is None:
            shifts = [far_ref[h, 0:1, :] for h in range(nh)]
        else:
            scores = [s + bias_ref[h, near_rows, :] for h, s in enumerate(scores)]
            shifts = [None] * nh
        return _attend(carries, scores, [vt_ref[vrows[h], keys] for h in range(nh)], shifts)

    carries = lax.cond(qi >= 1,
                       lambda c: tile(jnp.maximum(qi - 1, 0), c, slice(0, 2 * t), 2),
                       lambda c: tile(qi, c, slice(t, 2 * t), 1),
                       tuple(_attend_init(HEAD_DIM, t) for _ in range(nh)))
    carries = _far_tiles(jnp.maximum(qi - 1, 0), carries, lambda j, c, width: tile(j, c, None, width))
    o_ref[...] = jnp.concatenate([acc / l for _, l, acc in carries], axis=0).T.astype(BF16)


def _moba(mqt, mk, mvt, bias, far, batch, seq):
    t = ATT_TILE
    nq = seq // t
    return pl.pallas_call(
        _moba_kernel,
        grid=(batch, nq),
        in_specs=[pl.BlockSpec((256, t), lambda b, i: (0, b * nq + i)),
                  pl.BlockSpec((seq, N_HEADS * LANE), lambda b, i: (b, 0)),
                  pl.BlockSpec((256, seq), lambda b, i: (0, b)),
                  _const_spec((N_HEADS, 2 * t, t)), _const_spec((N_HEADS, 8, t))],
        out_specs=pl.BlockSpec((t, 256), lambda b, i: (b * nq + i, 0)),
        out_shape=jax.ShapeDtypeStruct((batch * seq, 256), BF16),
        scratch_shapes=[pltpu.VMEM((N_HEADS, nq, HEAD_DIM), F32)],
        compiler_params=_params(("arbitrary", "arbitrary")),
        name="moba",
    )(mqt, mk, mvt, bias, far)


def _compress_kernel(x_ref, pa_ref, pb_ref, w1a_ref, w1b_ref, w2_ref, gk_ref, o_ref, ot_ref):
    x = x_ref[0].astype(F32)
    u = _dot((x + pa_ref[...]).astype(BF16), w1a_ref[...])
    v = _dot((x + pb_ref[...]).astype(BF16), w1b_ref[...])
    n = u.shape[0]
    hid = u + pltpu.roll(v, n - 1, axis=0)
    y = _dot(jax.nn.gelu(hid, approximate=True).astype(BF16), w2_ref[...])
    is_k = lax.broadcasted_iota(jnp.int32, y.shape, 1) < HEAD_DIM
    ms = jnp.sum(jnp.where(is_k, y * y, 0.0), axis=1, keepdims=True) * (1.0 / HEAD_DIM)
    y = jnp.where(is_k, y * lax.rsqrt(ms + NORM_EPS) * gk_ref[...], y)
    o_ref[0] = y.astype(BF16)
    ot_ref[0] = y.T.astype(BF16)


def _compress(kvc3, pa, pb, w1a, w1b, w2, gk):
    batch, n, w = kvc3.shape
    return pl.pallas_call(
        _compress_kernel,
        grid=(batch,),
        in_specs=[pl.BlockSpec((1, n, w), lambda b: (b, 0, 0)), _const_spec((1, w)), _const_spec((1, w)),
                  _const_spec((w, LANE)), _const_spec((w, LANE)), _const_spec((LANE, LANE)), _const_spec((1, LANE))],
        out_specs=[pl.BlockSpec((1, n, LANE), lambda b: (b, 0, 0)), pl.BlockSpec((1, LANE, n), lambda b: (b, 0, 0))],
        out_shape=[jax.ShapeDtypeStruct((batch, n, LANE), BF16), jax.ShapeDtypeStruct((batch, LANE, n), BF16)],
        compiler_params=_params(("arbitrary",)),
        name="compress",
    )(kvc3, pa, pb, w1a, w1b, w2, gk)


def _nsa_kernel(qt_ref, kc_ref, kct_ref, kv_ref, kvt_ref, ngt_ref, bias_ref, far_ref, ovt_ref, o_ref):
    t = ATT_TILE
    nh = N_HEADS
    qi = pl.program_id(1)
    qt_all = qt_ref[...]
    qs = jnp.concatenate([qt_all[h * HEAD_DIM:(h + 1) * HEAD_DIM, :] for h in range(nh)], axis=1)

    kc = kc_ref[0][:, 0:HEAD_DIM]
    vct = kct_ref[0][HEAD_DIM:2 * HEAD_DIM, :]
    n_cmp = kc.shape[0]
    pos = qi * t + (lax.broadcasted_iota(jnp.int32, (n_cmp, nh * t), 1) & (t - 1))
    cend = lax.broadcasted_iota(jnp.int32, (n_cmp, nh * t), 0) * NSA_CMP_STRIDE + (NSA_CMP_LEN - 1)
    vis = cend <= pos
    sc = jnp.where(vis, _dot(kc, qs), NEG_INF)
    e = jnp.where(vis, jnp.exp2(sc - jnp.max(sc, axis=0, keepdims=True)), 0.0)
    den = jnp.sum(e, axis=0, keepdims=True)
    p = e / jnp.where(den > 0.0, den, 1.0)
    o_cmp = _dot(vct, p.astype(BF16))

    p_sum = p[:, 0:t] + p[:, t:2 * t] + p[:, 2 * t:3 * t] + p[:, 3 * t:4 * t]
    p_hi = p_sum.astype(BF16)
    p_lo = (p_sum - p_hi.astype(F32)).astype(BF16)
    ovt = ovt_ref[...]
    n_blk = ovt.shape[0]
    imp = _dot(ovt, p_hi) + _dot(ovt, p_lo)
    blk = lax.broadcasted_iota(jnp.int32, (n_blk, t), 0)
    cur = (qi * t + lax.broadcasted_iota(jnp.int32, (n_blk, t), 1)) // NSA_SEL_LEN
    forced = (blk == 0) | (blk == cur) | (blk == cur - 1)
    valid = blk <= cur
    imp = jnp.where(valid, imp + jnp.where(forced, NSA_FORCE_BONUS, 0.0), NEG_INF)
    keep = valid & (_rank_rows(imp, n_blk) < NSA_SEL_TOPN)
    pen = jnp.where(keep, 0.0, NEG_INF).astype(BF16)
    q_aug = jnp.concatenate([qs, jnp.concatenate([pen] * nh, axis=1),
                             jnp.zeros((LANE - HEAD_DIM - n_blk, nh * t), BF16)], axis=0)

    sel = (slice(0, LANE), q_aug, slice(HEAD_DIM, 2 * HEAD_DIM))
    win = (slice(LANE, LANE + HEAD_DIM), qs, slice(LANE + HEAD_DIM, 2 * LANE))

    def tiles(carries, branches, widths):
        scores, vts = [], []
        for (k_lanes, q, v_rows), width in zip(branches, widths):
            keys = pl.ds(pl.multiple_of(jnp.maximum(qi + 1 - width, 0) * t, t), width * t)
            scores.append(_dot(kv_ref[keys, k_lanes], q) + bias_ref[(3 - width) * t:3 * t, :])
            vts.append(kvt_ref[v_rows, keys])
        return _attend(carries, scores, vts, [None] * len(branches))

    def far_tiles(j, carries, width):
        keys = pl.ds(pl.multiple_of(j * t, t), width * t)
        k_lanes, q, v_rows = sel
        return _attend(carries, [_dot(kv_ref[keys, k_lanes], q)], [kvt_ref[v_rows, keys]], [far_ref[0:1, :]])

    init = _attend_init(HEAD_DIM, nh * t)
    c_sel, c_win = lax.switch(jnp.minimum(qi, 2),
                              [lambda c: tiles(c, [sel, win], (1, 1)),
                               lambda c: tiles(c, [sel, win], (2, 2)),
                               lambda c: tiles(c, [sel, win], (2, 3))], (init, init))
    (_, l, acc), = _far_tiles(jnp.maximum(qi - 1, 0), (c_sel,), far_tiles)
    o_slc = acc / l
    _, l, acc = c_win
    o_win = acc / l

    outs = []
    for h in range(nh):
        cs = slice(h * t, (h + 1) * t)
        g = [ngt_ref[_MISC_NGATE + 3 * h + i:_MISC_NGATE + 3 * h + i + 1, :] for i in range(3)]
        outs.append(g[0] * o_cmp[:, cs] + g[1] * o_slc[:, cs] + g[2] * o_win[:, cs])
    o_ref[...] = jnp.concatenate(outs, axis=0).T.astype(BF16)


def _nsa(nqt, kc, kct, nkv, nkvt, ngt, bias, far, ovt, batch, seq):
    t = ATT_TILE
    nq = seq // t
    assert NSA_WINDOW == 2 * t and seq // NSA_SEL_LEN <= LANE - HEAD_DIM
    n_cmp = kc.shape[1]
    n_blk = ovt.shape[0]
    return pl.pallas_call(
        _nsa_kernel,
        grid=(batch, nq),
        in_specs=[pl.BlockSpec((256, t), lambda b, i: (0, b * nq + i)),
                  pl.BlockSpec((1, n_cmp, LANE), lambda b, i: (b, 0, 0)),
                  pl.BlockSpec((1, LANE, n_cmp), lambda b, i: (b, 0, 0)),
                  pl.BlockSpec((seq, 256), lambda b, i: (b, 0)),
                  pl.BlockSpec((256, seq), lambda b, i: (0, b)),
                  pl.BlockSpec((LANE, t), lambda b, i: (0, b * nq + i)),
                  _weight_spec((3 * t, N_HEADS * t)), _const_spec((8, N_HEADS * t)),
                  _const_spec((n_blk, n_cmp))],
        out_specs=pl.BlockSpec((t, 256), lambda b, i: (b * nq + i, 0)),
        out_shape=jax.ShapeDtypeStruct((batch * seq, 256), BF16),
        compiler_params=_params(("arbitrary", "arbitrary")),
        name="nsa",
    )(nqt, kc, kct, nkv, nkvt, ngt, bias, far, ovt)


def _gla_kernel(qk_ref, v_ref, la_ref, go_ref, tri_ref, gn_ref, o_ref, st_ref):
    c = GLA_CHUNK
    nh = N_HEADS
    n_chunk = qk_ref.shape[0] // c
    st_ref[...] = jnp.zeros_like(st_ref)
    tri = tri_ref[...]
    row = lax.broadcasted_iota(jnp.int32, (c, nh * GLA_DK), 0)
    sub_causal = (lax.broadcasted_iota(jnp.int32, (GLA_SUB, c), 0)
                  - lax.broadcasted_iota(jnp.int32, (GLA_SUB, c), 1))

    hks = [slice(h * GLA_DK, (h + 1) * GLA_DK) for h in range(nh)]
    hvs = [slice(h * GLA_DV, (h + 1) * GLA_DV) for h in range(nh)]
    group_size = GLA_UNROLL

    def group(gi, _):
        rows = [pl.ds(pl.multiple_of((gi * group_size + u) * c, c), c) for u in range(group_size)]
        units = range(group_size)

        bs = []
        for u in units:
            g = la_ref[rows[u], :]
            g1 = g.astype(BF16)
            r1 = g - g1.astype(F32)
            g2 = r1.astype(BF16)
            g3 = (r1 - g2.astype(F32)).astype(BF16)
            bs.append(_dot(tri, g1) + _dot(tri, g2) + _dot(tri, g3))

        q_inter, k_state, decay, q_sub, k_sub, vs_ = [], [], [], [], [], []
        for u in units:
            b = bs[u]
            q = qk_ref[rows[u], 0:256].astype(F32)
            k = qk_ref[rows[u], 256:512].astype(F32)
            b_last = b[c - 1:c, :]
            q_inter.append((q * jnp.exp(b)).astype(BF16))
            k_state.append((k * jnp.exp(b_last - b)).astype(BF16))
            decay.append(jnp.exp(b_last))
            qs_u, ks_u = [], []
            for i in range(c // GLA_SUB):
                lo, hi = i * GLA_SUB, (i + 1) * GLA_SUB
                ref_b = b[lo:lo + 1, :]
                ks_u.append((k * jnp.exp(jnp.where(row < hi, ref_b - b, 0.0))).astype(BF16))
                qs_u.append((q[lo:hi] * jnp.exp(b[lo:hi] - ref_b)).astype(BF16))
            q_sub.append(qs_u)
            k_sub.append(ks_u)
            vs_.append(v_ref[rows[u], :])

        o_intra, kv = [], []
        for u in units:
            a_h = []
            for h in range(nh):
                blocks = [jnp.where(sub_causal + i * GLA_SUB >= 0, _dot_nt(q_sub[u][i][:, hks[h]], k_sub[u][i][:, hks[h]]), 0.0)
                          for i in range(c // GLA_SUB)]
                a_h.append(jnp.concatenate(blocks, axis=0).astype(BF16))
            o_intra.append([_dot(a_h[h], vs_[u][:, hvs[h]]) for h in range(nh)])
            kv.append([_dot_tn(vs_[u][:, hvs[h]], k_state[u][:, hks[h]]) for h in range(nh)])

        st = [st_ref[h] for h in range(nh)]
        for u in units:
            outs = []
            for h in range(nh):
                o = o_intra[u][h] + _dot_nt(q_inter[u][:, hks[h]], st[h].astype(BF16))
                st[h] = st[h] * decay[u][:, hks[h]] + kv[u][h]
                outs.append(_rms_rows(o, gn_ref[...]))
            o_ref[rows[u], :] = (jnp.concatenate(outs, axis=1) * go_ref[rows[u], :].astype(F32)).astype(BF16)
        for h in range(nh):
            st_ref[h] = st[h]
        return 0

    lax.fori_loop(0, n_chunk // group_size, group, 0)


def _gla(gqk, gv, la, go, tri, gn, batch, seq):
    spec = lambda w: pl.BlockSpec((seq, w), lambda b: (b, 0))
    return pl.pallas_call(
        _gla_kernel,
        grid=(batch,),
        in_specs=[spec(512), spec(512), spec(256), spec(512), _const_spec((GLA_CHUNK, GLA_CHUNK)),
                  _const_spec((1, GLA_DV))],
        out_specs=spec(512),
        out_shape=jax.ShapeDtypeStruct((batch * seq, 512), BF16),
        scratch_shapes=[pltpu.VMEM((N_HEADS, GLA_DV, GLA_DK), F32)],
        compiler_params=_params(("arbitrary",)),
        name="gla",
    )(gqk, gv, la, go, tri, gn)


def _merge_kernel(x_ref, an_ref, om_ref, on_ref, og_ref, wm_ref, pm_ref, pn_ref, pg_ref, wo_ref, o_ref):
    x = x_ref[...]
    h = _rms_rows(x, an_ref[...]).astype(BF16)
    z = jax.nn.sigmoid(_dot(h, wm_ref[:, 0:D_MODEL])) * _dot(om_ref[...], pm_ref[...])
    z += jax.nn.sigmoid(_dot(h, wm_ref[:, D_MODEL:2 * D_MODEL])) * _dot(on_ref[...], pn_ref[...])
    z += jax.nn.sigmoid(_dot(h, wm_ref[:, 2 * D_MODEL:3 * D_MODEL])) * _dot(og_ref[...], pg_ref[...])
    o_ref[...] = x + _dot(z.astype(BF16), wo_ref[...])


def _merge(x, an, om, on, og, wm, pm, pn, pg, wo, tm=512):
    n = x.shape[0]
    row = lambda w: pl.BlockSpec((tm, w), lambda i: (i, 0))
    return pl.pallas_call(
        _merge_kernel,
        grid=(n // tm,),
        in_specs=[row(D_MODEL), _const_spec((1, D_MODEL)), row(256), row(256), row(512),
                  _weight_spec((D_MODEL, 3 * D_MODEL)), _weight_spec((256, D_MODEL)), _weight_spec((256, D_MODEL)),
                  _weight_spec((512, D_MODEL)), _weight_spec((D_MODEL, D_MODEL))],
        out_specs=row(D_MODEL),
        out_shape=jax.ShapeDtypeStruct((n, D_MODEL), F32),
        compiler_params=_params(("arbitrary",)),
        name="merge",
    )(x, an, om, on, og, wm, pm, pn, pg, wo)


FFN_CHUNK = 256


def _ffn_kernel(x_ref, fn_ref, wa_ref, wg_ref, cw_ref, cb_ref, wd_ref, o_ref, carry_ref, act_ref, *, tiles_per_seq):
    i = pl.program_id(0)
    tm = x_ref.shape[0]
    x = x_ref[...]
    h = _rms_rows(x, fn_ref[...]).astype(BF16)
    row = lax.broadcasted_iota(jnp.int32, (tm, FFN_CHUNK), 0)

    @pl.when((i % tiles_per_seq) == 0)
    def _():
        carry_ref[...] = jnp.zeros_like(carry_ref)

    for c in range(D_FF // FFN_CHUNK):
        cs = slice(c * FFN_CHUNK, (c + 1) * FFN_CHUNK)
        a = _dot(h, wa_ref[:, cs])
        g = _dot(h, wg_ref[:, cs])
        prev = carry_ref[:, cs]
        p1 = prev[7:8, :]
        p2 = prev[6:7, :]
        a1 = jnp.where(row == 0, p1, pltpu.roll(a, 1, axis=0))
        a2 = jnp.where(row == 0, p2, jnp.where(row == 1, p1, pltpu.roll(a, 2, axis=0)))
        carry_ref[:, cs] = a[tm - 8:tm, :]
        w = cw_ref[:, cs]
        conv = w[0:1, :] * a2 + w[1:2, :] * a1 + w[2:3, :] * a + cb_ref[:, cs]
        act_ref[:, cs] = (jax.nn.gelu(conv, approximate=True) * g).astype(BF16)
    o_ref[...] = x + _dot(act_ref[...], wd_ref[...])


def _ffn(x, fn, wa, wg, cw, cb, wd, seq, tm=1024):
    n = x.shape[0]
    row = pl.BlockSpec((tm, D_MODEL), lambda i: (i, 0))
    return pl.pallas_call(
        functools.partial(_ffn_kernel, tiles_per_seq=seq // tm),
        grid=(n // tm,),
        in_specs=[row, _const_spec((1, D_MODEL)), _weight_spec((D_MODEL, D_FF)), _weight_spec((D_MODEL, D_FF)),
                  _const_spec((8, D_FF)), _const_spec((1, D_FF)), _weight_spec((D_FF, D_MODEL))],
        out_specs=row,
        out_shape=jax.ShapeDtypeStruct((n, D_MODEL), F32),
        scratch_shapes=[pltpu.VMEM((8, D_FF), F32), pltpu.VMEM((tm, D_FF), BF16)],
        compiler_params=_params(("arbitrary",)),
        name="ffn",
    )(x, fn, wa, wg, cw, cb, wd)


def _rel_bucket(dist):
    n = jnp.maximum(dist, 0)
    max_exact = REL_BUCKETS // 2
    nf = jnp.maximum(n, 1).astype(F32)
    large = max_exact + (jnp.log(nf / max_exact) / math.log(REL_MAX_DIST / max_exact)
                         * (REL_BUCKETS - max_exact)).astype(jnp.int32)
    return jnp.where(n < max_exact, n, jnp.minimum(large, REL_BUCKETS - 1))


def _bias_tiles(rel_tab):
    t = ATT_TILE
    d0 = jnp.arange(t)[None, :] - jnp.arange(t)[:, None]
    bucket = jnp.stack([_rel_bucket(d0 + k * t) for k in range(3)])
    out = jnp.zeros((rel_tab.shape[1],) + bucket.shape, F32)
    for b in range(REL_BUCKETS):
        out = jnp.where(bucket[None] == b, rel_tab[b][:, None, None, None], out)
    return out


def _block_diag_mean():
    g = jnp.arange(256) // HEAD_DIM
    return jnp.where(g[:, None] == g[None, :], 1.0 / HEAD_DIM, 0.0).astype(BF16)


def _overlap_t(n_cmp_pad, n_cmp, n_blk):
    tok = jnp.arange(n_blk * NSA_SEL_LEN)
    starts = jnp.arange(n_cmp_pad) * NSA_CMP_STRIDE
    inside = (tok[None, :] >= starts[:, None]) & (tok[None, :] < starts[:, None] + NSA_CMP_LEN)
    m = inside.reshape(n_cmp_pad, n_blk, NSA_SEL_LEN).sum(-1).astype(F32) / NSA_CMP_LEN
    m = jnp.where(jnp.arange(n_cmp_pad)[:, None] < n_cmp, m, 0.0)
    return m.T.astype(BF16)


def _tile_gain(g, reps):
    return jnp.tile(g.astype(F32), reps)[None, :]


def kernel(x, rel_bias, attn_norm, w_in, moba_q_norm, moba_k_norm, nsa_q_norm, nsa_k_norm, cmp_pos_k, cmp_pos_v,
           cmp_k_w1, cmp_k_w2, cmp_v_w1, cmp_v_w2, gla_gate_w, gla_gate_b, gla_out_norm, w_branch_moba,
           w_branch_nsa, w_branch_gla, w_out, ffn_norm, w_up, conv_w, conv_b, w_down):
    batch, seq, _ = x.shape
    depth = w_in.shape[0]
    n_cmp = seq // NSA_CMP_STRIDE - NSA_CMP_LEN // NSA_CMP_STRIDE + 1
    n_cmp_pad = seq // NSA_CMP_STRIDE
    n_blk = seq // NSA_SEL_LEN

    tiles = _bias_tiles(rel_bias.astype(F32)) * LOG2_E
    key_le_query = jnp.arange(ATT_TILE)[:, None] <= jnp.arange(ATT_TILE)[None, :]
    own = jnp.where(key_le_query, tiles[:, 0], NEG_INF)
    band = jnp.where(key_le_query, NEG_INF, tiles[:, 2])
    near, far = tiles[:, 1], tiles[:, 2, :8]
    bias_moba = jnp.concatenate([near[:N_HEADS], own[:N_HEADS]], axis=1)
    far_moba = far[:N_HEADS]
    heads_on_lanes = lambda a: jnp.concatenate(list(a[N_HEADS:]), axis=1)
    bias_nsa = jnp.concatenate([heads_on_lanes(band), heads_on_lanes(near), heads_on_lanes(own)], axis=0)
    far_nsa = heads_on_lanes(far)
    bd = _block_diag_mean()
    ovt = _overlap_t(n_cmp_pad, n_cmp, n_blk)
    tri = (jnp.arange(GLA_CHUNK)[:, None] >= jnp.arange(GLA_CHUNK)[None, :]).astype(BF16)
    ones64 = jnp.ones((HEAD_DIM,), F32)

    xf = x.reshape(batch * seq, D_MODEL)
    for l in range(depth):
        w_att, w_gla, w_go, w_glr, w_ng, wmerge = _wsplit(w_in, l)
        w_misc = jnp.pad(jnp.concatenate([w_glr, w_ng], axis=1), ((0, 0), (0, LANE - GLA_GATE_RANK - 3 * N_HEADS)))
        gw = jnp.pad(gla_gate_w[l], ((0, LANE - GLA_GATE_RANK), (0, 0))).astype(BF16)

        mqt, mk_a, mvt, nqt, kvc, nkv_a, nkvt, gqk, gv_a, la, go, ngt = _proj(
            xf, attn_norm[l][None, :], w_att, w_gla, w_go, w_misc, bd,
            _tile_gain(moba_q_norm[l], 4), _tile_gain(moba_k_norm[l], 4), _tile_gain(nsa_q_norm[l], 4),
            jnp.concatenate([nsa_k_norm[l, 1], ones64])[None, :], jnp.concatenate([nsa_k_norm[l, 2], ones64])[None, :],
            gw, gla_gate_b[l][None, :], seq)

        o_moba = _moba(mqt, mk_a, mvt, bias_moba, far_moba, batch, seq)

        half = NSA_CMP_STRIDE * HEAD_DIM
        zero = jnp.zeros((NSA_CMP_STRIDE, HEAD_DIM, HEAD_DIM), F32)

        def w1_part(part):
            wk = cmp_k_w1[l][part * half:(part + 1) * half].reshape(NSA_CMP_STRIDE, HEAD_DIM, HEAD_DIM)
            wv = cmp_v_w1[l][part * half:(part + 1) * half].reshape(NSA_CMP_STRIDE, HEAD_DIM, HEAD_DIM)
            top = jnp.concatenate([wk, zero], axis=2)
            bot = jnp.concatenate([zero, wv], axis=2)
            return jnp.concatenate([top, bot], axis=1).reshape(NSA_CMP_STRIDE * LANE, LANE).astype(BF16)

        def pos_part(part):
            pk = cmp_pos_k[l][part * NSA_CMP_STRIDE:(part + 1) * NSA_CMP_STRIDE]
            pv = cmp_pos_v[l][part * NSA_CMP_STRIDE:(part + 1) * NSA_CMP_STRIDE]
            return jnp.concatenate([pk, pv], axis=1).reshape(1, NSA_CMP_STRIDE * LANE).astype(F32)

        z64 = jnp.zeros((HEAD_DIM, HEAD_DIM), F32)
        w2 = jnp.concatenate([jnp.concatenate([cmp_k_w2[l], z64], axis=1),
                              jnp.concatenate([z64, cmp_v_w2[l]], axis=1)], axis=0).astype(BF16)
        kcv, kcvt = _compress(kvc.reshape(batch, n_cmp_pad, NSA_CMP_STRIDE * LANE), pos_part(0), pos_part(1),
                              w1_part(0), w1_part(1), w2, jnp.concatenate([nsa_k_norm[l, 0], ones64])[None, :])

        o_nsa = _nsa(nqt, kcv, kcvt, nkv_a, nkvt, ngt, bias_nsa, far_nsa, ovt, batch, seq)
        o_gla = _gla(gqk, gv_a, la, go, tri, gla_out_norm[l][None, :].astype(F32), batch, seq)

        xf = _merge(xf, attn_norm[l][None, :], o_moba, o_nsa, o_gla, wmerge.astype(BF16),
                    w_branch_moba[l].astype(BF16), w_branch_nsa[l].astype(BF16), w_branch_gla[l].astype(BF16),
                    w_out[l].astype(BF16))

        cw = jnp.pad(conv_w[l], ((0, 8 - conv_w.shape[1]), (0, 0)))
        xf = _ffn(xf, ffn_norm[l][None, :], w_up[l][:, :D_FF].astype(BF16), w_up[l][:, D_FF:].astype(BF16),
                  cw, conv_b[l][None, :], w_down[l].astype(BF16), seq)
    return xf.reshape(batch, seq, D_MODEL)
```

```python
import functools
import math

import jax
import jax.numpy as jnp
from jax import lax
from jax.experimental import pallas as pl
from jax.experimental.pallas import tpu as pltpu

F32 = jnp.float32
BF16 = jnp.bfloat16

D_MODEL = 1024
HEAD_DIM = 64
N_HEADS = 4
MOBA_BLOCK = 256
MOBA_TOPK = 3
NSA_CMP_LEN = 32
NSA_CMP_STRIDE = 16
NSA_SEL_LEN = 64
NSA_SEL_TOPN = 16
NSA_WINDOW = 512
NSA_FORCE_BONUS = 1e4
GLA_DK = 64
GLA_DV = 128
GLA_GATE_RANK = 16
GLA_GATE_NORM = 16.0
GLA_CHUNK = 64
GLA_SUB = 16
GLA_UNROLL = 8
D_FF = 2816
REL_BUCKETS = 32
REL_MAX_DIST = 128
NORM_EPS = 1e-6
NEG_INF = -1e30
LOG2_E = math.log2(math.e)

ATT_TILE = 256
LANE = 128
VMEM_LIMIT = 56 * 1024 * 1024

_C_MQ, _C_MK, _C_MV, _C_NQ = 0, 256, 512, 768
_C_KVC, _C_KSVS, _C_KWVW, _C_A_END = 1024, 1152, 1280, 1408
_MISC_NGATE = GLA_GATE_RANK


def _dot(a, b):
    return jnp.dot(a, b, preferred_element_type=F32)


def _dot_nt(a, b):
    return lax.dot_general(a, b, (((1,), (1,)), ((), ())), preferred_element_type=F32)


def _dot_tn(a, b):
    return lax.dot_general(a, b, (((0,), (0,)), ((), ())), preferred_element_type=F32)


def _rms_rows(x, gain):
    ms = jnp.mean(x * x, axis=-1, keepdims=True)
    return x * lax.rsqrt(ms + NORM_EPS) * gain


def _params(sem):
    return pltpu.CompilerParams(dimension_semantics=sem, vmem_limit_bytes=VMEM_LIMIT)


def _const_spec(shape):
    return pl.BlockSpec(shape, lambda *_: (0,) * len(shape))


def _weight_spec(shape):
    return pl.BlockSpec(shape, lambda *_: (0,) * len(shape), pipeline_mode=pl.Buffered(1))


_W_NG, _W_GLA, _W_GLR, _W_GO, _W_MERGE, _W_END = 1408, 1420, 2444, 2460, 2972, 6044


def _wsplit_kernel(w_ref, att_ref, gla_ref, go_ref, glr_ref, ng_ref, merge_ref):
    w = w_ref[0]
    att_ref[...] = w[:, 0:_W_NG].astype(BF16)
    ng_ref[...] = w[:, _W_NG:_W_GLA].astype(BF16)
    gla_ref[...] = w[:, _W_GLA:_W_GLR].astype(BF16)
    glr_ref[...] = w[:, _W_GLR:_W_GO].astype(BF16)
    go_ref[...] = w[:, _W_GO:_W_MERGE].astype(BF16)
    merge_ref[...] = w[:, _W_MERGE:_W_END].astype(BF16)


def _wsplit(w_in, layer, tr=256):
    widths = [_W_NG, _W_GLR - _W_GLA, _W_MERGE - _W_GO, _W_GO - _W_GLR, _W_GLA - _W_NG, _W_END - _W_MERGE]
    return pl.pallas_call(
        _wsplit_kernel,
        grid=(D_MODEL // tr,),
        in_specs=[pl.BlockSpec((1, tr, _W_END), lambda i: (layer, i, 0))],
        out_specs=[pl.BlockSpec((tr, w), lambda i: (i, 0)) for w in widths],
        out_shape=[jax.ShapeDtypeStruct((D_MODEL, w), BF16) for w in widths],
        compiler_params=_params(("arbitrary",)),
        name="wsplit",
    )(w_in)


def _proj_kernel(x_ref, an_ref, wa_ref, wg_ref, wo_ref, wm_ref, bd_ref, gmq_ref, gmk_ref, gnq_ref, gks_ref, gkw_ref,
                 gw_ref, gb_ref,
                 mqt_ref, mk_ref, mvt_ref, nqt_ref, kvc_ref, nkv_ref, nkvt_ref, gqk_ref, gv_ref, la_ref, go_ref,
                 ngt_ref, *, seq):
    tm = x_ref.shape[0]
    h = _rms_rows(x_ref[...], an_ref[...]).astype(BF16)
    pos = (pl.program_id(0) * tm + lax.broadcasted_iota(jnp.int32, (tm, LANE), 0)) % seq
    lane = lax.broadcasted_iota(jnp.int32, (tm, LANE), 1)

    y_all = {id(w_ref): _dot(h, w_ref[...]) for w_ref in (wa_ref, wg_ref, wo_ref, wm_ref)}

    def sec(w_ref, lo, hi):
        return y_all[id(w_ref)][:, lo:hi]

    def head_norm(y, gain):
        w = y.shape[1]
        ms = _dot((y * y).astype(BF16), bd_ref[0:w, 0:w])
        return y * lax.rsqrt(ms + NORM_EPS) * gain

    scale = HEAD_DIM ** -0.5 * LOG2_E
    mqt_ref[...] = (head_norm(sec(wa_ref, _C_MQ, _C_MK), gmq_ref[...]) * scale).T.astype(BF16)
    mk = head_norm(sec(wa_ref, _C_MK, _C_MV), gmk_ref[...])
    pos_h = (pl.program_id(0) * tm + lax.broadcasted_iota(jnp.int32, (tm, HEAD_DIM), 0)) % seq
    lane_h = lax.broadcasted_iota(jnp.int32, (tm, HEAD_DIM), 1)
    moba_onehot = jnp.where(lane_h == pos_h // MOBA_BLOCK, 1.0, 0.0)
    for hd in range(N_HEADS):
        mk_ref[:, hd * LANE:(hd + 1) * LANE] = jnp.concatenate(
            [mk[:, hd * HEAD_DIM:(hd + 1) * HEAD_DIM], moba_onehot], axis=1).astype(BF16)
    mvt_ref[...] = sec(wa_ref, _C_MV, _C_NQ).T.astype(BF16)
    nqt_ref[...] = (head_norm(sec(wa_ref, _C_NQ, _C_KVC), gnq_ref[...]) * scale).T.astype(BF16)
    kvc_ref[...] = sec(wa_ref, _C_KVC, _C_KSVS).astype(BF16)
    first_half = lane < HEAD_DIM
    ksvs = sec(wa_ref, _C_KSVS, _C_KWVW)
    ksvs = jnp.where(first_half, head_norm(ksvs, gks_ref[...]), ksvs)
    nkvt_ref[0:128, :] = ksvs.T.astype(BF16)
    sel_onehot = jnp.where(lane - HEAD_DIM == pos // NSA_SEL_LEN, 1.0, 0.0)
    nkv_ref[:, 0:128] = jnp.where(first_half, ksvs, sel_onehot).astype(BF16)
    kwvw = sec(wa_ref, _C_KWVW, _C_A_END)
    kwvw = jnp.where(first_half, head_norm(kwvw, gkw_ref[...]), kwvw)
    nkv_ref[:, 128:256] = kwvw.astype(BF16)
    nkvt_ref[128:256, :] = kwvw.T.astype(BF16)
    gqk_ref[:, 0:256] = (sec(wg_ref, 0, 256) * (GLA_DK ** -0.5)).astype(BF16)
    gqk_ref[:, 256:512] = sec(wg_ref, 256, 512).astype(BF16)
    gv_ref[...] = sec(wg_ref, 512, 1024).astype(BF16)
    go_ref[...] = jax.nn.silu(sec(wo_ref, 0, 512)).astype(BF16)
    misc = sec(wm_ref, 0, LANE)
    ngt_ref[...] = jax.nn.sigmoid(misc).T
    pre = _dot(misc.astype(BF16), gw_ref[...]) + gb_ref[...]
    la_ref[...] = (jnp.minimum(pre, 0.0) - jnp.log(1.0 + jnp.exp(-jnp.abs(pre)))) * (1.0 / GLA_GATE_NORM)


def _proj(x, an, w_att, w_gla, w_go, w_misc, bd, gmq, gmk, gnq, gks, gkw, gw, gb, seq, tm=512):
    n = x.shape[0]
    row = lambda w: pl.BlockSpec((tm, w), lambda i: (i, 0))
    col = lambda w: pl.BlockSpec((w, tm), lambda i: (0, i))
    outs = [(256, BF16, True), (N_HEADS * LANE, BF16, False), (256, BF16, True), (256, BF16, True),
            (128, BF16, False), (256, BF16, False), (256, BF16, True), (512, BF16, False), (512, BF16, False),
            (256, F32, False), (512, BF16, False), (128, F32, True)]
    return pl.pallas_call(
        functools.partial(_proj_kernel, seq=seq),
        grid=(n // tm,),
        in_specs=[row(D_MODEL), _const_spec((1, D_MODEL)), _weight_spec(w_att.shape), _weight_spec(w_gla.shape),
                  _weight_spec(w_go.shape), _weight_spec(w_misc.shape), _const_spec((256, 256)),
                  _const_spec((1, 256)), _const_spec((1, 256)), _const_spec((1, 256)), _const_spec((1, 128)),
                  _const_spec((1, 128)), _const_spec((LANE, 256)), _const_spec((1, 256))],
        out_specs=[col(w) if tr else row(w) for w, _, tr in outs],
        out_shape=[jax.ShapeDtypeStruct((w, n) if tr else (n, w), dt) for w, dt, tr in outs],
        compiler_params=_params(("arbitrary",)),
        name="proj",
    )(x, an, w_att, w_gla, w_go, w_misc, bd, gmq, gmk, gnq, gks, gkw, gw, gb)


def _softmax_pv(chains):
    def shifted(x, shift, sign):
        return x if shift is None else x + sign * shift

    maxes = [functools.reduce(jnp.maximum, [shifted(jnp.max(s, axis=0, keepdims=True), shift, 1)
                                            for s, _, shift in chain]) for chain in chains]
    probs = [[jnp.exp2(s - shifted(m, shift, -1)) for s, _, shift in chain] for chain, m in zip(chains, maxes)]
    sums = [functools.reduce(jnp.add, [jnp.sum(p, axis=0, keepdims=True) for p in ps]) for ps in probs]
    accs = [functools.reduce(jnp.add, [_dot(vt, p.astype(BF16)) for (_, vt, _), p in zip(chain, ps)])
            for chain, ps in zip(chains, probs)]
    return [acc / l for acc, l in zip(accs, sums)]


def _rank_rows(score, n):
    idx = lax.broadcasted_iota(jnp.int32, score.shape, 0)
    rank = jnp.zeros(score.shape, F32)
    for m in range(n):
        sm = score[m:m + 1, :]
        rank += jnp.where((sm > score) | ((sm == score) & (idx > m)), 1.0, 0.0)
    return rank


def _moba_kernel(qt_ref, k_ref, vt_ref, bias_ref, far_ref, o_ref, kmean_ref):
    t = ATT_TILE
    nh = N_HEADS
    nb = k_ref.shape[0] // t
    qi = pl.program_id(1)
    kcols = [slice(h * LANE, (h + 1) * LANE) for h in range(nh)]
    vrows = [slice(h * HEAD_DIM, (h + 1) * HEAD_DIM) for h in range(nh)]

    @pl.when(qi == 0)
    def _():
        for h in range(nh):
            kh = k_ref[:, h * LANE:h * LANE + HEAD_DIM].astype(F32)
            kmean_ref[h] = jnp.mean(kh.reshape(nb, t, HEAD_DIM), axis=1)

    past = lax.broadcasted_iota(jnp.int32, (nb, t), 0) < qi
    q_aug = []
    for h in range(nh):
        qt = qt_ref[vrows[h], :]
        gate = jnp.where(past, _dot(kmean_ref[h].astype(BF16), qt), NEG_INF)
        dropped = past & (_rank_rows(gate, nb) >= MOBA_TOPK)
        pen = jnp.concatenate([jnp.where(dropped, NEG_INF, 0.0), jnp.zeros((16 - nb, t), F32)], axis=0)
        q_aug.append(jnp.concatenate([qt, pen.astype(BF16), jnp.zeros((LANE - HEAD_DIM - 16, t), BF16)], axis=0))

    def attend(q_tile):
        width = min(q_tile + 1, 2)
        near = slice((q_tile + 1 - width) * t, (q_tile + 1) * t)
        far = slice(0, (q_tile + 1 - width) * t)

        def run():
            chains = []
            for h in range(nh):
                chain = []
                if far.stop > 0:
                    chain.append((_dot(k_ref[far, kcols[h]], q_aug[h]), vt_ref[vrows[h], far], far_ref[h, 0:1, :]))
                chain.append((_dot(k_ref[near, kcols[h]], q_aug[h]) + bias_ref[h, (2 - width) * t:2 * t, :],
                              vt_ref[vrows[h], near], None))
                chains.append(chain)
            o_ref[...] = jnp.concatenate(_softmax_pv(chains), axis=0).T.astype(BF16)
        return run

    lax.switch(qi, [attend(i) for i in range(nb)])


def _moba(mqt, mk, mvt, bias, far, batch, seq):
    t = ATT_TILE
    nq = seq // t
    return pl.pallas_call(
        _moba_kernel,
        grid=(batch, nq),
        in_specs=[pl.BlockSpec((256, t), lambda b, i: (0, b * nq + i)),
                  pl.BlockSpec((seq, N_HEADS * LANE), lambda b, i: (b, 0)),
                  pl.BlockSpec((256, seq), lambda b, i: (0, b)),
                  _const_spec((N_HEADS, 2 * t, t)), _const_spec((N_HEADS, 8, t))],
        out_specs=pl.BlockSpec((t, 256), lambda b, i: (b * nq + i, 0)),
        out_shape=jax.ShapeDtypeStruct((batch * seq, 256), BF16),
        scratch_shapes=[pltpu.VMEM((N_HEADS, nq, HEAD_DIM), F32)],
        compiler_params=_params(("arbitrary", "arbitrary")),
        name="moba",
    )(mqt, mk, mvt, bias, far)


def _compress_kernel(x_ref, pa_ref, pb_ref, w1a_ref, w1b_ref, w2_ref, gk_ref, o_ref, ot_ref):
    x = x_ref[0].astype(F32)
    u = _dot((x + pa_ref[...]).astype(BF16), w1a_ref[...])
    v = _dot((x + pb_ref[...]).astype(BF16), w1b_ref[...])
    n = u.shape[0]
    hid = u + pltpu.roll(v, n - 1, axis=0)
    y = _dot(jax.nn.gelu(hid, approximate=True).astype(BF16), w2_ref[...])
    is_k = lax.broadcasted_iota(jnp.int32, y.shape, 1) < HEAD_DIM
    ms = jnp.sum(jnp.where(is_k, y * y, 0.0), axis=1, keepdims=True) * (1.0 / HEAD_DIM)
    y = jnp.where(is_k, y * lax.rsqrt(ms + NORM_EPS) * gk_ref[...], y)
    o_ref[0] = y.astype(BF16)
    ot_ref[0] = y.T.astype(BF16)


def _compress(kvc3, pa, pb, w1a, w1b, w2, gk):
    batch, n, w = kvc3.shape
    return pl.pallas_call(
        _compress_kernel,
        grid=(batch,),
        in_specs=[pl.BlockSpec((1, n, w), lambda b: (b, 0, 0)), _const_spec((1, w)), _const_spec((1, w)),
                  _const_spec((w, LANE)), _const_spec((w, LANE)), _const_spec((LANE, LANE)), _const_spec((1, LANE))],
        out_specs=[pl.BlockSpec((1, n, LANE), lambda b: (b, 0, 0)), pl.BlockSpec((1, LANE, n), lambda b: (b, 0, 0))],
        out_shape=[jax.ShapeDtypeStruct((batch, n, LANE), BF16), jax.ShapeDtypeStruct((batch, LANE, n), BF16)],
        compiler_params=_params(("arbitrary",)),
        name="compress",
    )(kvc3, pa, pb, w1a, w1b, w2, gk)


def _nsa_kernel(qt_ref, kc_ref, kct_ref, kv_ref, kvt_ref, ngt_ref, bias_ref, far_ref, ovt_ref, o_ref):
    t = ATT_TILE
    nh = N_HEADS
    qi = pl.program_id(1)
    qt_all = qt_ref[...]
    qs = jnp.concatenate([qt_all[h * HEAD_DIM:(h + 1) * HEAD_DIM, :] for h in range(nh)], axis=1)

    kc = kc_ref[0][:, 0:HEAD_DIM]
    vct = kct_ref[0][HEAD_DIM:2 * HEAD_DIM, :]
    n_cmp = kc.shape[0]
    pos = qi * t + (lax.broadcasted_iota(jnp.int32, (n_cmp, nh * t), 1) & (t - 1))
    cend = lax.broadcasted_iota(jnp.int32, (n_cmp, nh * t), 0) * NSA_CMP_STRIDE + (NSA_CMP_LEN - 1)
    vis = cend <= pos
    sc = jnp.where(vis, _dot(kc, qs), NEG_INF)
    e = jnp.where(vis, jnp.exp2(sc - jnp.max(sc, axis=0, keepdims=True)), 0.0)
    den = jnp.sum(e, axis=0, keepdims=True)
    p = e / jnp.where(den > 0.0, den, 1.0)
    o_cmp = _dot(vct, p.astype(BF16))

    p_sum = p[:, 0:t] + p[:, t:2 * t] + p[:, 2 * t:3 * t] + p[:, 3 * t:4 * t]
    p_hi = p_sum.astype(BF16)
    p_lo = (p_sum - p_hi.astype(F32)).astype(BF16)
    ovt = ovt_ref[...]
    n_blk = ovt.shape[0]
    imp = _dot(ovt, p_hi) + _dot(ovt, p_lo)
    blk = lax.broadcasted_iota(jnp.int32, (n_blk, t), 0)
    cur = (qi * t + lax.broadcasted_iota(jnp.int32, (n_blk, t), 1)) // NSA_SEL_LEN
    forced = (blk == 0) | (blk == cur) | (blk == cur - 1)
    valid = blk <= cur
    imp = jnp.where(valid, imp + jnp.where(forced, NSA_FORCE_BONUS, 0.0), NEG_INF)
    keep = valid & (_rank_rows(imp, n_blk) < NSA_SEL_TOPN)
    pen = jnp.where(keep, 0.0, NEG_INF).astype(BF16)
    q_aug = jnp.concatenate([qs, jnp.concatenate([pen] * nh, axis=1),
                             jnp.zeros((LANE - HEAD_DIM - n_blk, nh * t), BF16)], axis=0)

    sel_k, sel_v = slice(0, LANE), slice(HEAD_DIM, 2 * HEAD_DIM)
    win_k, win_v = slice(LANE, LANE + HEAD_DIM), slice(LANE + HEAD_DIM, 2 * LANE)

    def attend(q_tile):
        def near(width):
            return slice((q_tile + 1 - width) * t, (q_tile + 1) * t), bias_ref[(3 - width) * t:3 * t, :]

        def run():
            keys, bias = near(min(q_tile + 1, 2))
            sel = [(_dot(kv_ref[keys, sel_k], q_aug) + bias, kvt_ref[sel_v, keys], None)]
            if keys.start > 0:
                far = slice(0, keys.start)
                sel.insert(0, (_dot(kv_ref[far, sel_k], q_aug), kvt_ref[sel_v, far], far_ref[0:1, :]))
            keys, bias = near(min(q_tile + 1, 3))
            win = [(_dot(kv_ref[keys, win_k], qs) + bias, kvt_ref[win_v, keys], None)]
            return tuple(_softmax_pv([sel, win]))
        return run

    o_slc, o_win = lax.switch(qi, [attend(i) for i in range(kv_ref.shape[0] // t)])

    outs = []
    for h in range(nh):
        cs = slice(h * t, (h + 1) * t)
        g = [ngt_ref[_MISC_NGATE + 3 * h + i:_MISC_NGATE + 3 * h + i + 1, :] for i in range(3)]
        outs.append(g[0] * o_cmp[:, cs] + g[1] * o_slc[:, cs] + g[2] * o_win[:, cs])
    o_ref[...] = jnp.concatenate(outs, axis=0).T.astype(BF16)


def _nsa(nqt, kc, kct, nkv, nkvt, ngt, bias, far, ovt, batch, seq):
    t = ATT_TILE
    nq = seq // t
    assert NSA_WINDOW == 2 * t and seq // NSA_SEL_LEN <= LANE - HEAD_DIM
    n_cmp = kc.shape[1]
    n_blk = ovt.shape[0]
    return pl.pallas_call(
        _nsa_kernel,
        grid=(batch, nq),
        in_specs=[pl.BlockSpec((256, t), lambda b, i: (0, b * nq + i)),
                  pl.BlockSpec((1, n_cmp, LANE), lambda b, i: (b, 0, 0)),
                  pl.BlockSpec((1, LANE, n_cmp), lambda b, i: (b, 0, 0)),
                  pl.BlockSpec((seq, 256), lambda b, i: (b, 0)),
                  pl.BlockSpec((256, seq), lambda b, i: (0, b)),
                  pl.BlockSpec((LANE, t), lambda b, i: (0, b * nq + i)),
                  _weight_spec((3 * t, N_HEADS * t)), _const_spec((8, N_HEADS * t)),
                  _const_spec((n_blk, n_cmp))],
        out_specs=pl.BlockSpec((t, 256), lambda b, i: (b * nq + i, 0)),
        out_shape=jax.ShapeDtypeStruct((batch * seq, 256), BF16),
        compiler_params=_params(("arbitrary", "arbitrary")),
        name="nsa",
    )(nqt, kc, kct, nkv, nkvt, ngt, bias, far, ovt)


def _gla_kernel(qk_ref, v_ref, la_ref, go_ref, tri_ref, gn_ref, o_ref, st_ref):
    c = GLA_CHUNK
    nh = N_HEADS
    n_chunk = qk_ref.shape[0] // c
    st_ref[...] = jnp.zeros_like(st_ref)
    tri = tri_ref[...]
    row = lax.broadcasted_iota(jnp.int32, (c, nh * GLA_DK), 0)
    sub_causal = (lax.broadcasted_iota(jnp.int32, (GLA_SUB, c), 0)
                  - lax.broadcasted_iota(jnp.int32, (GLA_SUB, c), 1))

    hks = [slice(h * GLA_DK, (h + 1) * GLA_DK) for h in range(nh)]
    hvs = [slice(h * GLA_DV, (h + 1) * GLA_DV) for h in range(nh)]
    group_size = GLA_UNROLL

    def group(gi, _):
        rows = [pl.ds(pl.multiple_of((gi * group_size + u) * c, c), c) for u in range(group_size)]
        units = range(group_size)

        bs = []
        for u in units:
            g = la_ref[rows[u], :]
            g1 = g.astype(BF16)
            r1 = g - g1.astype(F32)
            g2 = r1.astype(BF16)
            g3 = (r1 - g2.astype(F32)).astype(BF16)
            bs.append(_dot(tri, g1) + _dot(tri, g2) + _dot(tri, g3))

        q_inter, k_state, decay, q_sub, k_sub, vs_ = [], [], [], [], [], []
        for u in units:
            b = bs[u]
            q = qk_ref[rows[u], 0:256].astype(F32)
            k = qk_ref[rows[u], 256:512].astype(F32)
            b_last = b[c - 1:c, :]
            q_inter.append((q * jnp.exp(b)).astype(BF16))
            k_state.append((k * jnp.exp(b_last - b)).astype(BF16))
            decay.append(jnp.exp(b_last))
            qs_u, ks_u = [], []
            for i in range(c // GLA_SUB):
                lo, hi = i * GLA_SUB, (i + 1) * GLA_SUB
                ref_b = b[lo:lo + 1, :]
                ks_u.append((k * jnp.exp(jnp.where(row < hi, ref_b - b, 0.0))).astype(BF16))
                qs_u.append((q[lo:hi] * jnp.exp(b[lo:hi] - ref_b)).astype(BF16))
            q_sub.append(qs_u)
            k_sub.append(ks_u)
            vs_.append(v_ref[rows[u], :])

        o_intra, kv = [], []
        for u in units:
            a_h = []
            for h in range(nh):
                blocks = [jnp.where(sub_causal + i * GLA_SUB >= 0,
                                    _dot_nt(q_sub[u][i][:, hks[h]], k_sub[u][i][:, hks[h]]), 0.0)
                          for i in range(c // GLA_SUB)]
                a_h.append(jnp.concatenate(blocks, axis=0).astype(BF16))
            o_intra.append([_dot(a_h[h], vs_[u][:, hvs[h]]) for h in range(nh)])
            kv.append([_dot_tn(vs_[u][:, hvs[h]], k_state[u][:, hks[h]]) for h in range(nh)])

        st = [st_ref[h] for h in range(nh)]
        for u in units:
            outs = []
            for h in range(nh):
                o = o_intra[u][h] + _dot_nt(q_inter[u][:, hks[h]], st[h].astype(BF16))
                st[h] = st[h] * decay[u][:, hks[h]] + kv[u][h]
                outs.append(_rms_rows(o, gn_ref[...]))
            o_ref[rows[u], :] = (jnp.concatenate(outs, axis=1) * go_ref[rows[u], :].astype(F32)).astype(BF16)
        for h in range(nh):
            st_ref[h] = st[h]
        return 0

    lax.fori_loop(0, n_chunk // group_size, group, 0)


def _gla(gqk, gv, la, go, tri, gn, batch, seq):
    spec = lambda w: pl.BlockSpec((seq, w), lambda b: (b, 0))
    return pl.pallas_call(
        _gla_kernel,
        grid=(batch,),
        in_specs=[spec(512), spec(512), spec(256), spec(512), _const_spec((GLA_CHUNK, GLA_CHUNK)),
                  _const_spec((1, GLA_DV))],
        out_specs=spec(512),
        out_shape=jax.ShapeDtypeStruct((batch * seq, 512), BF16),
        scratch_shapes=[pltpu.VMEM((N_HEADS, GLA_DV, GLA_DK), F32)],
        compiler_params=_params(("arbitrary",)),
        name="gla",
    )(gqk, gv, la, go, tri, gn)


def _merge_kernel(x_ref, an_ref, om_ref, on_ref, og_ref, wm_ref, pm_ref, pn_ref, pg_ref, wo_ref, o_ref):
    x = x_ref[...]
    h = _rms_rows(x, an_ref[...]).astype(BF16)
    gates = [_dot(h, wm_ref[:, i * D_MODEL:(i + 1) * D_MODEL]) for i in range(3)]
    branches = [_dot(o_b[...], p_b[...]) for o_b, p_b in ((om_ref, pm_ref), (on_ref, pn_ref), (og_ref, pg_ref))]
    gated = [jax.nn.sigmoid(g) * b for g, b in zip(gates, branches)]
    z = gated[0] + gated[1] + gated[2]
    o_ref[...] = x + _dot(z.astype(BF16), wo_ref[...])


def _merge(x, an, om, on, og, wm, pm, pn, pg, wo, tm=512):
    n = x.shape[0]
    row = lambda w: pl.BlockSpec((tm, w), lambda i: (i, 0))
    return pl.pallas_call(
        _merge_kernel,
        grid=(n // tm,),
        in_specs=[row(D_MODEL), _const_spec((1, D_MODEL)), row(256), row(256), row(512),
                  _weight_spec((D_MODEL, 3 * D_MODEL)), _weight_spec((256, D_MODEL)), _weight_spec((256, D_MODEL)),
                  _weight_spec((512, D_MODEL)), _weight_spec((D_MODEL, D_MODEL))],
        out_specs=row(D_MODEL),
        out_shape=jax.ShapeDtypeStruct((n, D_MODEL), F32),
        compiler_params=_params(("arbitrary",)),
        name="merge",
    )(x, an, om, on, og, wm, pm, pn, pg, wo)


FFN_CHUNK = 256


def _ffn_kernel(x_ref, fn_ref, wa_ref, wg_ref, cw_ref, cb_ref, wd_ref, o_ref, carry_ref, act_ref, *, tiles_per_seq):
    i = pl.program_id(0)
    tm = x_ref.shape[0]
    x = x_ref[...]
    h = _rms_rows(x, fn_ref[...]).astype(BF16)
    row = lax.broadcasted_iota(jnp.int32, (tm, FFN_CHUNK), 0)

    @pl.when((i % tiles_per_seq) == 0)
    def _():
        carry_ref[...] = jnp.zeros_like(carry_ref)

    for c in range(D_FF // FFN_CHUNK):
        cs = slice(c * FFN_CHUNK, (c + 1) * FFN_CHUNK)
        a = _dot(h, wa_ref[:, cs])
        g = _dot(h, wg_ref[:, cs])
        prev = carry_ref[:, cs]
        p1 = prev[7:8, :]
        p2 = prev[6:7, :]
        a1 = jnp.where(row == 0, p1, pltpu.roll(a, 1, axis=0))
        a2 = jnp.where(row == 0, p2, jnp.where(row == 1, p1, pltpu.roll(a, 2, axis=0)))
        carry_ref[:, cs] = a[tm - 8:tm, :]
        w = cw_ref[:, cs]
        conv = w[0:1, :] * a2 + w[1:2, :] * a1 + w[2:3, :] * a + cb_ref[:, cs]
        act_ref[:, cs] = (jax.nn.gelu(conv, approximate=True) * g).astype(BF16)
    o_ref[...] = x + _dot(act_ref[...], wd_ref[...])


def _ffn(x, fn, wa, wg, cw, cb, wd, seq, tm=1024):
    n = x.shape[0]
    row = pl.BlockSpec((tm, D_MODEL), lambda i: (i, 0))
    return pl.pallas_call(
        functools.partial(_ffn_kernel, tiles_per_seq=seq // tm),
        grid=(n // tm,),
        in_specs=[row, _const_spec((1, D_MODEL)), _weight_spec((D_MODEL, D_FF)), _weight_spec((D_MODEL, D_FF)),
                  _const_spec((8, D_FF)), _const_spec((1, D_FF)), _weight_spec((D_FF, D_MODEL))],
        out_specs=row,
        out_shape=jax.ShapeDtypeStruct((n, D_MODEL), F32),
        scratch_shapes=[pltpu.VMEM((8, D_FF), F32), pltpu.VMEM((tm, D_FF), BF16)],
        compiler_params=_params(("arbitrary",)),
        name="ffn",
    )(x, fn, wa, wg, cw, cb, wd)


def _rel_bucket(dist):
    n = jnp.maximum(dist, 0)
    max_exact = REL_BUCKETS // 2
    nf = jnp.maximum(n, 1).astype(F32)
    large = max_exact + (jnp.log(nf / max_exact) / math.log(REL_MAX_DIST / max_exact)
                         * (REL_BUCKETS - max_exact)).astype(jnp.int32)
    return jnp.where(n < max_exact, n, jnp.minimum(large, REL_BUCKETS - 1))


def _bias_tiles(rel_tab):
    t = ATT_TILE
    d0 = jnp.arange(t)[None, :] - jnp.arange(t)[:, None]
    bucket = jnp.stack([_rel_bucket(d0 + k * t) for k in range(3)])
    out = jnp.zeros((rel_tab.shape[1],) + bucket.shape, F32)
    for b in range(REL_BUCKETS):
        out = jnp.where(bucket[None] == b, rel_tab[b][:, None, None, None], out)
    return out


def _block_diag_mean():
    g = jnp.arange(256) // HEAD_DIM
    return jnp.where(g[:, None] == g[None, :], 1.0 / HEAD_DIM, 0.0).astype(BF16)


def _overlap_t(n_cmp_pad, n_cmp, n_blk):
    tok = jnp.arange(n_blk * NSA_SEL_LEN)
    starts = jnp.arange(n_cmp_pad) * NSA_CMP_STRIDE
    inside = (tok[None, :] >= starts[:, None]) & (tok[None, :] < starts[:, None] + NSA_CMP_LEN)
    m = inside.reshape(n_cmp_pad, n_blk, NSA_SEL_LEN).sum(-1).astype(F32) / NSA_CMP_LEN
    m = jnp.where(jnp.arange(n_cmp_pad)[:, None] < n_cmp, m, 0.0)
    return m.T.astype(BF16)


def _tile_gain(g, reps):
    return jnp.tile(g.astype(F32), reps)[None, :]


def kernel(x, rel_bias, attn_norm, w_in, moba_q_norm, moba_k_norm, nsa_q_norm, nsa_k_norm, cmp_pos_k, cmp_pos_v,
           cmp_k_w1, cmp_k_w2, cmp_v_w1, cmp_v_w2, gla_gate_w, gla_gate_b, gla_out_norm, w_branch_moba,
           w_branch_nsa, w_branch_gla, w_out, ffn_norm, w_up, conv_w, conv_b, w_down):
    batch, seq, _ = x.shape
    depth = w_in.shape[0]
    n_cmp = seq // NSA_CMP_STRIDE - NSA_CMP_LEN // NSA_CMP_STRIDE + 1
    n_cmp_pad = seq // NSA_CMP_STRIDE
    n_blk = seq // NSA_SEL_LEN

    tiles = _bias_tiles(rel_bias.astype(F32)) * LOG2_E
    key_le_query = jnp.arange(ATT_TILE)[:, None] <= jnp.arange(ATT_TILE)[None, :]
    own = jnp.where(key_le_query, tiles[:, 0], NEG_INF)
    band = jnp.where(key_le_query, NEG_INF, tiles[:, 2])
    near, far = tiles[:, 1], tiles[:, 2, :8]
    bias_moba = jnp.concatenate([near[:N_HEADS], own[:N_HEADS]], axis=1)
    far_moba = far[:N_HEADS]
    heads_on_lanes = lambda a: jnp.concatenate(list(a[N_HEADS:]), axis=1)
    bias_nsa = jnp.concatenate([heads_on_lanes(band), heads_on_lanes(near), heads_on_lanes(own)], axis=0)
    far_nsa = heads_on_lanes(far)
    bd = _block_diag_mean()
    ovt = _overlap_t(n_cmp_pad, n_cmp, n_blk)
    tri = (jnp.arange(GLA_CHUNK)[:, None] >= jnp.arange(GLA_CHUNK)[None, :]).astype(BF16)
    ones64 = jnp.ones((HEAD_DIM,), F32)

    xf = x.reshape(batch * seq, D_MODEL)
    for l in range(depth):
        w_att, w_gla, w_go, w_glr, w_ng, wmerge = _wsplit(w_in, l)
        w_misc = jnp.pad(jnp.concatenate([w_glr, w_ng], axis=1), ((0, 0), (0, LANE - GLA_GATE_RANK - 3 * N_HEADS)))
        gw = jnp.pad(gla_gate_w[l], ((0, LANE - GLA_GATE_RANK), (0, 0))).astype(BF16)

        mqt, mk_a, mvt, nqt, kvc, nkv_a, nkvt, gqk, gv_a, la, go, ngt = _proj(
            xf, attn_norm[l][None, :], w_att, w_gla, w_go, w_misc, bd,
            _tile_gain(moba_q_norm[l], 4), _tile_gain(moba_k_norm[l], 4), _tile_gain(nsa_q_norm[l], 4),
            jnp.concatenate([nsa_k_norm[l, 1], ones64])[None, :], jnp.concatenate([nsa_k_norm[l, 2], ones64])[None, :],
            gw, gla_gate_b[l][None, :], seq)

        o_moba = _moba(mqt, mk_a, mvt, bias_moba, far_moba, batch, seq)

        half = NSA_CMP_STRIDE * HEAD_DIM
        zero = jnp.zeros((NSA_CMP_STRIDE, HEAD_DIM, HEAD_DIM), F32)

        def w1_part(part):
            wk = cmp_k_w1[l][part * half:(part + 1) * half].reshape(NSA_CMP_STRIDE, HEAD_DIM, HEAD_DIM)
            wv = cmp_v_w1[l][part * half:(part + 1) * half].reshape(NSA_CMP_STRIDE, HEAD_DIM, HEAD_DIM)
            top = jnp.concatenate([wk, zero], axis=2)
            bot = jnp.concatenate([zero, wv], axis=2)
            return jnp.concatenate([top, bot], axis=1).reshape(NSA_CMP_STRIDE * LANE, LANE).astype(BF16)

        def pos_part(part):
            pk = cmp_pos_k[l][part * NSA_CMP_STRIDE:(part + 1) * NSA_CMP_STRIDE]
            pv = cmp_pos_v[l][part * NSA_CMP_STRIDE:(part + 1) * NSA_CMP_STRIDE]
            return jnp.concatenate([pk, pv], axis=1).reshape(1, NSA_CMP_STRIDE * LANE).astype(F32)

        z64 = jnp.zeros((HEAD_DIM, HEAD_DIM), F32)
        w2 = jnp.concatenate([jnp.concatenate([cmp_k_w2[l], z64], axis=1),
                              jnp.concatenate([z64, cmp_v_w2[l]], axis=1)], axis=0).astype(BF16)
        kcv, kcvt = _compress(kvc.reshape(batch, n_cmp_pad, NSA_CMP_STRIDE * LANE), pos_part(0), pos_part(1),
                              w1_part(0), w1_part(1), w2, jnp.concatenate([nsa_k_norm[l, 0], ones64])[None, :])

        o_nsa = _nsa(nqt, kcv, kcvt, nkv_a, nkvt, ngt, bias_nsa, far_nsa, ovt, batch, seq)
        o_gla = _gla(gqk, gv_a, la, go, tri, gla_out_norm[l][None, :].astype(F32), batch, seq)

        xf = _merge(xf, attn_norm[l][None, :], o_moba, o_nsa, o_gla, wmerge.astype(BF16),
                    w_branch_moba[l].astype(BF16), w_branch_nsa[l].astype(BF16), w_branch_gla[l].astype(BF16),
                    w_out[l].astype(BF16))

        cw = jnp.pad(conv_w[l], ((0, 8 - conv_w.shape[1]), (0, 0)))
        xf = _ffn(xf, ffn_norm[l][None, :], w_up[l][:, :D_FF].astype(BF16), w_up[l][:, D_FF:].astype(BF16),
                  cw, conv_b[l][None, :], w_down[l].astype(BF16), seq)
    return xf.reshape(batch, seq, D_MODEL)
```

```python
import functools
import math

import jax
import jax.numpy as jnp
from jax import lax
from jax.experimental import pallas as pl
from jax.experimental.pallas import tpu as pltpu

F32 = jnp.float32
BF16 = jnp.bfloat16

D_MODEL = 1024
HEAD_DIM = 64
N_HEADS = 4
MOBA_BLOCK = 256
MOBA_TOPK = 3
NSA_CMP_LEN = 32
NSA_CMP_STRIDE = 16
NSA_SEL_LEN = 64
NSA_SEL_TOPN = 16
NSA_WINDOW = 512
NSA_FORCE_BONUS = 1e4
GLA_DK = 64
GLA_DV = 128
GLA_GATE_RANK = 16
GLA_GATE_NORM = 16.0
GLA_CHUNK = 64
GLA_SUB = 16
GLA_UNROLL = 8
D_FF = 2816
REL_BUCKETS = 32
REL_MAX_DIST = 128
NORM_EPS = 1e-6
NEG_INF = -1e30
LOG2_E = math.log2(math.e)

ATT_TILE = 256
LANE = 128
VMEM_LIMIT = 56 * 1024 * 1024

_C_MQ, _C_MK, _C_MV, _C_NQ = 0, 256, 512, 768
_C_KVC, _C_KSVS, _C_KWVW, _C_A_END = 1024, 1152, 1280, 1408
_MISC_NGATE = GLA_GATE_RANK


def _dot(a, b):
    return jnp.dot(a, b, preferred_element_type=F32)


def _dot_nt(a, b):
    return lax.dot_general(a, b, (((1,), (1,)), ((), ())), preferred_element_type=F32)


def _dot_tn(a, b):
    return lax.dot_general(a, b, (((0,), (0,)), ((), ())), preferred_element_type=F32)


def _rms_rows(x, gain):
    ms = jnp.mean(x * x, axis=-1, keepdims=True)
    return x * lax.rsqrt(ms + NORM_EPS) * gain


def _params(sem):
    return pltpu.CompilerParams(dimension_semantics=sem, vmem_limit_bytes=VMEM_LIMIT)


def _const_spec(shape):
    return pl.BlockSpec(shape, lambda *_: (0,) * len(shape))


def _weight_spec(shape):
    return pl.BlockSpec(shape, lambda *_: (0,) * len(shape), pipeline_mode=pl.Buffered(1))


_W_NG, _W_GLA, _W_GLR, _W_GO, _W_MERGE, _W_END = 1408, 1420, 2444, 2460, 2972, 6044


def _wsplit_kernel(w_ref, att_ref, gla_ref, go_ref, glr_ref, ng_ref, merge_ref):
    w = w_ref[0]
    att_ref[...] = w[:, 0:_W_NG].astype(BF16)
    ng_ref[...] = w[:, _W_NG:_W_GLA].astype(BF16)
    gla_ref[...] = w[:, _W_GLA:_W_GLR].astype(BF16)
    glr_ref[...] = w[:, _W_GLR:_W_GO].astype(BF16)
    go_ref[...] = w[:, _W_GO:_W_MERGE].astype(BF16)
    merge_ref[...] = w[:, _W_MERGE:_W_END].astype(BF16)


def _wsplit(w_in, layer, tr=256):
    widths = [_W_NG, _W_GLR - _W_GLA, _W_MERGE - _W_GO, _W_GO - _W_GLR, _W_GLA - _W_NG, _W_END - _W_MERGE]
    return pl.pallas_call(
        _wsplit_kernel,
        grid=(D_MODEL // tr,),
        in_specs=[pl.BlockSpec((1, tr, _W_END), lambda i: (layer, i, 0))],
        out_specs=[pl.BlockSpec((tr, w), lambda i: (i, 0)) for w in widths],
        out_shape=[jax.ShapeDtypeStruct((D_MODEL, w), BF16) for w in widths],
        compiler_params=_params(("arbitrary",)),
        name="wsplit",
    )(w_in)


def _proj_kernel(x_ref, an_ref, wa_ref, wg_ref, wo_ref, wm_ref, bd_ref, gmq_ref, gmk_ref, gnq_ref, gks_ref, gkw_ref,
                 gw_ref, gb_ref,
                 mqt_ref, mk_ref, mvt_ref, nqt_ref, kvc_ref, nkv_ref, nkvt_ref, gqk_ref, gv_ref, la_ref, go_ref,
                 ngt_ref, *, seq):
    tm = x_ref.shape[0]
    h = _rms_rows(x_ref[...], an_ref[...]).astype(BF16)
    pos = (pl.program_id(0) * tm + lax.broadcasted_iota(jnp.int32, (tm, LANE), 0)) % seq
    lane = lax.broadcasted_iota(jnp.int32, (tm, LANE), 1)

    y_all = {id(w_ref): _dot(h, w_ref[...]) for w_ref in (wa_ref, wg_ref, wo_ref, wm_ref)}

    def sec(w_ref, lo, hi):
        return y_all[id(w_ref)][:, lo:hi]

    def head_norm(y, gain):
        w = y.shape[1]
        ms = _dot((y * y).astype(BF16), bd_ref[0:w, 0:w])
        return y * lax.rsqrt(ms + NORM_EPS) * gain

    scale = HEAD_DIM ** -0.5 * LOG2_E
    mqt_ref[...] = (head_norm(sec(wa_ref, _C_MQ, _C_MK), gmq_ref[...]) * scale).T.astype(BF16)
    mk = head_norm(sec(wa_ref, _C_MK, _C_MV), gmk_ref[...])
    pos_h = (pl.program_id(0) * tm + lax.broadcasted_iota(jnp.int32, (tm, HEAD_DIM), 0)) % seq
    lane_h = lax.broadcasted_iota(jnp.int32, (tm, HEAD_DIM), 1)
    moba_onehot = jnp.where(lane_h == pos_h // MOBA_BLOCK, 1.0, 0.0)
    for hd in range(N_HEADS):
        mk_ref[:, hd * LANE:(hd + 1) * LANE] = jnp.concatenate(
            [mk[:, hd * HEAD_DIM:(hd + 1) * HEAD_DIM], moba_onehot], axis=1).astype(BF16)
    mvt_ref[...] = sec(wa_ref, _C_MV, _C_NQ).T.astype(BF16)
    nqt_ref[...] = (head_norm(sec(wa_ref, _C_NQ, _C_KVC), gnq_ref[...]) * scale).T.astype(BF16)
    kvc_ref[...] = sec(wa_ref, _C_KVC, _C_KSVS).astype(BF16)
    first_half = lane < HEAD_DIM
    ksvs = sec(wa_ref, _C_KSVS, _C_KWVW)
    ksvs = jnp.where(first_half, head_norm(ksvs, gks_ref[...]), ksvs)
    nkvt_ref[0:128, :] = ksvs.T.astype(BF16)
    sel_onehot = jnp.where(lane - HEAD_DIM == pos // NSA_SEL_LEN, 1.0, 0.0)
    nkv_ref[:, 0:128] = jnp.where(first_half, ksvs, sel_onehot).astype(BF16)
    kwvw = sec(wa_ref, _C_KWVW, _C_A_END)
    kwvw = jnp.where(first_half, head_norm(kwvw, gkw_ref[...]), kwvw)
    nkv_ref[:, 128:256] = kwvw.astype(BF16)
    nkvt_ref[128:256, :] = kwvw.T.astype(BF16)
    gqk_ref[:, 0:256] = (sec(wg_ref, 0, 256) * (GLA_DK ** -0.5)).astype(BF16)
    gqk_ref[:, 256:512] = sec(wg_ref, 256, 512).astype(BF16)
    gv_ref[...] = sec(wg_ref, 512, 1024).astype(BF16)
    go_ref[...] = jax.nn.silu(sec(wo_ref, 0, 512)).astype(BF16)
    misc = sec(wm_ref, 0, LANE)
    ngt_ref[...] = jax.nn.sigmoid(misc).T
    pre = _dot(misc.astype(BF16), gw_ref[...]) + gb_ref[...]
    la_ref[...] = (jnp.minimum(pre, 0.0) - jnp.log(1.0 + jnp.exp(-jnp.abs(pre)))) * (1.0 / GLA_GATE_NORM)


def _proj(x, an, w_att, w_gla, w_go, w_misc, bd, gmq, gmk, gnq, gks, gkw, gw, gb, seq, tm=512):
    n = x.shape[0]
    row = lambda w: pl.BlockSpec((tm, w), lambda i: (i, 0))
    col = lambda w: pl.BlockSpec((w, tm), lambda i: (0, i))
    outs = [(256, BF16, True), (N_HEADS * LANE, BF16, False), (256, BF16, True), (256, BF16, True),
            (128, BF16, False), (256, BF16, False), (256, BF16, True), (512, BF16, False), (512, BF16, False),
            (256, F32, False), (512, BF16, False), (128, F32, True)]
    return pl.pallas_call(
        functools.partial(_proj_kernel, seq=seq),
        grid=(n // tm,),
        in_specs=[row(D_MODEL), _const_spec((1, D_MODEL)), _weight_spec(w_att.shape), _weight_spec(w_gla.shape),
                  _weight_spec(w_go.shape), _weight_spec(w_misc.shape), _const_spec((256, 256)),
                  _const_spec((1, 256)), _const_spec((1, 256)), _const_spec((1, 256)), _const_spec((1, 128)),
                  _const_spec((1, 128)), _const_spec((LANE, 256)), _const_spec((1, 256))],
        out_specs=[col(w) if tr else row(w) for w, _, tr in outs],
        out_shape=[jax.ShapeDtypeStruct((w, n) if tr else (n, w), dt) for w, dt, tr in outs],
        compiler_params=_params(("arbitrary",)),
        name="proj",
    )(x, an, w_att, w_gla, w_go, w_misc, bd, gmq, gmk, gnq, gks, gkw, gw, gb)


def _softmax_pv(chains):
    def shifted(x, shift, sign):
        return x if shift is None else x + sign * shift

    maxes = [functools.reduce(jnp.maximum, [shifted(jnp.max(s, axis=0, keepdims=True), shift, 1)
                                            for s, _, shift in chain]) for chain in chains]
    probs = [[jnp.exp2(s - shifted(m, shift, -1)) for s, _, shift in chain] for chain, m in zip(chains, maxes)]
    sums = [functools.reduce(jnp.add, [jnp.sum(p, axis=0, keepdims=True) for p in ps]) for ps in probs]
    accs = [functools.reduce(jnp.add, [_dot(vt, p.astype(BF16)) for (_, vt, _), p in zip(chain, ps)])
            for chain, ps in zip(chains, probs)]
    return [acc / l for acc, l in zip(accs, sums)]


def _rank_rows(score, n):
    idx = lax.broadcasted_iota(jnp.int32, score.shape, 0)
    rank = jnp.zeros(score.shape, F32)
    for m in range(n):
        sm = score[m:m + 1, :]
        rank += jnp.where((sm > score) | ((sm == score) & (idx > m)), 1.0, 0.0)
    return rank


def _moba_kernel(qt_ref, k_ref, vt_ref, bias_ref, far_ref, o_ref, kmean_ref):
    t = ATT_TILE
    nh = N_HEADS
    nb = k_ref.shape[0] // t
    kcols = [slice(h * LANE, (h + 1) * LANE) for h in range(nh)]
    vrows = [slice(h * HEAD_DIM, (h + 1) * HEAD_DIM) for h in range(nh)]
    pen_rows = 16

    def step(q_tile):
        width = min(q_tile + 1, 2)
        near = slice((q_tile + 1 - width) * t, (q_tile + 1) * t)
        far = slice(0, (q_tile + 1 - width) * t)

        def run():
            if q_tile == 0:
                for h in range(nh):
                    kh = k_ref[:, h * LANE:h * LANE + HEAD_DIM].astype(F32)
                    kmean_ref[h] = jnp.mean(kh.reshape(nb, t, HEAD_DIM), axis=1)

            chains = []
            for h in range(nh):
                qt = qt_ref[vrows[h], :]
                if q_tile > MOBA_TOPK:
                    past = lax.broadcasted_iota(jnp.int32, (nb, t), 0) < q_tile
                    gate = jnp.where(past, _dot(kmean_ref[h].astype(BF16), qt), NEG_INF)
                    pen = jnp.where(past & (_rank_rows(gate, q_tile) >= MOBA_TOPK), NEG_INF, 0.0)
                    pen = jnp.concatenate([pen, jnp.zeros((pen_rows - nb, t), F32)], axis=0).astype(BF16)
                else:
                    pen = jnp.zeros((pen_rows, t), BF16)
                q_aug = jnp.concatenate([qt, pen, jnp.zeros((LANE - HEAD_DIM - pen_rows, t), BF16)], axis=0)
                chain = []
                if far.stop > 0:
                    chain.append((_dot(k_ref[far, kcols[h]], q_aug), vt_ref[vrows[h], far], far_ref[h, 0:1, :]))
                chain.append((_dot(k_ref[near, kcols[h]], q_aug) + bias_ref[h, (2 - width) * t:2 * t, :],
                              vt_ref[vrows[h], near], None))
                chains.append(chain)
            o_ref[...] = jnp.concatenate(_softmax_pv(chains), axis=0).T.astype(BF16)
        return run

    lax.switch(pl.program_id(1), [step(i) for i in range(nb)])


def _moba(mqt, mk, mvt, bias, far, batch, seq):
    t = ATT_TILE
    nq = seq // t
    return pl.pallas_call(
        _moba_kernel,
        grid=(batch, nq),
        in_specs=[pl.BlockSpec((256, t), lambda b, i: (0, b * nq + i)),
                  pl.BlockSpec((seq, N_HEADS * LANE), lambda b, i: (b, 0)),
                  pl.BlockSpec((256, seq), lambda b, i: (0, b)),
                  _const_spec((N_HEADS, 2 * t, t)), _const_spec((N_HEADS, 8, t))],
        out_specs=pl.BlockSpec((t, 256), lambda b, i: (b * nq + i, 0)),
        out_shape=jax.ShapeDtypeStruct((batch * seq, 256), BF16),
        scratch_shapes=[pltpu.VMEM((N_HEADS, nq, HEAD_DIM), F32)],
        compiler_params=_params(("arbitrary", "arbitrary")),
        name="moba",
    )(mqt, mk, mvt, bias, far)


def _compress_kernel(x_ref, pa_ref, pb_ref, w1a_ref, w1b_ref, w2_ref, gk_ref, o_ref, ot_ref):
    x = x_ref[0].astype(F32)
    u = _dot((x + pa_ref[...]).astype(BF16), w1a_ref[...])
    v = _dot((x + pb_ref[...]).astype(BF16), w1b_ref[...])
    n = u.shape[0]
    hid = u + pltpu.roll(v, n - 1, axis=0)
    y = _dot(jax.nn.gelu(hid, approximate=True).astype(BF16), w2_ref[...])
    is_k = lax.broadcasted_iota(jnp.int32, y.shape, 1) < HEAD_DIM
    ms = jnp.sum(jnp.where(is_k, y * y, 0.0), axis=1, keepdims=True) * (1.0 / HEAD_DIM)
    y = jnp.where(is_k, y * lax.rsqrt(ms + NORM_EPS) * gk_ref[...], y)
    o_ref[0] = y.astype(BF16)
    ot_ref[0] = y.T.astype(BF16)


def _compress(kvc3, pa, pb, w1a, w1b, w2, gk):
    batch, n, w = kvc3.shape
    return pl.pallas_call(
        _compress_kernel,
        grid=(batch,),
        in_specs=[pl.BlockSpec((1, n, w), lambda b: (b, 0, 0)), _const_spec((1, w)), _const_spec((1, w)),
                  _const_spec((w, LANE)), _const_spec((w, LANE)), _const_spec((LANE, LANE)), _const_spec((1, LANE))],
        out_specs=[pl.BlockSpec((1, n, LANE), lambda b: (b, 0, 0)), pl.BlockSpec((1, LANE, n), lambda b: (b, 0, 0))],
        out_shape=[jax.ShapeDtypeStruct((batch, n, LANE), BF16), jax.ShapeDtypeStruct((batch, LANE, n), BF16)],
        compiler_params=_params(("arbitrary",)),
        name="compress",
    )(kvc3, pa, pb, w1a, w1b, w2, gk)


def _nsa_kernel(qt_ref, kc_ref, kct_ref, kv_ref, kvt_ref, ngt_ref, bias_ref, far_ref, ovt_ref, o_ref):
    t = ATT_TILE
    nh = N_HEADS
    sel_k, sel_v = slice(0, LANE), slice(HEAD_DIM, 2 * HEAD_DIM)
    win_k, win_v = slice(LANE, LANE + HEAD_DIM), slice(LANE + HEAD_DIM, 2 * LANE)

    def step(q_tile):
        def near(width):
            return slice((q_tile + 1 - width) * t, (q_tile + 1) * t), bias_ref[(3 - width) * t:3 * t, :]

        def run():
            qt_all = qt_ref[...]
            qs = jnp.concatenate([qt_all[h * HEAD_DIM:(h + 1) * HEAD_DIM, :] for h in range(nh)], axis=1)

            keys, bias = near(min(q_tile + 1, NSA_WINDOW // t + 1))
            win = [(_dot(kv_ref[keys, win_k], qs) + bias, kvt_ref[win_v, keys], None)]

            kc = kc_ref[0][:, 0:HEAD_DIM]
            vct = kct_ref[0][HEAD_DIM:2 * HEAD_DIM, :]
            n_cmp = kc.shape[0]
            pos = q_tile * t + (lax.broadcasted_iota(jnp.int32, (n_cmp, nh * t), 1) & (t - 1))
            cend = lax.broadcasted_iota(jnp.int32, (n_cmp, nh * t), 0) * NSA_CMP_STRIDE + (NSA_CMP_LEN - 1)
            vis = cend <= pos
            sc = jnp.where(vis, _dot(kc, qs), NEG_INF)
            e = jnp.where(vis, jnp.exp2(sc - jnp.max(sc, axis=0, keepdims=True)), 0.0)
            den = jnp.sum(e, axis=0, keepdims=True)
            p = e / jnp.where(den > 0.0, den, 1.0)
            o_cmp = _dot(vct, p.astype(BF16))

            ovt = ovt_ref[...]
            n_blk = ovt.shape[0]
            blk = lax.broadcasted_iota(jnp.int32, (n_blk, t), 0)
            cur = (q_tile * t + lax.broadcasted_iota(jnp.int32, (n_blk, t), 1)) // NSA_SEL_LEN
            keep = blk <= cur
            max_visible = min(n_blk, (q_tile + 1) * t // NSA_SEL_LEN)
            if max_visible > NSA_SEL_TOPN:
                p_sum = p[:, 0:t] + p[:, t:2 * t] + p[:, 2 * t:3 * t] + p[:, 3 * t:4 * t]
                p_hi = p_sum.astype(BF16)
                p_lo = (p_sum - p_hi.astype(F32)).astype(BF16)
                imp = _dot(ovt, p_hi) + _dot(ovt, p_lo)
                forced = (blk == 0) | (blk == cur) | (blk == cur - 1)
                imp = jnp.where(keep, imp + jnp.where(forced, NSA_FORCE_BONUS, 0.0), NEG_INF)
                keep = keep & (_rank_rows(imp, max_visible) < NSA_SEL_TOPN)
            pen = jnp.where(keep, 0.0, NEG_INF).astype(BF16)
            q_aug = jnp.concatenate([qs, jnp.concatenate([pen] * nh, axis=1),
                                     jnp.zeros((LANE - HEAD_DIM - n_blk, nh * t), BF16)], axis=0)

            keys, bias = near(min(q_tile + 1, 2))
            sel = [(_dot(kv_ref[keys, sel_k], q_aug) + bias, kvt_ref[sel_v, keys], None)]
            if keys.start > 0:
                far = slice(0, keys.start)
                sel.insert(0, (_dot(kv_ref[far, sel_k], q_aug), kvt_ref[sel_v, far], far_ref[0:1, :]))
            o_slc, o_win = _softmax_pv([sel, win])

            outs = []
            for h in range(nh):
                cs = slice(h * t, (h + 1) * t)
                g = [ngt_ref[_MISC_NGATE + 3 * h + i:_MISC_NGATE + 3 * h + i + 1, :] for i in range(3)]
                outs.append(g[0] * o_cmp[:, cs] + g[1] * o_slc[:, cs] + g[2] * o_win[:, cs])
            o_ref[...] = jnp.concatenate(outs, axis=0).T.astype(BF16)
        return run

    lax.switch(pl.program_id(1), [step(i) for i in range(kv_ref.shape[0] // t)])


def _nsa(nqt, kc, kct, nkv, nkvt, ngt, bias, far, ovt, batch, seq):
    t = ATT_TILE
    nq = seq // t
    assert NSA_WINDOW == 2 * t and seq // NSA_SEL_LEN <= LANE - HEAD_DIM
    n_cmp = kc.shape[1]
    n_blk = ovt.shape[0]
    return pl.pallas_call(
        _nsa_kernel,
        grid=(batch, nq),
        in_specs=[pl.BlockSpec((256, t), lambda b, i: (0, b * nq + i)),
                  pl.BlockSpec((1, n_cmp, LANE), lambda b, i: (b, 0, 0)),
                  pl.BlockSpec((1, LANE, n_cmp), lambda b, i: (b, 0, 0)),
                  pl.BlockSpec((seq, 256), lambda b, i: (b, 0)),
                  pl.BlockSpec((256, seq), lambda b, i: (0, b)),
                  pl.BlockSpec((LANE, t), lambda b, i: (0, b * nq + i)),
                  _weight_spec((3 * t, N_HEADS * t)), _const_spec((8, N_HEADS * t)),
                  _const_spec((n_blk, n_cmp))],
        out_specs=pl.BlockSpec((t, 256), lambda b, i: (b * nq + i, 0)),
        out_shape=jax.ShapeDtypeStruct((batch * seq, 256), BF16),
        compiler_params=_params(("arbitrary", "arbitrary")),
        name="nsa",
    )(nqt, kc, kct, nkv, nkvt, ngt, bias, far, ovt)


def _gla_kernel(qk_ref, v_ref, la_ref, go_ref, tri_ref, gn_ref, o_ref, st_ref):
    c = GLA_CHUNK
    nh = N_HEADS
    n_chunk = qk_ref.shape[0] // c
    st_ref[...] = jnp.zeros_like(st_ref)
    tri = tri_ref[...]
    row = lax.broadcasted_iota(jnp.int32, (c, nh * GLA_DK), 0)
    sub_causal = (lax.broadcasted_iota(jnp.int32, (GLA_SUB, c), 0)
                  - lax.broadcasted_iota(jnp.int32, (GLA_SUB, c), 1))

    hks = [slice(h * GLA_DK, (h + 1) * GLA_DK) for h in range(nh)]
    hvs = [slice(h * GLA_DV, (h + 1) * GLA_DV) for h in range(nh)]
    group_size = GLA_UNROLL

    def group(gi, _):
        rows = [pl.ds(pl.multiple_of((gi * group_size + u) * c, c), c) for u in range(group_size)]
        units = range(group_size)

        bs = []
        for u in units:
            g = la_ref[rows[u], :]
            g1 = g.astype(BF16)
            r1 = g - g1.astype(F32)
            g2 = r1.astype(BF16)
            g3 = (r1 - g2.astype(F32)).astype(BF16)
            bs.append(_dot(tri, g1) + _dot(tri, g2) + _dot(tri, g3))

        q_inter, k_state, decay, q_sub, k_sub, vs_ = [], [], [], [], [], []
        for u in units:
            b = bs[u]
            q = qk_ref[rows[u], 0:256].astype(F32)
            k = qk_ref[rows[u], 256:512].astype(F32)
            b_last = b[c - 1:c, :]
            q_inter.append((q * jnp.exp(b)).astype(BF16))
            k_state.append((k * jnp.exp(b_last - b)).astype(BF16))
            decay.append(jnp.exp(b_last))
            qs_u, ks_u = [], []
            for i in range(c // GLA_SUB):
                lo, hi = i * GLA_SUB, (i + 1) * GLA_SUB
                ref_b = b[lo:lo + 1, :]
                ks_u.append((k * jnp.exp(jnp.where(row < hi, ref_b - b, 0.0))).astype(BF16))
                qs_u.append((q[lo:hi] * jnp.exp(b[lo:hi] - ref_b)).astype(BF16))
            q_sub.append(qs_u)
            k_sub.append(ks_u)
            vs_.append(v_ref[rows[u], :])

        o_intra, kv = [], []
        for u in units:
            a_h = []
            for h in range(nh):
                blocks = [jnp.where(sub_causal + i * GLA_SUB >= 0,
                                    _dot_nt(q_sub[u][i][:, hks[h]], k_sub[u][i][:, hks[h]]), 0.0)
                          for i in range(c // GLA_SUB)]
                a_h.append(jnp.concatenate(blocks, axis=0).astype(BF16))
            o_intra.append([_dot(a_h[h], vs_[u][:, hvs[h]]) for h in range(nh)])
            kv.append([_dot_tn(vs_[u][:, hvs[h]], k_state[u][:, hks[h]]) for h in range(nh)])

        st = [st_ref[h] for h in range(nh)]
        for u in units:
            outs = []
            for h in range(nh):
                o = o_intra[u][h] + _dot_nt(q_inter[u][:, hks[h]], st[h].astype(BF16))
                st[h] = st[h] * decay[u][:, hks[h]] + kv[u][h]
                outs.append(_rms_rows(o, gn_ref[...]))
            o_ref[rows[u], :] = (jnp.concatenate(outs, axis=1) * go_ref[rows[u], :].astype(F32)).astype(BF16)
        for h in range(nh):
            st_ref[h] = st[h]
        return 0

    lax.fori_loop(0, n_chunk // group_size, group, 0)


def _gla(gqk, gv, la, go, tri, gn, batch, seq):
    spec = lambda w: pl.BlockSpec((seq, w), lambda b: (b, 0))
    return pl.pallas_call(
        _gla_kernel,
        grid=(batch,),
        in_specs=[spec(512), spec(512), spec(256), spec(512), _const_spec((GLA_CHUNK, GLA_CHUNK)),
                  _const_spec((1, GLA_DV))],
        out_specs=spec(512),
        out_shape=jax.ShapeDtypeStruct((batch * seq, 512), BF16),
        scratch_shapes=[pltpu.VMEM((N_HEADS, GLA_DV, GLA_DK), F32)],
        compiler_params=_params(("arbitrary",)),
        name="gla",
    )(gqk, gv, la, go, tri, gn)


def _merge_kernel(x_ref, an_ref, om_ref, on_ref, og_ref, wm_ref, pm_ref, pn_ref, pg_ref, wo_ref, o_ref):
    x = x_ref[...]
    h = _rms_rows(x, an_ref[...]).astype(BF16)
    gates = [_dot(h, wm_ref[:, i * D_MODEL:(i + 1) * D_MODEL]) for i in range(3)]
    branches = [_dot(o_b[...], p_b[...]) for o_b, p_b in ((om_ref, pm_ref), (on_ref, pn_ref), (og_ref, pg_ref))]
    gated = [jax.nn.sigmoid(g) * b for g, b in zip(gates, branches)]
    z = gated[0] + gated[1] + gated[2]
    o_ref[...] = x + _dot(z.astype(BF16), wo_ref[...])


def _merge(x, an, om, on, og, wm, pm, pn, pg, wo, tm=512):
    n = x.shape[0]
    row = lambda w: pl.BlockSpec((tm, w), lambda i: (i, 0))
    return pl.pallas_call(
        _merge_kernel,
        grid=(n // tm,),
        in_specs=[row(D_MODEL), _const_spec((1, D_MODEL)), row(256), row(256), row(512),
                  _weight_spec((D_MODEL, 3 * D_MODEL)), _weight_spec((256, D_MODEL)), _weight_spec((256, D_MODEL)),
                  _weight_spec((512, D_MODEL)), _weight_spec((D_MODEL, D_MODEL))],
        out_specs=row(D_MODEL),
        out_shape=jax.ShapeDtypeStruct((n, D_MODEL), F32),
        compiler_params=_params(("arbitrary",)),
        name="merge",
    )(x, an, om, on, og, wm, pm, pn, pg, wo)


FFN_CHUNK = 256


def _ffn_kernel(x_ref, fn_ref, wa_ref, wg_ref, cw_ref, cb_ref, wd_ref, o_ref, carry_ref, act_ref, *, tiles_per_seq):
    i = pl.program_id(0)
    tm = x_ref.shape[0]
    x = x_ref[...]
    h = _rms_rows(x, fn_ref[...]).astype(BF16)
    row = lax.broadcasted_iota(jnp.int32, (tm, FFN_CHUNK), 0)

    @pl.when((i % tiles_per_seq) == 0)
    def _():
        carry_ref[...] = jnp.zeros_like(carry_ref)

    for c in range(D_FF // FFN_CHUNK):
        cs = slice(c * FFN_CHUNK, (c + 1) * FFN_CHUNK)
        a = _dot(h, wa_ref[:, cs])
        g = _dot(h, wg_ref[:, cs])
        prev = carry_ref[:, cs]
        p1 = prev[7:8, :]
        p2 = prev[6:7, :]
        a1 = jnp.where(row == 0, p1, pltpu.roll(a, 1, axis=0))
        a2 = jnp.where(row == 0, p2, jnp.where(row == 1, p1, pltpu.roll(a, 2, axis=0)))
        carry_ref[:, cs] = a[tm - 8:tm, :]
        w = cw_ref[:, cs]
        conv = w[0:1, :] * a2 + w[1:2, :] * a1 + w[2:3, :] * a + cb_ref[:, cs]
        act_ref[:, cs] = (jax.nn.gelu(conv, approximate=True) * g).astype(BF16)
    o_ref[...] = x + _dot(act_ref[...], wd_ref[...])


def _ffn(x, fn, wa, wg, cw, cb, wd, seq, tm=1024):
    n = x.shape[0]
    row = pl.BlockSpec((tm, D_MODEL), lambda i: (i, 0))
    return pl.pallas_call(
        functools.partial(_ffn_kernel, tiles_per_seq=seq // tm),
        grid=(n // tm,),
        in_specs=[row, _const_spec((1, D_MODEL)), _weight_spec((D_MODEL, D_FF)), _weight_spec((D_MODEL, D_FF)),
                  _const_spec((8, D_FF)), _const_spec((1, D_FF)), _weight_spec((D_FF, D_MODEL))],
        out_specs=row,
        out_shape=jax.ShapeDtypeStruct((n, D_MODEL), F32),
        scratch_shapes=[pltpu.VMEM((8, D_FF), F32), pltpu.VMEM((tm, D_FF), BF16)],
        compiler_params=_params(("arbitrary",)),
        name="ffn",
    )(x, fn, wa, wg, cw, cb, wd)


def _rel_bucket(dist):
    n = jnp.maximum(dist, 0)
    max_exact = REL_BUCKETS // 2
    nf = jnp.maximum(n, 1).astype(F32)
    large = max_exact + (jnp.log(nf / max_exact) / math.log(REL_MAX_DIST / max_exact)
                         * (REL_BUCKETS - max_exact)).astype(jnp.int32)
    return jnp.where(n < max_exact, n, jnp.minimum(large, REL_BUCKETS - 1))


def _bias_tiles(rel_tab):
    t = ATT_TILE
    d0 = jnp.arange(t)[None, :] - jnp.arange(t)[:, None]
    bucket = jnp.stack([_rel_bucket(d0 + k * t) for k in range(3)])
    out = jnp.zeros((rel_tab.shape[1],) + bucket.shape, F32)
    for b in range(REL_BUCKETS):
        out = jnp.where(bucket[None] == b, rel_tab[b][:, None, None, None], out)
    return out


def _block_diag_mean():
    g = jnp.arange(256) // HEAD_DIM
    return jnp.where(g[:, None] == g[None, :], 1.0 / HEAD_DIM, 0.0).astype(BF16)


def _overlap_t(n_cmp_pad, n_cmp, n_blk):
    tok = jnp.arange(n_blk * NSA_SEL_LEN)
    starts = jnp.arange(n_cmp_pad) * NSA_CMP_STRIDE
    inside = (tok[None, :] >= starts[:, None]) & (tok[None, :] < starts[:, None] + NSA_CMP_LEN)
    m = inside.reshape(n_cmp_pad, n_blk, NSA_SEL_LEN).sum(-1).astype(F32) / NSA_CMP_LEN
    m = jnp.where(jnp.arange(n_cmp_pad)[:, None] < n_cmp, m, 0.0)
    return m.T.astype(BF16)


def _tile_gain(g, reps):
    return jnp.tile(g.astype(F32), reps)[None, :]


def kernel(x, rel_bias, attn_norm, w_in, moba_q_norm, moba_k_norm, nsa_q_norm, nsa_k_norm, cmp_pos_k, cmp_pos_v,
           cmp_k_w1, cmp_k_w2, cmp_v_w1, cmp_v_w2, gla_gate_w, gla_gate_b, gla_out_norm, w_branch_moba,
           w_branch_nsa, w_branch_gla, w_out, ffn_norm, w_up, conv_w, conv_b, w_down):
    batch, seq, _ = x.shape
    depth = w_in.shape[0]
    n_cmp = seq // NSA_CMP_STRIDE - NSA_CMP_LEN // NSA_CMP_STRIDE + 1
    n_cmp_pad = seq // NSA_CMP_STRIDE
    n_blk = seq // NSA_SEL_LEN

    tiles = _bias_tiles(rel_bias.astype(F32)) * LOG2_E
    key_le_query = jnp.arange(ATT_TILE)[:, None] <= jnp.arange(ATT_TILE)[None, :]
    own = jnp.where(key_le_query, tiles[:, 0], NEG_INF)
    band = jnp.where(key_le_query, NEG_INF, tiles[:, 2])
    near, far = tiles[:, 1], tiles[:, 2, :8]
    bias_moba = jnp.concatenate([near[:N_HEADS], own[:N_HEADS]], axis=1)
    far_moba = far[:N_HEADS]
    heads_on_lanes = lambda a: jnp.concatenate(list(a[N_HEADS:]), axis=1)
    bias_nsa = jnp.concatenate([heads_on_lanes(band), heads_on_lanes(near), heads_on_lanes(own)], axis=0)
    far_nsa = heads_on_lanes(far)
    bd = _block_diag_mean()
    ovt = _overlap_t(n_cmp_pad, n_cmp, n_blk)
    tri = (jnp.arange(GLA_CHUNK)[:, None] >= jnp.arange(GLA_CHUNK)[None, :]).astype(BF16)
    ones64 = jnp.ones((HEAD_DIM,), F32)

    xf = x.reshape(batch * seq, D_MODEL)
    for l in range(depth):
        w_att, w_gla, w_go, w_glr, w_ng, wmerge = _wsplit(w_in, l)
        w_misc = jnp.pad(jnp.concatenate([w_glr, w_ng], axis=1), ((0, 0), (0, LANE - GLA_GATE_RANK - 3 * N_HEADS)))
        gw = jnp.pad(gla_gate_w[l], ((0, LANE - GLA_GATE_RANK), (0, 0))).astype(BF16)

        mqt, mk_a, mvt, nqt, kvc, nkv_a, nkvt, gqk, gv_a, la, go, ngt = _proj(
            xf, attn_norm[l][None, :], w_att, w_gla, w_go, w_misc, bd,
            _tile_gain(moba_q_norm[l], 4), _tile_gain(moba_k_norm[l], 4), _tile_gain(nsa_q_norm[l], 4),
            jnp.concatenate([nsa_k_norm[l, 1], ones64])[None, :], jnp.concatenate([nsa_k_norm[l, 2], ones64])[None, :],
            gw, gla_gate_b[l][None, :], seq)

        o_moba = _moba(mqt, mk_a, mvt, bias_moba, far_moba, batch, seq)

        half = NSA_CMP_STRIDE * HEAD_DIM
        zero = jnp.zeros((NSA_CMP_STRIDE, HEAD_DIM, HEAD_DIM), F32)

        def w1_part(part):
            wk = cmp_k_w1[l][part * half:(part + 1) * half].reshape(NSA_CMP_STRIDE, HEAD_DIM, HEAD_DIM)
            wv = cmp_v_w1[l][part * half:(part + 1) * half].reshape(NSA_CMP_STRIDE, HEAD_DIM, HEAD_DIM)
            top = jnp.concatenate([wk, zero], axis=2)
            bot = jnp.concatenate([zero, wv], axis=2)
            return jnp.concatenate([top, bot], axis=1).reshape(NSA_CMP_STRIDE * LANE, LANE).astype(BF16)

        def pos_part(part):
            pk = cmp_pos_k[l][part * NSA_CMP_STRIDE:(part + 1) * NSA_CMP_STRIDE]
            pv = cmp_pos_v[l][part * NSA_CMP_STRIDE:(part + 1) * NSA_CMP_STRIDE]
            return jnp.concatenate([pk, pv], axis=1).reshape(1, NSA_CMP_STRIDE * LANE).astype(F32)

        z64 = jnp.zeros((HEAD_DIM, HEAD_DIM), F32)
        w2 = jnp.concatenate([jnp.concatenate([cmp_k_w2[l], z64], axis=1),
                              jnp.concatenate([z64, cmp_v_w2[l]], axis=1)], axis=0).astype(BF16)
        kcv, kcvt = _compress(kvc.reshape(batch, n_cmp_pad, NSA_CMP_STRIDE * LANE), pos_part(0), pos_part(1),
                              w1_part(0), w1_part(1), w2, jnp.concatenate([nsa_k_norm[l, 0], ones64])[None, :])

        o_nsa = _nsa(nqt, kcv, kcvt, nkv_a, nkvt, ngt, bias_nsa, far_nsa, ovt, batch, seq)
        o_gla = _gla(gqk, gv_a, la, go, tri, gla_out_norm[l][None, :].astype(F32), batch, seq)

        xf = _merge(xf, attn_norm[l][None, :], o_moba, o_nsa, o_gla, wmerge.astype(BF16),
                    w_branch_moba[l].astype(BF16), w_branch_nsa[l].astype(BF16), w_branch_gla[l].astype(BF16),
                    w_out[l].astype(BF16))

        cw = jnp.pad(conv_w[l], ((0, 8 - conv_w.shape[1]), (0, 0)))
        xf = _ffn(xf, ffn_norm[l][None, :], w_up[l][:, :D_FF].astype(BF16), w_up[l][:, D_FF:].astype(BF16),
                  cw, conv_b[l][None, :], w_down[l].astype(BF16), seq)
    return xf.reshape(batch, seq, D_MODEL)
```

```python
import functools
import math

import jax
import jax.numpy as jnp
from jax import lax
from jax.experimental import pallas as pl
from jax.experimental.pallas import tpu as pltpu

F32 = jnp.float32
BF16 = jnp.bfloat16

D_MODEL = 1024
HEAD_DIM = 64
N_HEADS = 4
MOBA_BLOCK = 256
MOBA_TOPK = 3
NSA_CMP_LEN = 32
NSA_CMP_STRIDE = 16
NSA_SEL_LEN = 64
NSA_SEL_TOPN = 16
NSA_WINDOW = 512
NSA_FORCE_BONUS = 1e4
GLA_DK = 64
GLA_DV = 128
GLA_GATE_RANK = 16
GLA_GATE_NORM = 16.0
GLA_CHUNK = 64
GLA_SUB = 16
GLA_UNROLL = 8
D_FF = 2816
REL_BUCKETS = 32
REL_MAX_DIST = 128
NORM_EPS = 1e-6
NEG_INF = -1e30
LOG2_E = math.log2(math.e)

ATT_TILE = 256
LANE = 128
VMEM_LIMIT = 56 * 1024 * 1024

_C_MQ, _C_MK, _C_MV, _C_NQ = 0, 256, 512, 768
_C_KVC, _C_KSVS, _C_KWVW, _C_A_END = 1024, 1152, 1280, 1408
_MISC_NGATE = GLA_GATE_RANK


def _dot(a, b):
    return jnp.dot(a, b, preferred_element_type=F32)


def _dot_nt(a, b):
    return lax.dot_general(a, b, (((1,), (1,)), ((), ())), preferred_element_type=F32)


def _dot_tn(a, b):
    return lax.dot_general(a, b, (((0,), (0,)), ((), ())), preferred_element_type=F32)


def _rms_rows(x, gain):
    ms = jnp.mean(x * x, axis=-1, keepdims=True)
    return x * lax.rsqrt(ms + NORM_EPS) * gain


def _params(sem):
    return pltpu.CompilerParams(dimension_semantics=sem, vmem_limit_bytes=VMEM_LIMIT)


def _const_spec(shape):
    return pl.BlockSpec(shape, lambda *_: (0,) * len(shape))


def _weight_spec(shape):
    return pl.BlockSpec(shape, lambda *_: (0,) * len(shape), pipeline_mode=pl.Buffered(1))


_W_NG, _W_GLA, _W_GLR, _W_GO, _W_MERGE, _W_END = 1408, 1420, 2444, 2460, 2972, 6044


def _wsplit_kernel(w_ref, att_ref, gla_ref, go_ref, glr_ref, ng_ref, merge_ref):
    w = w_ref[0]
    att_ref[...] = w[:, 0:_W_NG].astype(BF16)
    ng_ref[...] = w[:, _W_NG:_W_GLA].astype(BF16)
    gla_ref[...] = w[:, _W_GLA:_W_GLR].astype(BF16)
    glr_ref[...] = w[:, _W_GLR:_W_GO].astype(BF16)
    go_ref[...] = w[:, _W_GO:_W_MERGE].astype(BF16)
    merge_ref[...] = w[:, _W_MERGE:_W_END].astype(BF16)


def _wsplit(w_in, layer, tr=256):
    widths = [_W_NG, _W_GLR - _W_GLA, _W_MERGE - _W_GO, _W_GO - _W_GLR, _W_GLA - _W_NG, _W_END - _W_MERGE]
    return pl.pallas_call(
        _wsplit_kernel,
        grid=(D_MODEL // tr,),
        in_specs=[pl.BlockSpec((1, tr, _W_END), lambda i: (layer, i, 0))],
        out_specs=[pl.BlockSpec((tr, w), lambda i: (i, 0)) for w in widths],
        out_shape=[jax.ShapeDtypeStruct((D_MODEL, w), BF16) for w in widths],
        compiler_params=_params(("arbitrary",)),
        name="wsplit",
    )(w_in)


def _proj_kernel(x_ref, an_ref, wa_ref, wg_ref, wo_ref, wm_ref, bd_ref, gmq_ref, gmk_ref, gnq_ref, gks_ref, gkw_ref,
                 gw_ref, gb_ref,
                 mqt_ref, mk_ref, mvt_ref, nqt_ref, kvc_ref, nkv_ref, nkvt_ref, gqk_ref, gv_ref, la_ref, go_ref,
                 ngt_ref, *, seq):
    tm = x_ref.shape[0]
    h = _rms_rows(x_ref[...], an_ref[...]).astype(BF16)
    pos = (pl.program_id(0) * tm + lax.broadcasted_iota(jnp.int32, (tm, LANE), 0)) % seq
    lane = lax.broadcasted_iota(jnp.int32, (tm, LANE), 1)

    y_all = {id(w_ref): _dot(h, w_ref[...]) for w_ref in (wa_ref, wg_ref, wo_ref, wm_ref)}

    def sec(w_ref, lo, hi):
        return y_all[id(w_ref)][:, lo:hi]

    def head_norm(y, gain):
        w = y.shape[1]
        ms = _dot((y * y).astype(BF16), bd_ref[0:w, 0:w])
        return y * lax.rsqrt(ms + NORM_EPS) * gain

    scale = HEAD_DIM ** -0.5 * LOG2_E
    mqt_ref[...] = (head_norm(sec(wa_ref, _C_MQ, _C_MK), gmq_ref[...]) * scale).T.astype(BF16)
    mk = head_norm(sec(wa_ref, _C_MK, _C_MV), gmk_ref[...])
    pos_h = (pl.program_id(0) * tm + lax.broadcasted_iota(jnp.int32, (tm, HEAD_DIM), 0)) % seq
    lane_h = lax.broadcasted_iota(jnp.int32, (tm, HEAD_DIM), 1)
    moba_onehot = jnp.where(lane_h == pos_h // MOBA_BLOCK, 1.0, 0.0)
    for hd in range(N_HEADS):
        mk_ref[:, hd * LANE:(hd + 1) * LANE] = jnp.concatenate(
            [mk[:, hd * HEAD_DIM:(hd + 1) * HEAD_DIM], moba_onehot], axis=1).astype(BF16)
    mvt_ref[...] = sec(wa_ref, _C_MV, _C_NQ).T.astype(BF16)
    nqt_ref[...] = (head_norm(sec(wa_ref, _C_NQ, _C_KVC), gnq_ref[...]) * scale).T.astype(BF16)
    kvc_ref[...] = sec(wa_ref, _C_KVC, _C_KSVS).astype(BF16)
    first_half = lane < HEAD_DIM
    ksvs = sec(wa_ref, _C_KSVS, _C_KWVW)
    ksvs = jnp.where(first_half, head_norm(ksvs, gks_ref[...]), ksvs)
    nkvt_ref[0:128, :] = ksvs.T.astype(BF16)
    sel_onehot = jnp.where(lane - HEAD_DIM == pos // NSA_SEL_LEN, 1.0, 0.0)
    nkv_ref[:, 0:128] = jnp.where(first_half, ksvs, sel_onehot).astype(BF16)
    kwvw = sec(wa_ref, _C_KWVW, _C_A_END)
    kwvw = jnp.where(first_half, head_norm(kwvw, gkw_ref[...]), kwvw)
    nkv_ref[:, 128:256] = kwvw.astype(BF16)
    nkvt_ref[128:256, :] = kwvw.T.astype(BF16)
    gqk_ref[:, 0:256] = (sec(wg_ref, 0, 256) * (GLA_DK ** -0.5)).astype(BF16)
    gqk_ref[:, 256:512] = sec(wg_ref, 256, 512).astype(BF16)
    gv_ref[...] = sec(wg_ref, 512, 1024).astype(BF16)
    go_ref[...] = jax.nn.silu(sec(wo_ref, 0, 512)).astype(BF16)
    misc = sec(wm_ref, 0, LANE)
    ngt_ref[...] = jax.nn.sigmoid(misc).T
    pre = _dot(misc.astype(BF16), gw_ref[...]) + gb_ref[...]
    la_ref[...] = (jnp.minimum(pre, 0.0) - jnp.log(1.0 + jnp.exp(-jnp.abs(pre)))) * (1.0 / GLA_GATE_NORM)


def _proj(x, an, w_att, w_gla, w_go, w_misc, bd, gmq, gmk, gnq, gks, gkw, gw, gb, seq, tm=512):
    n = x.shape[0]
    row = lambda w: pl.BlockSpec((tm, w), lambda i: (i, 0))
    col = lambda w: pl.BlockSpec((w, tm), lambda i: (0, i))
    outs = [(256, BF16, True), (N_HEADS * LANE, BF16, False), (256, BF16, True), (256, BF16, True),
            (128, BF16, False), (256, BF16, False), (256, BF16, True), (512, BF16, False), (512, BF16, False),
            (256, F32, False), (512, BF16, False), (128, F32, True)]
    return pl.pallas_call(
        functools.partial(_proj_kernel, seq=seq),
        grid=(n // tm,),
        in_specs=[row(D_MODEL), _const_spec((1, D_MODEL)), _weight_spec(w_att.shape), _weight_spec(w_gla.shape),
                  _weight_spec(w_go.shape), _weight_spec(w_misc.shape), _const_spec((256, 256)),
                  _const_spec((1, 256)), _const_spec((1, 256)), _const_spec((1, 256)), _const_spec((1, 128)),
                  _const_spec((1, 128)), _const_spec((LANE, 256)), _const_spec((1, 256))],
        out_specs=[col(w) if tr else row(w) for w, _, tr in outs],
        out_shape=[jax.ShapeDtypeStruct((w, n) if tr else (n, w), dt) for w, dt, tr in outs],
        compiler_params=_params(("arbitrary",)),
        name="proj",
    )(x, an, w_att, w_gla, w_go, w_misc, bd, gmq, gmk, gnq, gks, gkw, gw, gb)


ATT_TILES_PER_STEP = 2


def _tile_in_step(q_tile):
    sub = q_tile % ATT_TILES_PER_STEP
    return slice(sub * ATT_TILE, (sub + 1) * ATT_TILE)


def _grid_steps(step, n_tiles):
    def group(first):
        def run():
            for q_tile in range(first, first + ATT_TILES_PER_STEP):
                step(q_tile)()
        return run
    return [group(first) for first in range(0, n_tiles, ATT_TILES_PER_STEP)]


def _softmax_pv(chains):
    def shifted(x, shift, sign):
        return x if shift is None else x + sign * shift

    maxes = [functools.reduce(jnp.maximum, [shifted(jnp.max(s, axis=0, keepdims=True), shift, 1)
                                            for s, _, shift in chain]) for chain in chains]
    probs = [[jnp.exp2(s - shifted(m, shift, -1)) for s, _, shift in chain] for chain, m in zip(chains, maxes)]
    sums = [functools.reduce(jnp.add, [jnp.sum(p, axis=0, keepdims=True) for p in ps]) for ps in probs]
    accs = [functools.reduce(jnp.add, [_dot(vt, p.astype(BF16)) for (_, vt, _), p in zip(chain, ps)])
            for chain, ps in zip(chains, probs)]
    return [acc / l for acc, l in zip(accs, sums)]


def _rank_rows(score, n):
    idx = lax.broadcasted_iota(jnp.int32, score.shape, 0)
    rank = jnp.zeros(score.shape, F32)
    for m in range(n):
        sm = score[m:m + 1, :]
        rank += jnp.where((sm > score) | ((sm == score) & (idx > m)), 1.0, 0.0)
    return rank


def _moba_kernel(qt_ref, k_ref, vt_ref, bias_ref, far_ref, o_ref, kmean_ref):
    t = ATT_TILE
    nh = N_HEADS
    nb = k_ref.shape[0] // t
    kcols = [slice(h * LANE, (h + 1) * LANE) for h in range(nh)]
    vrows = [slice(h * HEAD_DIM, (h + 1) * HEAD_DIM) for h in range(nh)]
    pen_rows = 16

    def step(q_tile):
        width = min(q_tile + 1, 2)
        near = slice((q_tile + 1 - width) * t, (q_tile + 1) * t)
        far = slice(0, (q_tile + 1 - width) * t)
        mine = _tile_in_step(q_tile)

        def run():
            if q_tile == 0:
                for h in range(nh):
                    kh = k_ref[:, h * LANE:h * LANE + HEAD_DIM].astype(F32)
                    kmean_ref[h] = jnp.mean(kh.reshape(nb, t, HEAD_DIM), axis=1)

            chains = []
            for h in range(nh):
                qt = qt_ref[vrows[h], mine]
                if q_tile > MOBA_TOPK:
                    past = lax.broadcasted_iota(jnp.int32, (nb, t), 0) < q_tile
                    gate = jnp.where(past, _dot(kmean_ref[h].astype(BF16), qt), NEG_INF)
                    pen = jnp.where(past & (_rank_rows(gate, q_tile) >= MOBA_TOPK), NEG_INF, 0.0)
                    pen = jnp.concatenate([pen, jnp.zeros((pen_rows - nb, t), F32)], axis=0).astype(BF16)
                else:
                    pen = jnp.zeros((pen_rows, t), BF16)
                q_aug = jnp.concatenate([qt, pen, jnp.zeros((LANE - HEAD_DIM - pen_rows, t), BF16)], axis=0)
                chain = []
                if far.stop > 0:
                    chain.append((_dot(k_ref[far, kcols[h]], q_aug), vt_ref[vrows[h], far], far_ref[h, 0:1, :]))
                chain.append((_dot(k_ref[near, kcols[h]], q_aug) + bias_ref[h, (2 - width) * t:2 * t, :],
                              vt_ref[vrows[h], near], None))
                chains.append(chain)
            o_ref[mine, :] = jnp.concatenate(_softmax_pv(chains), axis=0).T.astype(BF16)
        return run

    lax.switch(pl.program_id(1), _grid_steps(step, nb))


def _moba(mqt, mk, mvt, bias, far, batch, seq):
    t = ATT_TILE
    nq = seq // t
    steps = nq // ATT_TILES_PER_STEP
    qw = ATT_TILES_PER_STEP * t
    return pl.pallas_call(
        _moba_kernel,
        grid=(batch, steps),
        in_specs=[pl.BlockSpec((256, qw), lambda b, i: (0, b * steps + i)),
                  pl.BlockSpec((seq, N_HEADS * LANE), lambda b, i: (b, 0)),
                  pl.BlockSpec((256, seq), lambda b, i: (0, b)),
                  _const_spec((N_HEADS, 2 * t, t)), _const_spec((N_HEADS, 8, t))],
        out_specs=pl.BlockSpec((qw, 256), lambda b, i: (b * steps + i, 0)),
        out_shape=jax.ShapeDtypeStruct((batch * seq, 256), BF16),
        scratch_shapes=[pltpu.VMEM((N_HEADS, nq, HEAD_DIM), F32)],
        compiler_params=_params(("arbitrary", "arbitrary")),
        name="moba",
    )(mqt, mk, mvt, bias, far)


def _compress_kernel(x_ref, pa_ref, pb_ref, w1a_ref, w1b_ref, w2_ref, gk_ref, o_ref, ot_ref):
    x = x_ref[0].astype(F32)
    u = _dot((x + pa_ref[...]).astype(BF16), w1a_ref[...])
    v = _dot((x + pb_ref[...]).astype(BF16), w1b_ref[...])
    n = u.shape[0]
    hid = u + pltpu.roll(v, n - 1, axis=0)
    y = _dot(jax.nn.gelu(hid, approximate=True).astype(BF16), w2_ref[...])
    is_k = lax.broadcasted_iota(jnp.int32, y.shape, 1) < HEAD_DIM
    ms = jnp.sum(jnp.where(is_k, y * y, 0.0), axis=1, keepdims=True) * (1.0 / HEAD_DIM)
    y = jnp.where(is_k, y * lax.rsqrt(ms + NORM_EPS) * gk_ref[...], y)
    o_ref[0] = y.astype(BF16)
    ot_ref[0] = y.T.astype(BF16)


def _compress(kvc3, pa, pb, w1a, w1b, w2, gk):
    batch, n, w = kvc3.shape
    return pl.pallas_call(
        _compress_kernel,
        grid=(batch,),
        in_specs=[pl.BlockSpec((1, n, w), lambda b: (b, 0, 0)), _const_spec((1, w)), _const_spec((1, w)),
                  _const_spec((w, LANE)), _const_spec((w, LANE)), _const_spec((LANE, LANE)), _const_spec((1, LANE))],
        out_specs=[pl.BlockSpec((1, n, LANE), lambda b: (b, 0, 0)), pl.BlockSpec((1, LANE, n), lambda b: (b, 0, 0))],
        out_shape=[jax.ShapeDtypeStruct((batch, n, LANE), BF16), jax.ShapeDtypeStruct((batch, LANE, n), BF16)],
        compiler_params=_params(("arbitrary",)),
        name="compress",
    )(kvc3, pa, pb, w1a, w1b, w2, gk)


def _nsa_kernel(qt_ref, kc_ref, kct_ref, kv_ref, kvt_ref, ngt_ref, bias_ref, far_ref, ovt_ref, o_ref):
    t = ATT_TILE
    nh = N_HEADS
    sel_k, sel_v = slice(0, LANE), slice(HEAD_DIM, 2 * HEAD_DIM)
    win_k, win_v = slice(LANE, LANE + HEAD_DIM), slice(LANE + HEAD_DIM, 2 * LANE)

    def step(q_tile):
        def near(width):
            return slice((q_tile + 1 - width) * t, (q_tile + 1) * t), bias_ref[(3 - width) * t:3 * t, :]

        mine = _tile_in_step(q_tile)

        def run():
            qt_all = qt_ref[:, mine]
            qs = jnp.concatenate([qt_all[h * HEAD_DIM:(h + 1) * HEAD_DIM, :] for h in range(nh)], axis=1)

            keys, bias = near(min(q_tile + 1, NSA_WINDOW // t + 1))
            win = [(_dot(kv_ref[keys, win_k], qs) + bias, kvt_ref[win_v, keys], None)]

            kc = kc_ref[0][:, 0:HEAD_DIM]
            vct = kct_ref[0][HEAD_DIM:2 * HEAD_DIM, :]
            n_cmp = kc.shape[0]
            pos = q_tile * t + (lax.broadcasted_iota(jnp.int32, (n_cmp, nh * t), 1) & (t - 1))
            cend = lax.broadcasted_iota(jnp.int32, (n_cmp, nh * t), 0) * NSA_CMP_STRIDE + (NSA_CMP_LEN - 1)
            vis = cend <= pos
            sc = jnp.where(vis, _dot(kc, qs), NEG_INF)
            e = jnp.where(vis, jnp.exp2(sc - jnp.max(sc, axis=0, keepdims=True)), 0.0)
            den = jnp.sum(e, axis=0, keepdims=True)
            p = e / jnp.where(den > 0.0, den, 1.0)
            o_cmp = _dot(vct, p.astype(BF16))

            ovt = ovt_ref[...]
            n_blk = ovt.shape[0]
            blk = lax.broadcasted_iota(jnp.int32, (n_blk, t), 0)
            cur = (q_tile * t + lax.broadcasted_iota(jnp.int32, (n_blk, t), 1)) // NSA_SEL_LEN
            keep = blk <= cur
            max_visible = min(n_blk, (q_tile + 1) * t // NSA_SEL_LEN)
            if max_visible > NSA_SEL_TOPN:
                p_sum = p[:, 0:t] + p[:, t:2 * t] + p[:, 2 * t:3 * t] + p[:, 3 * t:4 * t]
                p_hi = p_sum.astype(BF16)
                p_lo = (p_sum - p_hi.astype(F32)).astype(BF16)
                imp = _dot(ovt, p_hi) + _dot(ovt, p_lo)
                forced = (blk == 0) | (blk == cur) | (blk == cur - 1)
                imp = jnp.where(keep, imp + jnp.where(forced, NSA_FORCE_BONUS, 0.0), NEG_INF)
                keep = keep & (_rank_rows(imp, max_visible) < NSA_SEL_TOPN)
            pen = jnp.where(keep, 0.0, NEG_INF).astype(BF16)
            q_aug = jnp.concatenate([qs, jnp.concatenate([pen] * nh, axis=1),
                                     jnp.zeros((LANE - HEAD_DIM - n_blk, nh * t), BF16)], axis=0)

            keys, bias = near(min(q_tile + 1, 2))
            sel = [(_dot(kv_ref[keys, sel_k], q_aug) + bias, kvt_ref[sel_v, keys], None)]
            if keys.start > 0:
                far = slice(0, keys.start)
                sel.insert(0, (_dot(kv_ref[far, sel_k], q_aug), kvt_ref[sel_v, far], far_ref[0:1, :]))
            o_slc, o_win = _softmax_pv([sel, win])

            outs = []
            for h in range(nh):
                cs = slice(h * t, (h + 1) * t)
                g = [ngt_ref[_MISC_NGATE + 3 * h + i:_MISC_NGATE + 3 * h + i + 1, mine] for i in range(3)]
                outs.append(g[0] * o_cmp[:, cs] + g[1] * o_slc[:, cs] + g[2] * o_win[:, cs])
            o_ref[mine, :] = jnp.concatenate(outs, axis=0).T.astype(BF16)
        return run

    lax.switch(pl.program_id(1), _grid_steps(step, kv_ref.shape[0] // t))


def _nsa(nqt, kc, kct, nkv, nkvt, ngt, bias, far, ovt, batch, seq):
    t = ATT_TILE
    nq = seq // t
    assert NSA_WINDOW == 2 * t and seq // NSA_SEL_LEN <= LANE - HEAD_DIM
    n_cmp = kc.shape[1]
    n_blk = ovt.shape[0]
    steps = nq // ATT_TILES_PER_STEP
    qw = ATT_TILES_PER_STEP * t
    return pl.pallas_call(
        _nsa_kernel,
        grid=(batch, steps),
        in_specs=[pl.BlockSpec((256, qw), lambda b, i: (0, b * steps + i)),
                  pl.BlockSpec((1, n_cmp, LANE), lambda b, i: (b, 0, 0)),
                  pl.BlockSpec((1, LANE, n_cmp), lambda b, i: (b, 0, 0)),
                  pl.BlockSpec((seq, 256), lambda b, i: (b, 0)),
                  pl.BlockSpec((256, seq), lambda b, i: (0, b)),
                  pl.BlockSpec((LANE, qw), lambda b, i: (0, b * steps + i)),
                  _weight_spec((3 * t, N_HEADS * t)), _const_spec((8, N_HEADS * t)),
                  _const_spec((n_blk, n_cmp))],
        out_specs=pl.BlockSpec((qw, 256), lambda b, i: (b * steps + i, 0)),
        out_shape=jax.ShapeDtypeStruct((batch * seq, 256), BF16),
        compiler_params=_params(("arbitrary", "arbitrary")),
        name="nsa",
    )(nqt, kc, kct, nkv, nkvt, ngt, bias, far, ovt)


def _gla_kernel(qk_ref, v_ref, la_ref, go_ref, tri_ref, gn_ref, o_ref, st_ref):
    c = GLA_CHUNK
    nh = N_HEADS
    n_chunk = qk_ref.shape[0] // c
    st_ref[...] = jnp.zeros_like(st_ref)
    tri = tri_ref[...]
    row = lax.broadcasted_iota(jnp.int32, (c, nh * GLA_DK), 0)
    sub_causal = (lax.broadcasted_iota(jnp.int32, (GLA_SUB, c), 0)
                  - lax.broadcasted_iota(jnp.int32, (GLA_SUB, c), 1))

    hks = [slice(h * GLA_DK, (h + 1) * GLA_DK) for h in range(nh)]
    hvs = [slice(h * GLA_DV, (h + 1) * GLA_DV) for h in range(nh)]
    group_size = GLA_UNROLL

    def group(gi, _):
        rows = [pl.ds(pl.multiple_of((gi * group_size + u) * c, c), c) for u in range(group_size)]
        units = range(group_size)

        bs = []
        for u in units:
            g = la_ref[rows[u], :]
            g1 = g.astype(BF16)
            r1 = g - g1.astype(F32)
            g2 = r1.astype(BF16)
            g3 = (r1 - g2.astype(F32)).astype(BF16)
            bs.append(_dot(tri, g1) + _dot(tri, g2) + _dot(tri, g3))

        q_inter, k_state, decay, q_sub, k_sub, vs_ = [], [], [], [], [], []
        for u in units:
            b = bs[u]
            q = qk_ref[rows[u], 0:256].astype(F32)
            k = qk_ref[rows[u], 256:512].astype(F32)
            b_last = b[c - 1:c, :]
            q_inter.append((q * jnp.exp(b)).astype(BF16))
            k_state.append((k * jnp.exp(b_last - b)).astype(BF16))
            decay.append(jnp.exp(b_last))
            qs_u, ks_u = [], []
            for i in range(c // GLA_SUB):
                lo, hi = i * GLA_SUB, (i + 1) * GLA_SUB
                ref_b = b[lo:lo + 1, :]
                ks_u.append((k * jnp.exp(jnp.where(row < hi, ref_b - b, 0.0))).astype(BF16))
                qs_u.append((q[lo:hi] * jnp.exp(b[lo:hi] - ref_b)).astype(BF16))
            q_sub.append(qs_u)
            k_sub.append(ks_u)
            vs_.append(v_ref[rows[u], :])

        o_intra, kv = [], []
        for u in units:
            a_h = []
            for h in range(nh):
                blocks = [jnp.where(sub_causal + i * GLA_SUB >= 0,
                                    _dot_nt(q_sub[u][i][:, hks[h]], k_sub[u][i][:, hks[h]]), 0.0)
                          for i in range(c // GLA_SUB)]
                a_h.append(jnp.concatenate(blocks, axis=0).astype(BF16))
            o_intra.append([_dot(a_h[h], vs_[u][:, hvs[h]]) for h in range(nh)])
            kv.append([_dot_tn(vs_[u][:, hvs[h]], k_state[u][:, hks[h]]) for h in range(nh)])

        st = [st_ref[h] for h in range(nh)]
        for u in units:
            outs = []
            for h in range(nh):
                o = o_intra[u][h] + _dot_nt(q_inter[u][:, hks[h]], st[h].astype(BF16))
                st[h] = st[h] * decay[u][:, hks[h]] + kv[u][h]
                outs.append(_rms_rows(o, gn_ref[...]))
            o_ref[rows[u], :] = (jnp.concatenate(outs, axis=1) * go_ref[rows[u], :].astype(F32)).astype(BF16)
        for h in range(nh):
            st_ref[h] = st[h]
        return 0

    lax.fori_loop(0, n_chunk // group_size, group, 0)


def _gla(gqk, gv, la, go, tri, gn, batch, seq):
    spec = lambda w: pl.BlockSpec((seq, w), lambda b: (b, 0))
    return pl.pallas_call(
        _gla_kernel,
        grid=(batch,),
        in_specs=[spec(512), spec(512), spec(256), spec(512), _const_spec((GLA_CHUNK, GLA_CHUNK)),
                  _const_spec((1, GLA_DV))],
        out_specs=spec(512),
        out_shape=jax.ShapeDtypeStruct((batch * seq, 512), BF16),
        scratch_shapes=[pltpu.VMEM((N_HEADS, GLA_DV, GLA_DK), F32)],
        compiler_params=_params(("arbitrary",)),
        name="gla",
    )(gqk, gv, la, go, tri, gn)


def _merge_kernel(x_ref, an_ref, om_ref, on_ref, og_ref, wm_ref, pm_ref, pn_ref, pg_ref, wo_ref, o_ref):
    x = x_ref[...]
    h = _rms_rows(x, an_ref[...]).astype(BF16)
    gates = [_dot(h, wm_ref[:, i * D_MODEL:(i + 1) * D_MODEL]) for i in range(3)]
    branches = [_dot(o_b[...], p_b[...]) for o_b, p_b in ((om_ref, pm_ref), (on_ref, pn_ref), (og_ref, pg_ref))]
    gated = [jax.nn.sigmoid(g) * b for g, b in zip(gates, branches)]
    z = gated[0] + gated[1] + gated[2]
    o_ref[...] = x + _dot(z.astype(BF16), wo_ref[...])


def _merge(x, an, om, on, og, wm, pm, pn, pg, wo, tm=512):
    n = x.shape[0]
    row = lambda w: pl.BlockSpec((tm, w), lambda i: (i, 0))
    return pl.pallas_call(
        _merge_kernel,
        grid=(n // tm,),
        in_specs=[row(D_MODEL), _const_spec((1, D_MODEL)), row(256), row(256), row(512),
                  _weight_spec((D_MODEL, 3 * D_MODEL)), _weight_spec((256, D_MODEL)), _weight_spec((256, D_MODEL)),
                  _weight_spec((512, D_MODEL)), _weight_spec((D_MODEL, D_MODEL))],
        out_specs=row(D_MODEL),
        out_shape=jax.ShapeDtypeStruct((n, D_MODEL), F32),
        compiler_params=_params(("arbitrary",)),
        name="merge",
    )(x, an, om, on, og, wm, pm, pn, pg, wo)


FFN_CHUNK = 256


def _ffn_kernel(x_ref, fn_ref, wa_ref, wg_ref, cw_ref, cb_ref, wd_ref, o_ref, carry_ref, act_ref, *, tiles_per_seq):
    i = pl.program_id(0)
    tm = x_ref.shape[0]
    x = x_ref[...]
    h = _rms_rows(x, fn_ref[...]).astype(BF16)
    row = lax.broadcasted_iota(jnp.int32, (tm, FFN_CHUNK), 0)

    @pl.when((i % tiles_per_seq) == 0)
    def _():
        carry_ref[...] = jnp.zeros_like(carry_ref)

    for c in range(D_FF // FFN_CHUNK):
        cs = slice(c * FFN_CHUNK, (c + 1) * FFN_CHUNK)
        a = _dot(h, wa_ref[:, cs])
        g = _dot(h, wg_ref[:, cs])
        prev = carry_ref[:, cs]
        p1 = prev[7:8, :]
        p2 = prev[6:7, :]
        a1 = jnp.where(row == 0, p1, pltpu.roll(a, 1, axis=0))
        a2 = jnp.where(row == 0, p2, jnp.where(row == 1, p1, pltpu.roll(a, 2, axis=0)))
        carry_ref[:, cs] = a[tm - 8:tm, :]
        w = cw_ref[:, cs]
        conv = w[0:1, :] * a2 + w[1:2, :] * a1 + w[2:3, :] * a + cb_ref[:, cs]
        act_ref[:, cs] = (jax.nn.gelu(conv, approximate=True) * g).astype(BF16)
    o_ref[...] = x + _dot(act_ref[...], wd_ref[...])


def _ffn(x, fn, wa, wg, cw, cb, wd, seq, tm=1024):
    n = x.shape[0]
    row = pl.BlockSpec((tm, D_MODEL), lambda i: (i, 0))
    return pl.pallas_call(
        functools.partial(_ffn_kernel, tiles_per_seq=seq // tm),
        grid=(n // tm,),
        in_specs=[row, _const_spec((1, D_MODEL)), _weight_spec((D_MODEL, D_FF)), _weight_spec((D_MODEL, D_FF)),
                  _const_spec((8, D_FF)), _const_spec((1, D_FF)), _weight_spec((D_FF, D_MODEL))],
        out_specs=row,
        out_shape=jax.ShapeDtypeStruct((n, D_MODEL), F32),
        scratch_shapes=[pltpu.VMEM((8, D_FF), F32), pltpu.VMEM((tm, D_FF), BF16)],
        compiler_params=_params(("arbitrary",)),
        name="ffn",
    )(x, fn, wa, wg, cw, cb, wd)


def _rel_bucket(dist):
    n = jnp.maximum(dist, 0)
    max_exact = REL_BUCKETS // 2
    nf = jnp.maximum(n, 1).astype(F32)
    large = max_exact + (jnp.log(nf / max_exact) / math.log(REL_MAX_DIST / max_exact)
                         * (REL_BUCKETS - max_exact)).astype(jnp.int32)
    return jnp.where(n < max_exact, n, jnp.minimum(large, REL_BUCKETS - 1))


def _bias_tiles(rel_tab):
    t = ATT_TILE
    d0 = jnp.arange(t)[None, :] - jnp.arange(t)[:, None]
    bucket = jnp.stack([_rel_bucket(d0 + k * t) for k in range(3)])
    out = jnp.zeros((rel_tab.shape[1],) + bucket.shape, F32)
    for b in range(REL_BUCKETS):
        out = jnp.where(bucket[None] == b, rel_tab[b][:, None, None, None], out)
    return out


def _block_diag_mean():
    g = jnp.arange(256) // HEAD_DIM
    return jnp.where(g[:, None] == g[None, :], 1.0 / HEAD_DIM, 0.0).astype(BF16)


def _overlap_t(n_cmp_pad, n_cmp, n_blk):
    tok = jnp.arange(n_blk * NSA_SEL_LEN)
    starts = jnp.arange(n_cmp_pad) * NSA_CMP_STRIDE
    inside = (tok[None, :] >= starts[:, None]) & (tok[None, :] < starts[:, None] + NSA_CMP_LEN)
    m = inside.reshape(n_cmp_pad, n_blk, NSA_SEL_LEN).sum(-1).astype(F32) / NSA_CMP_LEN
    m = jnp.where(jnp.arange(n_cmp_pad)[:, None] < n_cmp, m, 0.0)
    return m.T.astype(BF16)


def _tile_gain(g, reps):
    return jnp.tile(g.astype(F32), reps)[None, :]


def kernel(x, rel_bias, attn_norm, w_in, moba_q_norm, moba_k_norm, nsa_q_norm, nsa_k_norm, cmp_pos_k, cmp_pos_v,
           cmp_k_w1, cmp_k_w2, cmp_v_w1, cmp_v_w2, gla_gate_w, gla_gate_b, gla_out_norm, w_branch_moba,
           w_branch_nsa, w_branch_gla, w_out, ffn_norm, w_up, conv_w, conv_b, w_down):
    batch, seq, _ = x.shape
    depth = w_in.shape[0]
    n_cmp = seq // NSA_CMP_STRIDE - NSA_CMP_LEN // NSA_CMP_STRIDE + 1
    n_cmp_pad = seq // NSA_CMP_STRIDE
    n_blk = seq // NSA_SEL_LEN

    tiles = _bias_tiles(rel_bias.astype(F32)) * LOG2_E
    key_le_query = jnp.arange(ATT_TILE)[:, None] <= jnp.arange(ATT_TILE)[None, :]
    own = jnp.where(key_le_query, tiles[:, 0], NEG_INF)
    band = jnp.where(key_le_query, NEG_INF, tiles[:, 2])
    near, far = tiles[:, 1], tiles[:, 2, :8]
    bias_moba = jnp.concatenate([near[:N_HEADS], own[:N_HEADS]], axis=1)
    far_moba = far[:N_HEADS]
    heads_on_lanes = lambda a: jnp.concatenate(list(a[N_HEADS:]), axis=1)
    bias_nsa = jnp.concatenate([heads_on_lanes(band), heads_on_lanes(near), heads_on_lanes(own)], axis=0)
    far_nsa = heads_on_lanes(far)
    bd = _block_diag_mean()
    ovt = _overlap_t(n_cmp_pad, n_cmp, n_blk)
    tri = (jnp.arange(GLA_CHUNK)[:, None] >= jnp.arange(GLA_CHUNK)[None, :]).astype(BF16)
    ones64 = jnp.ones((HEAD_DIM,), F32)

    xf = x.reshape(batch * seq, D_MODEL)
    for l in range(depth):
        w_att, w_gla, w_go, w_glr, w_ng, wmerge = _wsplit(w_in, l)
        w_misc = jnp.pad(jnp.concatenate([w_glr, w_ng], axis=1), ((0, 0), (0, LANE - GLA_GATE_RANK - 3 * N_HEADS)))
        gw = jnp.pad(gla_gate_w[l], ((0, LANE - GLA_GATE_RANK), (0, 0))).astype(BF16)

        mqt, mk_a, mvt, nqt, kvc, nkv_a, nkvt, gqk, gv_a, la, go, ngt = _proj(
            xf, attn_norm[l][None, :], w_att, w_gla, w_go, w_misc, bd,
            _tile_gain(moba_q_norm[l], 4), _tile_gain(moba_k_norm[l], 4), _tile_gain(nsa_q_norm[l], 4),
            jnp.concatenate([nsa_k_norm[l, 1], ones64])[None, :], jnp.concatenate([nsa_k_norm[l, 2], ones64])[None, :],
            gw, gla_gate_b[l][None, :], seq)

        o_moba = _moba(mqt, mk_a, mvt, bias_moba, far_moba, batch, seq)

        half = NSA_CMP_STRIDE * HEAD_DIM
        zero = jnp.zeros((NSA_CMP_STRIDE, HEAD_DIM, HEAD_DIM), F32)

        def w1_part(part):
            wk = cmp_k_w1[l][part * half:(part + 1) * half].reshape(NSA_CMP_STRIDE, HEAD_DIM, HEAD_DIM)
            wv = cmp_v_w1[l][part * half:(part + 1) * half].reshape(NSA_CMP_STRIDE, HEAD_DIM, HEAD_DIM)
            top = jnp.concatenate([wk, zero], axis=2)
            bot = jnp.concatenate([zero, wv], axis=2)
            return jnp.concatenate([top, bot], axis=1).reshape(NSA_CMP_STRIDE * LANE, LANE).astype(BF16)

        def pos_part(part):
            pk = cmp_pos_k[l][part * NSA_CMP_STRIDE:(part + 1) * NSA_CMP_STRIDE]
            pv = cmp_pos_v[l][part * NSA_CMP_STRIDE:(part + 1) * NSA_CMP_STRIDE]
            return jnp.concatenate([pk, pv], axis=1).reshape(1, NSA_CMP_STRIDE * LANE).astype(F32)

        z64 = jnp.zeros((HEAD_DIM, HEAD_DIM), F32)
        w2 = jnp.concatenate([jnp.concatenate([cmp_k_w2[l], z64], axis=1),
                              jnp.concatenate([z64, cmp_v_w2[l]], axis=1)], axis=0).astype(BF16)
        kcv, kcvt = _compress(kvc.reshape(batch, n_cmp_pad, NSA_CMP_STRIDE * LANE), pos_part(0), pos_part(1),
                              w1_part(0), w1_part(1), w2, jnp.concatenate([nsa_k_norm[l, 0], ones64])[None, :])

        o_nsa = _nsa(nqt, kcv, kcvt, nkv_a, nkvt, ngt, bias_nsa, far_nsa, ovt, batch, seq)
        o_gla = _gla(gqk, gv_a, la, go, tri, gla_out_norm[l][None, :].astype(F32), batch, seq)

        xf = _merge(xf, attn_norm[l][None, :], o_moba, o_nsa, o_gla, wmerge.astype(BF16),
                    w_branch_moba[l].astype(BF16), w_branch_nsa[l].astype(BF16), w_branch_gla[l].astype(BF16),
                    w_out[l].astype(BF16))

        cw = jnp.pad(conv_w[l], ((0, 8 - conv_w.shape[1]), (0, 0)))
        xf = _ffn(xf, ffn_norm[l][None, :], w_up[l][:, :D_FF].astype(BF16), w_up[l][:, D_FF:].astype(BF16),
                  cw, conv_b[l][None, :], w_down[l].astype(BF16), seq)
    return xf.reshape(batch, seq, D_MODEL)
```

```python
import functools
import math

import jax
import jax.numpy as jnp
from jax import lax
from jax.experimental import pallas as pl
from jax.experimental.pallas import tpu as pltpu

F32 = jnp.float32
BF16 = jnp.bfloat16

D_MODEL = 1024
HEAD_DIM = 64
N_HEADS = 4
MOBA_BLOCK = 256
MOBA_TOPK = 3
NSA_CMP_LEN = 32
NSA_CMP_STRIDE = 16
NSA_SEL_LEN = 64
NSA_SEL_TOPN = 16
NSA_WINDOW = 512
NSA_FORCE_BONUS = 1e4
GLA_DK = 64
GLA_DV = 128
GLA_GATE_RANK = 16
GLA_GATE_NORM = 16.0
GLA_CHUNK = 64
GLA_SUB = 16
GLA_UNROLL = 8
D_FF = 2816
REL_BUCKETS = 32
REL_MAX_DIST = 128
NORM_EPS = 1e-6
NEG_INF = -1e30
LOG2_E = math.log2(math.e)

LANE = 128
V7X_VMEM_BYTES = 64 * 1024 * 1024
VMEM_LIMIT = V7X_VMEM_BYTES - 8 * 1024 * 1024

PROJ_ROWS = 512
MERGE_ROWS = 512
FFN_ROWS = 1024
WSPLIT_ROWS = 256
ATT_TILE = 256

_C_MQ, _C_MK, _C_MV, _C_NQ = 0, 256, 512, 768
_C_KVC, _C_KSVS, _C_KWVW, _C_A_END = 1024, 1152, 1280, 1408
_MISC_NGATE = GLA_GATE_RANK


def _dot(a, b):
    return jnp.dot(a, b, preferred_element_type=F32)


def _dot_nt(a, b):
    return lax.dot_general(a, b, (((1,), (1,)), ((), ())), preferred_element_type=F32)


def _dot_tn(a, b):
    return lax.dot_general(a, b, (((0,), (0,)), ((), ())), preferred_element_type=F32)


def _rms_rows(x, gain):
    ms = jnp.mean(x * x, axis=-1, keepdims=True)
    return x * lax.rsqrt(ms + NORM_EPS) * gain


def _params(sem):
    return pltpu.CompilerParams(dimension_semantics=sem, vmem_limit_bytes=VMEM_LIMIT)


def _const_spec(shape):
    return pl.BlockSpec(shape, lambda *_: (0,) * len(shape))


def _weight_spec(shape):
    return pl.BlockSpec(shape, lambda *_: (0,) * len(shape), pipeline_mode=pl.Buffered(1))


_W_NG, _W_GLA, _W_GLR, _W_GO, _W_MERGE, _W_END = 1408, 1420, 2444, 2460, 2972, 6044


def _wsplit_kernel(w_ref, att_ref, gla_ref, go_ref, glr_ref, ng_ref, merge_ref):
    w = w_ref[0]
    att_ref[...] = w[:, 0:_W_NG].astype(BF16)
    ng_ref[...] = w[:, _W_NG:_W_GLA].astype(BF16)
    gla_ref[...] = w[:, _W_GLA:_W_GLR].astype(BF16)
    glr_ref[...] = w[:, _W_GLR:_W_GO].astype(BF16)
    go_ref[...] = w[:, _W_GO:_W_MERGE].astype(BF16)
    merge_ref[...] = w[:, _W_MERGE:_W_END].astype(BF16)


def _wsplit(w_in, layer, tr=WSPLIT_ROWS):
    widths = [_W_NG, _W_GLR - _W_GLA, _W_MERGE - _W_GO, _W_GO - _W_GLR, _W_GLA - _W_NG, _W_END - _W_MERGE]
    return pl.pallas_call(
        _wsplit_kernel,
        grid=(D_MODEL // tr,),
        in_specs=[pl.BlockSpec((1, tr, _W_END), lambda i: (layer, i, 0))],
        out_specs=[pl.BlockSpec((tr, w), lambda i: (i, 0)) for w in widths],
        out_shape=[jax.ShapeDtypeStruct((D_MODEL, w), BF16) for w in widths],
        compiler_params=_params(("arbitrary",)),
        name="wsplit",
    )(w_in)


def _proj_kernel(x_ref, an_ref, wa_ref, wg_ref, wo_ref, wm_ref, bd_ref, gmq_ref, gmk_ref, gnq_ref, gks_ref, gkw_ref,
                 gw_ref, gb_ref,
                 mqt_ref, mk_ref, mvt_ref, nqt_ref, kvc_ref, nkv_ref, nkvt_ref, gqk_ref, gv_ref, la_ref, go_ref,
                 ngt_ref, *, seq):
    tm = x_ref.shape[0]
    h = _rms_rows(x_ref[...], an_ref[...]).astype(BF16)
    pos = (pl.program_id(0) * tm + lax.broadcasted_iota(jnp.int32, (tm, LANE), 0)) % seq
    lane = lax.broadcasted_iota(jnp.int32, (tm, LANE), 1)

    y_all = {id(w_ref): _dot(h, w_ref[...]) for w_ref in (wa_ref, wg_ref, wo_ref, wm_ref)}

    def sec(w_ref, lo, hi):
        return y_all[id(w_ref)][:, lo:hi]

    def head_norm(y, gain):
        w = y.shape[1]
        ms = _dot((y * y).astype(BF16), bd_ref[0:w, 0:w])
        return y * lax.rsqrt(ms + NORM_EPS) * gain

    scale = HEAD_DIM ** -0.5 * LOG2_E
    mqt_ref[...] = (head_norm(sec(wa_ref, _C_MQ, _C_MK), gmq_ref[...]) * scale).T.astype(BF16)
    mk = head_norm(sec(wa_ref, _C_MK, _C_MV), gmk_ref[...])
    pos_h = (pl.program_id(0) * tm + lax.broadcasted_iota(jnp.int32, (tm, HEAD_DIM), 0)) % seq
    lane_h = lax.broadcasted_iota(jnp.int32, (tm, HEAD_DIM), 1)
    moba_onehot = jnp.where(lane_h == pos_h // MOBA_BLOCK, 1.0, 0.0)
    for hd in range(N_HEADS):
        mk_ref[:, hd * LANE:(hd + 1) * LANE] = jnp.concatenate(
            [mk[:, hd * HEAD_DIM:(hd + 1) * HEAD_DIM], moba_onehot], axis=1).astype(BF16)
    mvt_ref[...] = sec(wa_ref, _C_MV, _C_NQ).T.astype(BF16)
    nqt_ref[...] = (head_norm(sec(wa_ref, _C_NQ, _C_KVC), gnq_ref[...]) * scale).T.astype(BF16)
    kvc_ref[...] = sec(wa_ref, _C_KVC, _C_KSVS).astype(BF16)
    first_half = lane < HEAD_DIM
    ksvs = sec(wa_ref, _C_KSVS, _C_KWVW)
    ksvs = jnp.where(first_half, head_norm(ksvs, gks_ref[...]), ksvs)
    nkvt_ref[0:128, :] = ksvs.T.astype(BF16)
    sel_onehot = jnp.where(lane - HEAD_DIM == pos // NSA_SEL_LEN, 1.0, 0.0)
    nkv_ref[:, 0:128] = jnp.where(first_half, ksvs, sel_onehot).astype(BF16)
    kwvw = sec(wa_ref, _C_KWVW, _C_A_END)
    kwvw = jnp.where(first_half, head_norm(kwvw, gkw_ref[...]), kwvw)
    nkv_ref[:, 128:256] = kwvw.astype(BF16)
    nkvt_ref[128:256, :] = kwvw.T.astype(BF16)
    gqk_ref[:, 0:256] = (sec(wg_ref, 0, 256) * (GLA_DK ** -0.5)).astype(BF16)
    gqk_ref[:, 256:512] = sec(wg_ref, 256, 512).astype(BF16)
    gv_ref[...] = sec(wg_ref, 512, 1024).astype(BF16)
    go_ref[...] = jax.nn.silu(sec(wo_ref, 0, 512)).astype(BF16)
    misc = sec(wm_ref, 0, LANE)
    ngt_ref[...] = jax.nn.sigmoid(misc).T
    pre = _dot(misc.astype(BF16), gw_ref[...]) + gb_ref[...]
    la_ref[...] = (jnp.minimum(pre, 0.0) - jnp.log(1.0 + jnp.exp(-jnp.abs(pre)))) * (1.0 / GLA_GATE_NORM)


def _proj(x, an, w_att, w_gla, w_go, w_misc, bd, gmq, gmk, gnq, gks, gkw, gw, gb, seq, tm=PROJ_ROWS):
    n = x.shape[0]
    row = lambda w: pl.BlockSpec((tm, w), lambda i: (i, 0))
    col = lambda w: pl.BlockSpec((w, tm), lambda i: (0, i))
    outs = [(256, BF16, True), (N_HEADS * LANE, BF16, False), (256, BF16, True), (256, BF16, True),
            (128, BF16, False), (256, BF16, False), (256, BF16, True), (512, BF16, False), (512, BF16, False),
            (256, F32, False), (512, BF16, False), (128, F32, True)]
    return pl.pallas_call(
        functools.partial(_proj_kernel, seq=seq),
        grid=(n // tm,),
        in_specs=[row(D_MODEL), _const_spec((1, D_MODEL)), _weight_spec(w_att.shape), _weight_spec(w_gla.shape),
                  _weight_spec(w_go.shape), _weight_spec(w_misc.shape), _const_spec((256, 256)),
                  _const_spec((1, 256)), _const_spec((1, 256)), _const_spec((1, 256)), _const_spec((1, 128)),
                  _const_spec((1, 128)), _const_spec((LANE, 256)), _const_spec((1, 256))],
        out_specs=[col(w) if tr else row(w) for w, _, tr in outs],
        out_shape=[jax.ShapeDtypeStruct((w, n) if tr else (n, w), dt) for w, dt, tr in outs],
        compiler_params=_params(("arbitrary",)),
        name="proj",
    )(x, an, w_att, w_gla, w_go, w_misc, bd, gmq, gmk, gnq, gks, gkw, gw, gb)


ATT_TILES_PER_STEP = 4


def _tile_in_step(q_tile):
    sub = q_tile % ATT_TILES_PER_STEP
    return slice(sub * ATT_TILE, (sub + 1) * ATT_TILE)


def _grid_steps(step, n_tiles):
    def group(first):
        def run():
            for q_tile in range(first, first + ATT_TILES_PER_STEP):
                step(q_tile)()
        return run
    return [group(first) for first in range(0, n_tiles, ATT_TILES_PER_STEP)]


def _softmax_pv(chains):
    def shifted(x, shift, sign):
        return x if shift is None else x + sign * shift

    maxes = [functools.reduce(jnp.maximum, [shifted(jnp.max(s, axis=0, keepdims=True), shift, 1)
                                            for s, _, shift in chain]) for chain in chains]
    probs = [[jnp.exp2(s - shifted(m, shift, -1)) for s, _, shift in chain] for chain, m in zip(chains, maxes)]
    sums = [functools.reduce(jnp.add, [jnp.sum(p, axis=0, keepdims=True) for p in ps]) for ps in probs]
    accs = [functools.reduce(jnp.add, [_dot(vt, p.astype(BF16)) for (_, vt, _), p in zip(chain, ps)])
            for chain, ps in zip(chains, probs)]
    return [acc / l for acc, l in zip(accs, sums)]


def _rank_rows(score, n):
    idx = lax.broadcasted_iota(jnp.int32, score.shape, 0)
    rank = jnp.zeros(score.shape, F32)
    for m in range(n):
        sm = score[m:m + 1, :]
        rank += jnp.where((sm > score) | ((sm == score) & (idx > m)), 1.0, 0.0)
    return rank


def _moba_kernel(qt_ref, k_ref, vt_ref, bias_ref, far_ref, o_ref, kmean_ref):
    t = ATT_TILE
    nh = N_HEADS
    nb = k_ref.shape[0] // t
    kcols = [slice(h * LANE, (h + 1) * LANE) for h in range(nh)]
    vrows = [slice(h * HEAD_DIM, (h + 1) * HEAD_DIM) for h in range(nh)]
    pen_rows = 16

    def step(q_tile):
        width = min(q_tile + 1, 2)
        near = slice((q_tile + 1 - width) * t, (q_tile + 1) * t)
        far = slice(0, (q_tile + 1 - width) * t)
        mine = _tile_in_step(q_tile)

        def run():
            if q_tile == 0:
                for h in range(nh):
                    kh = k_ref[:, h * LANE:h * LANE + HEAD_DIM].astype(F32)
                    kmean_ref[h] = jnp.mean(kh.reshape(nb, t, HEAD_DIM), axis=1)

            chains = []
            for h in range(nh):
                qt = qt_ref[vrows[h], mine]
                if q_tile > MOBA_TOPK:
                    past = lax.broadcasted_iota(jnp.int32, (nb, t), 0) < q_tile
                    gate = jnp.where(past, _dot(kmean_ref[h].astype(BF16), qt), NEG_INF)
                    pen = jnp.where(past & (_rank_rows(gate, q_tile) >= MOBA_TOPK), NEG_INF, 0.0)
                    pen = jnp.concatenate([pen, jnp.zeros((pen_rows - nb, t), F32)], axis=0).astype(BF16)
                else:
                    pen = jnp.zeros((pen_rows, t), BF16)
                q_aug = jnp.concatenate([qt, pen, jnp.zeros((LANE - HEAD_DIM - pen_rows, t), BF16)], axis=0)
                chain = []
                if far.stop > 0:
                    chain.append((_dot(k_ref[far, kcols[h]], q_aug), vt_ref[vrows[h], far], far_ref[h, 0:1, :]))
                chain.append((_dot(k_ref[near, kcols[h]], q_aug) + bias_ref[h, (2 - width) * t:2 * t, :],
                              vt_ref[vrows[h], near], None))
                chains.append(chain)
            o_ref[mine, :] = jnp.concatenate(_softmax_pv(chains), axis=0).T.astype(BF16)
        return run

    lax.switch(pl.program_id(1), _grid_steps(step, nb))


def _moba(mqt, mk, mvt, bias, far, batch, seq):
    t = ATT_TILE
    nq = seq // t
    steps = nq // ATT_TILES_PER_STEP
    qw = ATT_TILES_PER_STEP * t
    return pl.pallas_call(
        _moba_kernel,
        grid=(batch, steps),
        in_specs=[pl.BlockSpec((256, qw), lambda b, i: (0, b * steps + i)),
                  pl.BlockSpec((seq, N_HEADS * LANE), lambda b, i: (b, 0)),
                  pl.BlockSpec((256, seq), lambda b, i: (0, b)),
                  _const_spec((N_HEADS, 2 * t, t)), _const_spec((N_HEADS, 8, t))],
        out_specs=pl.BlockSpec((qw, 256), lambda b, i: (b * steps + i, 0)),
        out_shape=jax.ShapeDtypeStruct((batch * seq, 256), BF16),
        scratch_shapes=[pltpu.VMEM((N_HEADS, nq, HEAD_DIM), F32)],
        compiler_params=_params(("arbitrary", "arbitrary")),
        name="moba",
    )(mqt, mk, mvt, bias, far)


def _compress_kernel(x_ref, pa_ref, pb_ref, w1a_ref, w1b_ref, w2_ref, gk_ref, o_ref, ot_ref):
    x = x_ref[0].astype(F32)
    u = _dot((x + pa_ref[...]).astype(BF16), w1a_ref[...])
    v = _dot((x + pb_ref[...]).astype(BF16), w1b_ref[...])
    n = u.shape[0]
    hid = u + pltpu.roll(v, n - 1, axis=0)
    y = _dot(jax.nn.gelu(hid, approximate=True).astype(BF16), w2_ref[...])
    is_k = lax.broadcasted_iota(jnp.int32, y.shape, 1) < HEAD_DIM
    ms = jnp.sum(jnp.where(is_k, y * y, 0.0), axis=1, keepdims=True) * (1.0 / HEAD_DIM)
    y = jnp.where(is_k, y * lax.rsqrt(ms + NORM_EPS) * gk_ref[...], y)
    o_ref[0] = y.astype(BF16)
    ot_ref[0] = y.T.astype(BF16)


def _compress(kvc3, pa, pb, w1a, w1b, w2, gk):
    batch, n, w = kvc3.shape
    return pl.pallas_call(
        _compress_kernel,
        grid=(batch,),
        in_specs=[pl.BlockSpec((1, n, w), lambda b: (b, 0, 0)), _const_spec((1, w)), _const_spec((1, w)),
                  _const_spec((w, LANE)), _const_spec((w, LANE)), _const_spec((LANE, LANE)), _const_spec((1, LANE))],
        out_specs=[pl.BlockSpec((1, n, LANE), lambda b: (b, 0, 0)), pl.BlockSpec((1, LANE, n), lambda b: (b, 0, 0))],
        out_shape=[jax.ShapeDtypeStruct((batch, n, LANE), BF16), jax.ShapeDtypeStruct((batch, LANE, n), BF16)],
        compiler_params=_params(("arbitrary",)),
        name="compress",
    )(kvc3, pa, pb, w1a, w1b, w2, gk)


def _nsa_kernel(qt_ref, kc_ref, kct_ref, kv_ref, kvt_ref, ngt_ref, bias_ref, far_ref, ovt_ref, o_ref):
    t = ATT_TILE
    nh = N_HEADS
    sel_k, sel_v = slice(0, LANE), slice(HEAD_DIM, 2 * HEAD_DIM)
    win_k, win_v = slice(LANE, LANE + HEAD_DIM), slice(LANE + HEAD_DIM, 2 * LANE)

    def step(q_tile):
        def near(width):
            return slice((q_tile + 1 - width) * t, (q_tile + 1) * t), bias_ref[(3 - width) * t:3 * t, :]

        mine = _tile_in_step(q_tile)

        def run():
            qt_all = qt_ref[:, mine]
            qs = jnp.concatenate([qt_all[h * HEAD_DIM:(h + 1) * HEAD_DIM, :] for h in range(nh)], axis=1)

            keys, bias = near(min(q_tile + 1, NSA_WINDOW // t + 1))
            win = [(_dot(kv_ref[keys, win_k], qs) + bias, kvt_ref[win_v, keys], None)]

            kc = kc_ref[0][:, 0:HEAD_DIM]
            vct = kct_ref[0][HEAD_DIM:2 * HEAD_DIM, :]
            n_cmp = kc.shape[0]
            pos = q_tile * t + (lax.broadcasted_iota(jnp.int32, (n_cmp, nh * t), 1) & (t - 1))
            cend = lax.broadcasted_iota(jnp.int32, (n_cmp, nh * t), 0) * NSA_CMP_STRIDE + (NSA_CMP_LEN - 1)
            vis = cend <= pos
            sc = jnp.where(vis, _dot(kc, qs), NEG_INF)
            e = jnp.where(vis, jnp.exp2(sc - jnp.max(sc, axis=0, keepdims=True)), 0.0)
            den = jnp.sum(e, axis=0, keepdims=True)
            p = e / jnp.where(den > 0.0, den, 1.0)
            o_cmp = _dot(vct, p.astype(BF16))

            ovt = ovt_ref[...]
            n_blk = ovt.shape[0]
            blk = lax.broadcasted_iota(jnp.int32, (n_blk, t), 0)
            cur = (q_tile * t + lax.broadcasted_iota(jnp.int32, (n_blk, t), 1)) // NSA_SEL_LEN
            keep = blk <= cur
            max_visible = min(n_blk, (q_tile + 1) * t // NSA_SEL_LEN)
            if max_visible > NSA_SEL_TOPN:
                p_sum = p[:, 0:t] + p[:, t:2 * t] + p[:, 2 * t:3 * t] + p[:, 3 * t:4 * t]
                p_hi = p_sum.astype(BF16)
                p_lo = (p_sum - p_hi.astype(F32)).astype(BF16)
                imp = _dot(ovt, p_hi) + _dot(ovt, p_lo)
                forced = (blk == 0) | (blk == cur) | (blk == cur - 1)
                imp = jnp.where(keep, imp + jnp.where(forced, NSA_FORCE_BONUS, 0.0), NEG_INF)
                keep = keep & (_rank_rows(imp, max_visible) < NSA_SEL_TOPN)
            pen = jnp.where(keep, 0.0, NEG_INF).astype(BF16)
            q_aug = jnp.concatenate([qs, jnp.concatenate([pen] * nh, axis=1),
                                     jnp.zeros((LANE - HEAD_DIM - n_blk, nh * t), BF16)], axis=0)

            keys, bias = near(min(q_tile + 1, 2))
            sel = [(_dot(kv_ref[keys, sel_k], q_aug) + bias, kvt_ref[sel_v, keys], None)]
            if keys.start > 0:
                far = slice(0, keys.start)
                sel.insert(0, (_dot(kv_ref[far, sel_k], q_aug), kvt_ref[sel_v, far], far_ref[0:1, :]))
            o_slc, o_win = _softmax_pv([sel, win])

            outs = []
            for h in range(nh):
                cs = slice(h * t, (h + 1) * t)
                g = [ngt_ref[_MISC_NGATE + 3 * h + i:_MISC_NGATE + 3 * h + i + 1, mine] for i in range(3)]
                outs.append(g[0] * o_cmp[:, cs] + g[1] * o_slc[:, cs] + g[2] * o_win[:, cs])
            o_ref[mine, :] = jnp.concatenate(outs, axis=0).T.astype(BF16)
        return run

    lax.switch(pl.program_id(1), _grid_steps(step, kv_ref.shape[0] // t))


def _nsa(nqt, kc, kct, nkv, nkvt, ngt, bias, far, ovt, batch, seq):
    t = ATT_TILE
    nq = seq // t
    assert NSA_WINDOW == 2 * t and seq // NSA_SEL_LEN <= LANE - HEAD_DIM
    n_cmp = kc.shape[1]
    n_blk = ovt.shape[0]
    steps = nq // ATT_TILES_PER_STEP
    qw = ATT_TILES_PER_STEP * t
    return pl.pallas_call(
        _nsa_kernel,
        grid=(batch, steps),
        in_specs=[pl.BlockSpec((256, qw), lambda b, i: (0, b * steps + i)),
                  pl.BlockSpec((1, n_cmp, LANE), lambda b, i: (b, 0, 0)),
                  pl.BlockSpec((1, LANE, n_cmp), lambda b, i: (b, 0, 0)),
                  pl.BlockSpec((seq, 256), lambda b, i: (b, 0)),
                  pl.BlockSpec((256, seq), lambda b, i: (0, b)),
                  pl.BlockSpec((LANE, qw), lambda b, i: (0, b * steps + i)),
                  _weight_spec((3 * t, N_HEADS * t)), _const_spec((8, N_HEADS * t)),
                  _const_spec((n_blk, n_cmp))],
        out_specs=pl.BlockSpec((qw, 256), lambda b, i: (b * steps + i, 0)),
        out_shape=jax.ShapeDtypeStruct((batch * seq, 256), BF16),
        compiler_params=_params(("arbitrary", "arbitrary")),
        name="nsa",
    )(nqt, kc, kct, nkv, nkvt, ngt, bias, far, ovt)


def _gla_kernel(qk_ref, v_ref, la_ref, go_ref, tri_ref, gn_ref, o_ref, st_ref):
    c = GLA_CHUNK
    nh = N_HEADS
    n_chunk = qk_ref.shape[0] // c
    st_ref[...] = jnp.zeros_like(st_ref)
    tri = tri_ref[...]
    row = lax.broadcasted_iota(jnp.int32, (c, nh * GLA_DK), 0)
    sub_causal = (lax.broadcasted_iota(jnp.int32, (GLA_SUB, c), 0)
                  - lax.broadcasted_iota(jnp.int32, (GLA_SUB, c), 1))

    hks = [slice(h * GLA_DK, (h + 1) * GLA_DK) for h in range(nh)]
    hvs = [slice(h * GLA_DV, (h + 1) * GLA_DV) for h in range(nh)]
    group_size = GLA_UNROLL

    def group(gi, _):
        rows = [pl.ds(pl.multiple_of((gi * group_size + u) * c, c), c) for u in range(group_size)]
        units = range(group_size)

        bs = []
        for u in units:
            g = la_ref[rows[u], :]
            g1 = g.astype(BF16)
            r1 = g - g1.astype(F32)
            g2 = r1.astype(BF16)
            g3 = (r1 - g2.astype(F32)).astype(BF16)
            bs.append(_dot(tri, g1) + _dot(tri, g2) + _dot(tri, g3))

        q_inter, k_state, decay, q_sub, k_sub, vs_ = [], [], [], [], [], []
        for u in units:
            b = bs[u]
            q = qk_ref[rows[u], 0:256].astype(F32)
            k = qk_ref[rows[u], 256:512].astype(F32)
            b_last = b[c - 1:c, :]
            q_inter.append((q * jnp.exp(b)).astype(BF16))
            k_state.append((k * jnp.exp(b_last - b)).astype(BF16))
            decay.append(jnp.exp(b_last))
            qs_u, ks_u = [], []
            for i in range(c // GLA_SUB):
                lo, hi = i * GLA_SUB, (i + 1) * GLA_SUB
                ref_b = b[lo:lo + 1, :]
                ks_u.append((k * jnp.exp(jnp.where(row < hi, ref_b - b, 0.0))).astype(BF16))
                qs_u.append((q[lo:hi] * jnp.exp(b[lo:hi] - ref_b)).astype(BF16))
            q_sub.append(qs_u)
            k_sub.append(ks_u)
            vs_.append(v_ref[rows[u], :])

        o_intra, kv = [], []
        for u in units:
            a_h = []
            for h in range(nh):
                blocks = [jnp.where(sub_causal + i * GLA_SUB >= 0,
                                    _dot_nt(q_sub[u][i][:, hks[h]], k_sub[u][i][:, hks[h]]), 0.0)
                          for i in range(c // GLA_SUB)]
                a_h.append(jnp.concatenate(blocks, axis=0).astype(BF16))
            o_intra.append([_dot(a_h[h], vs_[u][:, hvs[h]]) for h in range(nh)])
            kv.append([_dot_tn(vs_[u][:, hvs[h]], k_state[u][:, hks[h]]) for h in range(nh)])

        st = [st_ref[h] for h in range(nh)]
        for u in units:
            outs = []
            for h in range(nh):
                o = o_intra[u][h] + _dot_nt(q_inter[u][:, hks[h]], st[h].astype(BF16))
                st[h] = st[h] * decay[u][:, hks[h]] + kv[u][h]
                outs.append(_rms_rows(o, gn_ref[...]))
            o_ref[rows[u], :] = (jnp.concatenate(outs, axis=1) * go_ref[rows[u], :].astype(F32)).astype(BF16)
        for h in range(nh):
            st_ref[h] = st[h]
        return 0

    lax.fori_loop(0, n_chunk // group_size, group, 0)


def _gla(gqk, gv, la, go, tri, gn, batch, seq):
    spec = lambda w: pl.BlockSpec((seq, w), lambda b: (b, 0))
    return pl.pallas_call(
        _gla_kernel,
        grid=(batch,),
        in_specs=[spec(512), spec(512), spec(256), spec(512), _const_spec((GLA_CHUNK, GLA_CHUNK)),
                  _const_spec((1, GLA_DV))],
        out_specs=spec(512),
        out_shape=jax.ShapeDtypeStruct((batch * seq, 512), BF16),
        scratch_shapes=[pltpu.VMEM((N_HEADS, GLA_DV, GLA_DK), F32)],
        compiler_params=_params(("arbitrary",)),
        name="gla",
    )(gqk, gv, la, go, tri, gn)


def _merge_kernel(x_ref, an_ref, om_ref, on_ref, og_ref, wm_ref, pm_ref, pn_ref, pg_ref, wo_ref, o_ref):
    x = x_ref[...]
    h = _rms_rows(x, an_ref[...]).astype(BF16)
    gates = [_dot(h, wm_ref[:, i * D_MODEL:(i + 1) * D_MODEL]) for i in range(3)]
    branches = [_dot(o_b[...], p_b[...]) for o_b, p_b in ((om_ref, pm_ref), (on_ref, pn_ref), (og_ref, pg_ref))]
    gated = [jax.nn.sigmoid(g) * b for g, b in zip(gates, branches)]
    z = gated[0] + gated[1] + gated[2]
    o_ref[...] = x + _dot(z.astype(BF16), wo_ref[...])


def _merge(x, an, om, on, og, wm, pm, pn, pg, wo, tm=MERGE_ROWS):
    n = x.shape[0]
    row = lambda w: pl.BlockSpec((tm, w), lambda i: (i, 0))
    return pl.pallas_call(
        _merge_kernel,
        grid=(n // tm,),
        in_specs=[row(D_MODEL), _const_spec((1, D_MODEL)), row(256), row(256), row(512),
                  _weight_spec((D_MODEL, 3 * D_MODEL)), _weight_spec((256, D_MODEL)), _weight_spec((256, D_MODEL)),
                  _weight_spec((512, D_MODEL)), _weight_spec((D_MODEL, D_MODEL))],
        out_specs=row(D_MODEL),
        out_shape=jax.ShapeDtypeStruct((n, D_MODEL), F32),
        compiler_params=_params(("arbitrary",)),
        name="merge",
    )(x, an, om, on, og, wm, pm, pn, pg, wo)


FFN_CHUNK = 256


def _ffn_kernel(x_ref, fn_ref, wa_ref, wg_ref, cw_ref, cb_ref, wd_ref, o_ref, carry_ref, act_ref, *, tiles_per_seq):
    i = pl.program_id(0)
    tm = x_ref.shape[0]
    x = x_ref[...]
    h = _rms_rows(x, fn_ref[...]).astype(BF16)
    row = lax.broadcasted_iota(jnp.int32, (tm, FFN_CHUNK), 0)

    @pl.when((i % tiles_per_seq) == 0)
    def _():
        carry_ref[...] = jnp.zeros_like(carry_ref)

    for c in range(D_FF // FFN_CHUNK):
        cs = slice(c * FFN_CHUNK, (c + 1) * FFN_CHUNK)
        a = _dot(h, wa_ref[:, cs])
        g = _dot(h, wg_ref[:, cs])
        prev = carry_ref[:, cs]
        p1 = prev[7:8, :]
        p2 = prev[6:7, :]
        a1 = jnp.where(row == 0, p1, pltpu.roll(a, 1, axis=0))
        a2 = jnp.where(row == 0, p2, jnp.where(row == 1, p1, pltpu.roll(a, 2, axis=0)))
        carry_ref[:, cs] = a[tm - 8:tm, :]
        w = cw_ref[:, cs]
        conv = w[0:1, :] * a2 + w[1:2, :] * a1 + w[2:3, :] * a + cb_ref[:, cs]
        act_ref[:, cs] = (jax.nn.gelu(conv, approximate=True) * g).astype(BF16)
    o_ref[...] = x + _dot(act_ref[...], wd_ref[...])


def _ffn(x, fn, wa, wg, cw, cb, wd, seq, tm=FFN_ROWS):
    n = x.shape[0]
    row = pl.BlockSpec((tm, D_MODEL), lambda i: (i, 0))
    return pl.pallas_call(
        functools.partial(_ffn_kernel, tiles_per_seq=seq // tm),
        grid=(n // tm,),
        in_specs=[row, _const_spec((1, D_MODEL)), _weight_spec((D_MODEL, D_FF)), _weight_spec((D_MODEL, D_FF)),
                  _const_spec((8, D_FF)), _const_spec((1, D_FF)), _weight_spec((D_FF, D_MODEL))],
        out_specs=row,
        out_shape=jax.ShapeDtypeStruct((n, D_MODEL), F32),
        scratch_shapes=[pltpu.VMEM((8, D_FF), F32), pltpu.VMEM((tm, D_FF), BF16)],
        compiler_params=_params(("arbitrary",)),
        name="ffn",
    )(x, fn, wa, wg, cw, cb, wd)


def _rel_bucket(dist):
    n = jnp.maximum(dist, 0)
    max_exact = REL_BUCKETS // 2
    nf = jnp.maximum(n, 1).astype(F32)
    large = max_exact + (jnp.log(nf / max_exact) / math.log(REL_MAX_DIST / max_exact)
                         * (REL_BUCKETS - max_exact)).astype(jnp.int32)
    return jnp.where(n < max_exact, n, jnp.minimum(large, REL_BUCKETS - 1))


def _bias_tiles(rel_tab):
    t = ATT_TILE
    d0 = jnp.arange(t)[None, :] - jnp.arange(t)[:, None]
    bucket = jnp.stack([_rel_bucket(d0 + k * t) for k in range(3)])
    out = jnp.zeros((rel_tab.shape[1],) + bucket.shape, F32)
    for b in range(REL_BUCKETS):
        out = jnp.where(bucket[None] == b, rel_tab[b][:, None, None, None], out)
    return out


def _block_diag_mean():
    g = jnp.arange(256) // HEAD_DIM
    return jnp.where(g[:, None] == g[None, :], 1.0 / HEAD_DIM, 0.0).astype(BF16)


def _overlap_t(n_cmp_pad, n_cmp, n_blk):
    tok = jnp.arange(n_blk * NSA_SEL_LEN)
    starts = jnp.arange(n_cmp_pad) * NSA_CMP_STRIDE
    inside = (tok[None, :] >= starts[:, None]) & (tok[None, :] < starts[:, None] + NSA_CMP_LEN)
    m = inside.reshape(n_cmp_pad, n_blk, NSA_SEL_LEN).sum(-1).astype(F32) / NSA_CMP_LEN
    m = jnp.where(jnp.arange(n_cmp_pad)[:, None] < n_cmp, m, 0.0)
    return m.T.astype(BF16)


def _tile_gain(g, reps):
    return jnp.tile(g.astype(F32), reps)[None, :]


def kernel(x, rel_bias, attn_norm, w_in, moba_q_norm, moba_k_norm, nsa_q_norm, nsa_k_norm, cmp_pos_k, cmp_pos_v,
           cmp_k_w1, cmp_k_w2, cmp_v_w1, cmp_v_w2, gla_gate_w, gla_gate_b, gla_out_norm, w_branch_moba,
           w_branch_nsa, w_branch_gla, w_out, ffn_norm, w_up, conv_w, conv_b, w_down):
    batch, seq, _ = x.shape
    depth = w_in.shape[0]
    assert x.shape[2] == D_MODEL and w_in.shape[1:] == (D_MODEL, _W_END)
    assert MOBA_BLOCK == ATT_TILE and seq // MOBA_BLOCK <= 16
    assert seq % (ATT_TILES_PER_STEP * ATT_TILE) == 0 and seq % FFN_ROWS == 0
    assert (batch * seq) % PROJ_ROWS == 0 and (batch * seq) % MERGE_ROWS == 0
    assert (seq // GLA_CHUNK) % GLA_UNROLL == 0
    n_cmp = seq // NSA_CMP_STRIDE - NSA_CMP_LEN // NSA_CMP_STRIDE + 1
    n_cmp_pad = seq // NSA_CMP_STRIDE
    n_blk = seq // NSA_SEL_LEN

    tiles = _bias_tiles(rel_bias.astype(F32)) * LOG2_E
    key_le_query = jnp.arange(ATT_TILE)[:, None] <= jnp.arange(ATT_TILE)[None, :]
    own = jnp.where(key_le_query, tiles[:, 0], NEG_INF)
    band = jnp.where(key_le_query, NEG_INF, tiles[:, 2])
    near, far = tiles[:, 1], tiles[:, 2, :8]
    bias_moba = jnp.concatenate([near[:N_HEADS], own[:N_HEADS]], axis=1)
    far_moba = far[:N_HEADS]
    heads_on_lanes = lambda a: jnp.concatenate(list(a[N_HEADS:]), axis=1)
    bias_nsa = jnp.concatenate([heads_on_lanes(band), heads_on_lanes(near), heads_on_lanes(own)], axis=0)
    far_nsa = heads_on_lanes(far)
    bd = _block_diag_mean()
    ovt = _overlap_t(n_cmp_pad, n_cmp, n_blk)
    tri = (jnp.arange(GLA_CHUNK)[:, None] >= jnp.arange(GLA_CHUNK)[None, :]).astype(BF16)
    ones64 = jnp.ones((HEAD_DIM,), F32)

    xf = x.reshape(batch * seq, D_MODEL)
    for l in range(depth):
        w_att, w_gla, w_go, w_glr, w_ng, wmerge = _wsplit(w_in, l)
        w_misc = jnp.pad(jnp.concatenate([w_glr, w_ng], axis=1), ((0, 0), (0, LANE - GLA_GATE_RANK - 3 * N_HEADS)))
        gw = jnp.pad(gla_gate_w[l], ((0, LANE - GLA_GATE_RANK), (0, 0))).astype(BF16)

        mqt, mk_a, mvt, nqt, kvc, nkv_a, nkvt, gqk, gv_a, la, go, ngt = _proj(
            xf, attn_norm[l][None, :], w_att, w_gla, w_go, w_misc, bd,
            _tile_gain(moba_q_norm[l], 4), _tile_gain(moba_k_norm[l], 4), _tile_gain(nsa_q_norm[l], 4),
            jnp.concatenate([nsa_k_norm[l, 1], ones64])[None, :], jnp.concatenate([nsa_k_norm[l, 2], ones64])[None, :],
            gw, gla_gate_b[l][None, :], seq)

        o_moba = _moba(mqt, mk_a, mvt, bias_moba, far_moba, batch, seq)

        half = NSA_CMP_STRIDE * HEAD_DIM
        zero = jnp.zeros((NSA_CMP_STRIDE, HEAD_DIM, HEAD_DIM), F32)

        def w1_part(part):
            wk = cmp_k_w1[l][part * half:(part + 1) * half].reshape(NSA_CMP_STRIDE, HEAD_DIM, HEAD_DIM)
            wv = cmp_v_w1[l][part * half:(part + 1) * half].reshape(NSA_CMP_STRIDE, HEAD_DIM, HEAD_DIM)
            top = jnp.concatenate([wk, zero], axis=2)
            bot = jnp.concatenate([zero, wv], axis=2)
            return jnp.concatenate([top, bot], axis=1).reshape(NSA_CMP_STRIDE * LANE, LANE).astype(BF16)

        def pos_part(part):
            pk = cmp_pos_k[l][part * NSA_CMP_STRIDE:(part + 1) * NSA_CMP_STRIDE]
            pv = cmp_pos_v[l][part * NSA_CMP_STRIDE:(part + 1) * NSA_CMP_STRIDE]
            return jnp.concatenate([pk, pv], axis=1).reshape(1, NSA_CMP_STRIDE * LANE).astype(F32)

        z64 = jnp.zeros((HEAD_DIM, HEAD_DIM), F32)
        w2 = jnp.concatenate([jnp.concatenate([cmp_k_w2[l], z64], axis=1),
                              jnp.concatenate([z64, cmp_v_w2[l]], axis=1)], axis=0).astype(BF16)
        kcv, kcvt = _compress(kvc.reshape(batch, n_cmp_pad, NSA_CMP_STRIDE * LANE), pos_part(0), pos_part(1),
                              w1_part(0), w1_part(1), w2, jnp.concatenate([nsa_k_norm[l, 0], ones64])[None, :])

        o_nsa = _nsa(nqt, kcv, kcvt, nkv_a, nkvt, ngt, bias_nsa, far_nsa, ovt, batch, seq)
        o_gla = _gla(gqk, gv_a, la, go, tri, gla_out_norm[l][None, :].astype(F32), batch, seq)

        xf = _merge(xf, attn_norm[l][None, :], o_moba, o_nsa, o_gla, wmerge.astype(BF16),
                    w_branch_moba[l].astype(BF16), w_branch_nsa[l].astype(BF16), w_branch_gla[l].astype(BF16),
                    w_out[l].astype(BF16))

        cw = jnp.pad(conv_w[l], ((0, 8 - conv_w.shape[1]), (0, 0)))
        xf = _ffn(xf, ffn_norm[l][None, :], w_up[l][:, :D_FF].astype(BF16), w_up[l][:, D_FF:].astype(BF16),
                  cw, conv_b[l][None, :], w_down[l].astype(BF16), seq)
    return xf.reshape(batch, seq, D_MODEL)
```

```python
import functools
import math

import jax
import jax.numpy as jnp
from jax import lax
from jax.experimental import pallas as pl
from jax.experimental.pallas import tpu as pltpu

F32 = jnp.float32
BF16 = jnp.bfloat16

D_MODEL = 1024
HEAD_DIM = 64
N_HEADS = 4
MOBA_BLOCK = 256
MOBA_TOPK = 3
NSA_CMP_LEN = 32
NSA_CMP_STRIDE = 16
NSA_SEL_LEN = 64
NSA_SEL_TOPN = 16
NSA_WINDOW = 512
NSA_FORCE_BONUS = 1e4
GLA_DK = 64
GLA_DV = 128
GLA_GATE_RANK = 16
GLA_GATE_NORM = 16.0
GLA_CHUNK = 64
GLA_SUB = 16
GLA_UNROLL = 8
D_FF = 2816
REL_BUCKETS = 32
REL_MAX_DIST = 128
NORM_EPS = 1e-6
NEG_INF = -1e30
LOG2_E = math.log2(math.e)

LANE = 128
V7X_VMEM_BYTES = 64 * 1024 * 1024
VMEM_LIMIT = V7X_VMEM_BYTES - 8 * 1024 * 1024

PROJ_ROWS = 512
MERGE_ROWS = 512
FFN_ROWS = 1024
WSPLIT_ROWS = 256
ATT_TILE = 256

_C_MQ, _C_MK, _C_MV, _C_NQ = 0, 256, 512, 768
_C_KVC, _C_KSVS, _C_KWVW, _C_A_END = 1024, 1152, 1280, 1408
_MISC_NGATE = GLA_GATE_RANK


def _dot(a, b):
    return jnp.dot(a, b, preferred_element_type=F32)


def _dot_nt(a, b):
    return lax.dot_general(a, b, (((1,), (1,)), ((), ())), preferred_element_type=F32)


def _dot_tn(a, b):
    return lax.dot_general(a, b, (((0,), (0,)), ((), ())), preferred_element_type=F32)


def _rms_rows(x, gain):
    ms = jnp.mean(x * x, axis=-1, keepdims=True)
    return x * lax.rsqrt(ms + NORM_EPS) * gain


def _params(sem):
    return pltpu.CompilerParams(dimension_semantics=sem, vmem_limit_bytes=VMEM_LIMIT)


def _const_spec(shape):
    return pl.BlockSpec(shape, lambda *_: (0,) * len(shape))


def _weight_spec(shape):
    return pl.BlockSpec(shape, lambda *_: (0,) * len(shape), pipeline_mode=pl.Buffered(1))


_W_NG, _W_GLA, _W_GLR, _W_GO, _W_MERGE, _W_END = 1408, 1420, 2444, 2460, 2972, 6044


def _wsplit_kernel(w_ref, att_ref, gla_ref, go_ref, glr_ref, ng_ref, merge_ref):
    w = w_ref[0]
    att_ref[...] = w[:, 0:_W_NG].astype(BF16)
    ng_ref[...] = w[:, _W_NG:_W_GLA].astype(BF16)
    gla_ref[...] = w[:, _W_GLA:_W_GLR].astype(BF16)
    glr_ref[...] = w[:, _W_GLR:_W_GO].astype(BF16)
    go_ref[...] = w[:, _W_GO:_W_MERGE].astype(BF16)
    merge_ref[...] = w[:, _W_MERGE:_W_END].astype(BF16)


def _wsplit(w_in, layer, tr=WSPLIT_ROWS):
    widths = [_W_NG, _W_GLR - _W_GLA, _W_MERGE - _W_GO, _W_GO - _W_GLR, _W_GLA - _W_NG, _W_END - _W_MERGE]
    return pl.pallas_call(
        _wsplit_kernel,
        grid=(D_MODEL // tr,),
        in_specs=[pl.BlockSpec((1, tr, _W_END), lambda i: (layer, i, 0))],
        out_specs=[pl.BlockSpec((tr, w), lambda i: (i, 0)) for w in widths],
        out_shape=[jax.ShapeDtypeStruct((D_MODEL, w), BF16) for w in widths],
        compiler_params=_params(("arbitrary",)),
        name="wsplit",
    )(w_in)


def _proj_kernel(x_ref, an_ref, wa_ref, wg_ref, wo_ref, wm_ref, bd_ref, gmq_ref, gmk_ref, gnq_ref, gks_ref, gkw_ref,
                 gw_ref, gb_ref,
                 mqt_ref, mk_ref, mvt_ref, nqt_ref, kvc_ref, nkv_ref, nkvt_ref, gqk_ref, gv_ref, la_ref, go_ref,
                 ngt_ref, *, seq):
    tm = x_ref.shape[0]
    h = _rms_rows(x_ref[...], an_ref[...]).astype(BF16)
    pos = (pl.program_id(0) * tm + lax.broadcasted_iota(jnp.int32, (tm, LANE), 0)) % seq
    lane = lax.broadcasted_iota(jnp.int32, (tm, LANE), 1)

    y_all = {id(w_ref): _dot(h, w_ref[...]) for w_ref in (wa_ref, wg_ref, wo_ref, wm_ref)}

    def sec(w_ref, lo, hi):
        return y_all[id(w_ref)][:, lo:hi]

    def head_norm(y, gain):
        w = y.shape[1]
        ms = _dot((y * y).astype(BF16), bd_ref[0:w, 0:w])
        return y * lax.rsqrt(ms + NORM_EPS) * gain

    scale = HEAD_DIM ** -0.5 * LOG2_E
    mqt_ref[...] = (head_norm(sec(wa_ref, _C_MQ, _C_MK), gmq_ref[...]) * scale).T.astype(BF16)
    mk = head_norm(sec(wa_ref, _C_MK, _C_MV), gmk_ref[...])
    pos_h = (pl.program_id(0) * tm + lax.broadcasted_iota(jnp.int32, (tm, HEAD_DIM), 0)) % seq
    lane_h = lax.broadcasted_iota(jnp.int32, (tm, HEAD_DIM), 1)
    moba_onehot = jnp.where(lane_h == pos_h // MOBA_BLOCK, 1.0, 0.0)
    for hd in range(N_HEADS):
        mk_ref[:, hd * LANE:(hd + 1) * LANE] = jnp.concatenate(
            [mk[:, hd * HEAD_DIM:(hd + 1) * HEAD_DIM], moba_onehot], axis=1).astype(BF16)
    mvt_ref[...] = sec(wa_ref, _C_MV, _C_NQ).T.astype(BF16)
    nqt_ref[...] = (head_norm(sec(wa_ref, _C_NQ, _C_KVC), gnq_ref[...]) * scale).T.astype(BF16)
    kvc_ref[...] = sec(wa_ref, _C_KVC, _C_KSVS).astype(BF16)
    first_half = lane < HEAD_DIM
    ksvs = sec(wa_ref, _C_KSVS, _C_KWVW)
    ksvs = jnp.where(first_half, head_norm(ksvs, gks_ref[...]), ksvs)
    nkvt_ref[0:128, :] = ksvs.T.astype(BF16)
    sel_onehot = jnp.where(lane - HEAD_DIM == pos // NSA_SEL_LEN, 1.0, 0.0)
    nkv_ref[:, 0:128] = jnp.where(first_half, ksvs, sel_onehot).astype(BF16)
    kwvw = sec(wa_ref, _C_KWVW, _C_A_END)
    kwvw = jnp.where(first_half, head_norm(kwvw, gkw_ref[...]), kwvw)
    nkv_ref[:, 128:256] = kwvw.astype(BF16)
    nkvt_ref[128:256, :] = kwvw.T.astype(BF16)
    gqk_ref[:, 0:256] = (sec(wg_ref, 0, 256) * (GLA_DK ** -0.5)).astype(BF16)
    gqk_ref[:, 256:512] = sec(wg_ref, 256, 512).astype(BF16)
    gv_ref[...] = sec(wg_ref, 512, 1024).astype(BF16)
    go_ref[...] = jax.nn.silu(sec(wo_ref, 0, 512)).astype(BF16)
    misc = sec(wm_ref, 0, LANE)
    ngt_ref[...] = jax.nn.sigmoid(misc).T
    pre = _dot(misc.astype(BF16), gw_ref[...]) + gb_ref[...]
    la_ref[...] = (jnp.minimum(pre, 0.0) - jnp.log(1.0 + jnp.exp(-jnp.abs(pre)))) * (1.0 / GLA_GATE_NORM)


def _proj(x, an, w_att, w_gla, w_go, w_misc, bd, gmq, gmk, gnq, gks, gkw, gw, gb, seq, tm=PROJ_ROWS):
    n = x.shape[0]
    row = lambda w: pl.BlockSpec((tm, w), lambda i: (i, 0))
    col = lambda w: pl.BlockSpec((w, tm), lambda i: (0, i))
    outs = [(256, BF16, True), (N_HEADS * LANE, BF16, False), (256, BF16, True), (256, BF16, True),
            (128, BF16, False), (256, BF16, False), (256, BF16, True), (512, BF16, False), (512, BF16, False),
            (256, F32, False), (512, BF16, False), (128, F32, True)]
    return pl.pallas_call(
        functools.partial(_proj_kernel, seq=seq),
        grid=(n // tm,),
        in_specs=[row(D_MODEL), _const_spec((1, D_MODEL)), _weight_spec(w_att.shape), _weight_spec(w_gla.shape),
                  _weight_spec(w_go.shape), _weight_spec(w_misc.shape), _const_spec((256, 256)),
                  _const_spec((1, 256)), _const_spec((1, 256)), _const_spec((1, 256)), _const_spec((1, 128)),
                  _const_spec((1, 128)), _const_spec((LANE, 256)), _const_spec((1, 256))],
        out_specs=[col(w) if tr else row(w) for w, _, tr in outs],
        out_shape=[jax.ShapeDtypeStruct((w, n) if tr else (n, w), dt) for w, dt, tr in outs],
        compiler_params=_params(("arbitrary",)),
        name="proj",
    )(x, an, w_att, w_gla, w_go, w_misc, bd, gmq, gmk, gnq, gks, gkw, gw, gb)


ATT_TILES_PER_STEP = 8


def _tile_in_step(q_tile):
    sub = q_tile % ATT_TILES_PER_STEP
    return slice(sub * ATT_TILE, (sub + 1) * ATT_TILE)


def _grid_steps(step, n_tiles):
    def group(first):
        def run():
            for q_tile in range(first, first + ATT_TILES_PER_STEP):
                step(q_tile)()
        return run
    return [group(first) for first in range(0, n_tiles, ATT_TILES_PER_STEP)]


def _softmax_pv(chains):
    def shifted(x, shift, sign):
        return x if shift is None else x + sign * shift

    maxes = [functools.reduce(jnp.maximum, [shifted(jnp.max(s, axis=0, keepdims=True), shift, 1)
                                            for s, _, shift in chain]) for chain in chains]
    probs = [[jnp.exp2(s - shifted(m, shift, -1)) for s, _, shift in chain] for chain, m in zip(chains, maxes)]
    sums = [functools.reduce(jnp.add, [jnp.sum(p, axis=0, keepdims=True) for p in ps]) for ps in probs]
    accs = [functools.reduce(jnp.add, [_dot(vt, p.astype(BF16)) for (_, vt, _), p in zip(chain, ps)])
            for chain, ps in zip(chains, probs)]
    return [acc / l for acc, l in zip(accs, sums)]


def _rank_rows(score, n):
    idx = lax.broadcasted_iota(jnp.int32, score.shape, 0)
    rank = jnp.zeros(score.shape, F32)
    for m in range(n):
        sm = score[m:m + 1, :]
        rank += jnp.where((sm > score) | ((sm == score) & (idx > m)), 1.0, 0.0)
    return rank


def _moba_kernel(qt_ref, k_ref, vt_ref, bias_ref, far_ref, o_ref, kmean_ref):
    t = ATT_TILE
    nh = N_HEADS
    nb = k_ref.shape[0] // t
    kcols = [slice(h * LANE, (h + 1) * LANE) for h in range(nh)]
    vrows = [slice(h * HEAD_DIM, (h + 1) * HEAD_DIM) for h in range(nh)]
    pen_rows = 16

    def step(q_tile):
        width = min(q_tile + 1, 2)
        near = slice((q_tile + 1 - width) * t, (q_tile + 1) * t)
        far = slice(0, (q_tile + 1 - width) * t)
        mine = _tile_in_step(q_tile)

        def run():
            if q_tile == 0:
                for h in range(nh):
                    kh = k_ref[:, h * LANE:h * LANE + HEAD_DIM].astype(F32)
                    kmean_ref[h] = jnp.mean(kh.reshape(nb, t, HEAD_DIM), axis=1)

            chains = []
            for h in range(nh):
                qt = qt_ref[vrows[h], mine]
                if q_tile > MOBA_TOPK:
                    past = lax.broadcasted_iota(jnp.int32, (nb, t), 0) < q_tile
                    gate = jnp.where(past, _dot(kmean_ref[h].astype(BF16), qt), NEG_INF)
                    pen = jnp.where(past & (_rank_rows(gate, q_tile) >= MOBA_TOPK), NEG_INF, 0.0)
                    pen = jnp.concatenate([pen, jnp.zeros((pen_rows - nb, t), F32)], axis=0).astype(BF16)
                else:
                    pen = jnp.zeros((pen_rows, t), BF16)
                q_aug = jnp.concatenate([qt, pen, jnp.zeros((LANE - HEAD_DIM - pen_rows, t), BF16)], axis=0)
                chain = []
                if far.stop > 0:
                    chain.append((_dot(k_ref[far, kcols[h]], q_aug), vt_ref[vrows[h], far], far_ref[h, 0:1, :]))
                chain.append((_dot(k_ref[near, kcols[h]], q_aug) + bias_ref[h, (2 - width) * t:2 * t, :],
                              vt_ref[vrows[h], near], None))
                chains.append(chain)
            o_ref[mine, :] = jnp.concatenate(_softmax_pv(chains), axis=0).T.astype(BF16)
        return run

    lax.switch(pl.program_id(1), _grid_steps(step, nb))


def _moba(mqt, mk, mvt, bias, far, batch, seq):
    t = ATT_TILE
    nq = seq // t
    steps = nq // ATT_TILES_PER_STEP
    qw = ATT_TILES_PER_STEP * t
    return pl.pallas_call(
        _moba_kernel,
        grid=(batch, steps),
        in_specs=[pl.BlockSpec((256, qw), lambda b, i: (0, b * steps + i)),
                  pl.BlockSpec((seq, N_HEADS * LANE), lambda b, i: (b, 0)),
                  pl.BlockSpec((256, seq), lambda b, i: (0, b)),
                  _const_spec((N_HEADS, 2 * t, t)), _const_spec((N_HEADS, 8, t))],
        out_specs=pl.BlockSpec((qw, 256), lambda b, i: (b * steps + i, 0)),
        out_shape=jax.ShapeDtypeStruct((batch * seq, 256), BF16),
        scratch_shapes=[pltpu.VMEM((N_HEADS, nq, HEAD_DIM), F32)],
        compiler_params=_params(("arbitrary", "arbitrary")),
        name="moba",
    )(mqt, mk, mvt, bias, far)


def _compress_kernel(x_ref, pa_ref, pb_ref, w1a_ref, w1b_ref, w2_ref, gk_ref, o_ref, ot_ref):
    x = x_ref[0].astype(F32)
    u = _dot((x + pa_ref[...]).astype(BF16), w1a_ref[...])
    v = _dot((x + pb_ref[...]).astype(BF16), w1b_ref[...])
    n = u.shape[0]
    hid = u + pltpu.roll(v, n - 1, axis=0)
    y = _dot(jax.nn.gelu(hid, approximate=True).astype(BF16), w2_ref[...])
    is_k = lax.broadcasted_iota(jnp.int32, y.shape, 1) < HEAD_DIM
    ms = jnp.sum(jnp.where(is_k, y * y, 0.0), axis=1, keepdims=True) * (1.0 / HEAD_DIM)
    y = jnp.where(is_k, y * lax.rsqrt(ms + NORM_EPS) * gk_ref[...], y)
    o_ref[0] = y.astype(BF16)
    ot_ref[0] = y.T.astype(BF16)


def _compress(kvc3, pa, pb, w1a, w1b, w2, gk):
    batch, n, w = kvc3.shape
    return pl.pallas_call(
        _compress_kernel,
        grid=(batch,),
        in_specs=[pl.BlockSpec((1, n, w), lambda b: (b, 0, 0)), _const_spec((1, w)), _const_spec((1, w)),
                  _const_spec((w, LANE)), _const_spec((w, LANE)), _const_spec((LANE, LANE)), _const_spec((1, LANE))],
        out_specs=[pl.BlockSpec((1, n, LANE), lambda b: (b, 0, 0)), pl.BlockSpec((1, LANE, n), lambda b: (b, 0, 0))],
        out_shape=[jax.ShapeDtypeStruct((batch, n, LANE), BF16), jax.ShapeDtypeStruct((batch, LANE, n), BF16)],
        compiler_params=_params(("arbitrary",)),
        name="compress",
    )(kvc3, pa, pb, w1a, w1b, w2, gk)


def _nsa_kernel(qt_ref, kc_ref, kct_ref, kv_ref, kvt_ref, ngt_ref, bias_ref, far_ref, ovt_ref, o_ref):
    t = ATT_TILE
    nh = N_HEADS
    sel_k, sel_v = slice(0, LANE), slice(HEAD_DIM, 2 * HEAD_DIM)
    win_k, win_v = slice(LANE, LANE + HEAD_DIM), slice(LANE + HEAD_DIM, 2 * LANE)

    def step(q_tile):
        def near(width):
            return slice((q_tile + 1 - width) * t, (q_tile + 1) * t), bias_ref[(3 - width) * t:3 * t, :]

        mine = _tile_in_step(q_tile)

        def run():
            qt_all = qt_ref[:, mine]
            qs = jnp.concatenate([qt_all[h * HEAD_DIM:(h + 1) * HEAD_DIM, :] for h in range(nh)], axis=1)

            keys, bias = near(min(q_tile + 1, NSA_WINDOW // t + 1))
            win = [(_dot(kv_ref[keys, win_k], qs) + bias, kvt_ref[win_v, keys], None)]

            kc = kc_ref[0][:, 0:HEAD_DIM]
            vct = kct_ref[0][HEAD_DIM:2 * HEAD_DIM, :]
            n_cmp = kc.shape[0]
            pos = q_tile * t + (lax.broadcasted_iota(jnp.int32, (n_cmp, nh * t), 1) & (t - 1))
            cend = lax.broadcasted_iota(jnp.int32, (n_cmp, nh * t), 0) * NSA_CMP_STRIDE + (NSA_CMP_LEN - 1)
            vis = cend <= pos
            sc = jnp.where(vis, _dot(kc, qs), NEG_INF)
            e = jnp.where(vis, jnp.exp2(sc - jnp.max(sc, axis=0, keepdims=True)), 0.0)
            den = jnp.sum(e, axis=0, keepdims=True)
            p = e / jnp.where(den > 0.0, den, 1.0)
            o_cmp = _dot(vct, p.astype(BF16))

            ovt = ovt_ref[...]
            n_blk = ovt.shape[0]
            blk = lax.broadcasted_iota(jnp.int32, (n_blk, t), 0)
            cur = (q_tile * t + lax.broadcasted_iota(jnp.int32, (n_blk, t), 1)) // NSA_SEL_LEN
            keep = blk <= cur
            max_visible = min(n_blk, (q_tile + 1) * t // NSA_SEL_LEN)
            if max_visible > NSA_SEL_TOPN:
                p_sum = p[:, 0:t] + p[:, t:2 * t] + p[:, 2 * t:3 * t] + p[:, 3 * t:4 * t]
                p_hi = p_sum.astype(BF16)
                p_lo = (p_sum - p_hi.astype(F32)).astype(BF16)
                imp = _dot(ovt, p_hi) + _dot(ovt, p_lo)
                forced = (blk == 0) | (blk == cur) | (blk == cur - 1)
                imp = jnp.where(keep, imp + jnp.where(forced, NSA_FORCE_BONUS, 0.0), NEG_INF)
                keep = keep & (_rank_rows(imp, max_visible) < NSA_SEL_TOPN)
            pen = jnp.where(keep, 0.0, NEG_INF).astype(BF16)
            q_aug = jnp.concatenate([qs, jnp.concatenate([pen] * nh, axis=1),
                                     jnp.zeros((LANE - HEAD_DIM - n_blk, nh * t), BF16)], axis=0)

            keys, bias = near(min(q_tile + 1, 2))
            sel = [(_dot(kv_ref[keys, sel_k], q_aug) + bias, kvt_ref[sel_v, keys], None)]
            if keys.start > 0:
                far = slice(0, keys.start)
                sel.insert(0, (_dot(kv_ref[far, sel_k], q_aug), kvt_ref[sel_v, far], far_ref[0:1, :]))
            o_slc, o_win = _softmax_pv([sel, win])

            outs = []
            for h in range(nh):
                cs = slice(h * t, (h + 1) * t)
                g = [ngt_ref[_MISC_NGATE + 3 * h + i:_MISC_NGATE + 3 * h + i + 1, mine] for i in range(3)]
                outs.append(g[0] * o_cmp[:, cs] + g[1] * o_slc[:, cs] + g[2] * o_win[:, cs])
            o_ref[mine, :] = jnp.concatenate(outs, axis=0).T.astype(BF16)
        return run

    lax.switch(pl.program_id(1), _grid_steps(step, kv_ref.shape[0] // t))


def _nsa(nqt, kc, kct, nkv, nkvt, ngt, bias, far, ovt, batch, seq):
    t = ATT_TILE
    nq = seq // t
    assert NSA_WINDOW == 2 * t and seq // NSA_SEL_LEN <= LANE - HEAD_DIM
    n_cmp = kc.shape[1]
    n_blk = ovt.shape[0]
    steps = nq // ATT_TILES_PER_STEP
    qw = ATT_TILES_PER_STEP * t
    return pl.pallas_call(
        _nsa_kernel,
        grid=(batch, steps),
        in_specs=[pl.BlockSpec((256, qw), lambda b, i: (0, b * steps + i)),
                  pl.BlockSpec((1, n_cmp, LANE), lambda b, i: (b, 0, 0)),
                  pl.BlockSpec((1, LANE, n_cmp), lambda b, i: (b, 0, 0)),
                  pl.BlockSpec((seq, 256), lambda b, i: (b, 0)),
                  pl.BlockSpec((256, seq), lambda b, i: (0, b)),
                  pl.BlockSpec((LANE, qw), lambda b, i: (0, b * steps + i)),
                  _weight_spec((3 * t, N_HEADS * t)), _const_spec((8, N_HEADS * t)),
                  _const_spec((n_blk, n_cmp))],
        out_specs=pl.BlockSpec((qw, 256), lambda b, i: (b * steps + i, 0)),
        out_shape=jax.ShapeDtypeStruct((batch * seq, 256), BF16),
        compiler_params=_params(("arbitrary", "arbitrary")),
        name="nsa",
    )(nqt, kc, kct, nkv, nkvt, ngt, bias, far, ovt)


def _gla_kernel(qk_ref, v_ref, la_ref, go_ref, tri_ref, gn_ref, o_ref, st_ref):
    c = GLA_CHUNK
    nh = N_HEADS
    n_chunk = qk_ref.shape[0] // c
    st_ref[...] = jnp.zeros_like(st_ref)
    tri = tri_ref[...]
    row = lax.broadcasted_iota(jnp.int32, (c, nh * GLA_DK), 0)
    sub_causal = (lax.broadcasted_iota(jnp.int32, (GLA_SUB, c), 0)
                  - lax.broadcasted_iota(jnp.int32, (GLA_SUB, c), 1))

    hks = [slice(h * GLA_DK, (h + 1) * GLA_DK) for h in range(nh)]
    hvs = [slice(h * GLA_DV, (h + 1) * GLA_DV) for h in range(nh)]
    group_size = GLA_UNROLL

    def group(gi, _):
        rows = [pl.ds(pl.multiple_of((gi * group_size + u) * c, c), c) for u in range(group_size)]
        units = range(group_size)

        bs = []
        for u in units:
            g = la_ref[rows[u], :]
            g1 = g.astype(BF16)
            r1 = g - g1.astype(F32)
            g2 = r1.astype(BF16)
            g3 = (r1 - g2.astype(F32)).astype(BF16)
            bs.append(_dot(tri, g1) + _dot(tri, g2) + _dot(tri, g3))

        q_inter, k_state, decay, q_sub, k_sub, vs_ = [], [], [], [], [], []
        for u in units:
            b = bs[u]
            q = qk_ref[rows[u], 0:256].astype(F32)
            k = qk_ref[rows[u], 256:512].astype(F32)
            b_last = b[c - 1:c, :]
            q_inter.append((q * jnp.exp(b)).astype(BF16))
            k_state.append((k * jnp.exp(b_last - b)).astype(BF16))
            decay.append(jnp.exp(b_last))
            qs_u, ks_u = [], []
            for i in range(c // GLA_SUB):
                lo, hi = i * GLA_SUB, (i + 1) * GLA_SUB
                ref_b = b[lo:lo + 1, :]
                ks_u.append((k * jnp.exp(jnp.where(row < hi, ref_b - b, 0.0))).astype(BF16))
                qs_u.append((q[lo:hi] * jnp.exp(b[lo:hi] - ref_b)).astype(BF16))
            q_sub.append(qs_u)
            k_sub.append(ks_u)
            vs_.append(v_ref[rows[u], :])

        o_intra, kv = [], []
        for u in units:
            a_h = []
            for h in range(nh):
                blocks = [jnp.where(sub_causal + i * GLA_SUB >= 0,
                                    _dot_nt(q_sub[u][i][:, hks[h]], k_sub[u][i][:, hks[h]]), 0.0)
                          for i in range(c // GLA_SUB)]
                a_h.append(jnp.concatenate(blocks, axis=0).astype(BF16))
            o_intra.append([_dot(a_h[h], vs_[u][:, hvs[h]]) for h in range(nh)])
            kv.append([_dot_tn(vs_[u][:, hvs[h]], k_state[u][:, hks[h]]) for h in range(nh)])

        st = [st_ref[h] for h in range(nh)]
        for u in units:
            outs = []
            for h in range(nh):
                o = o_intra[u][h] + _dot_nt(q_inter[u][:, hks[h]], st[h].astype(BF16))
                st[h] = st[h] * decay[u][:, hks[h]] + kv[u][h]
                outs.append(_rms_rows(o, gn_ref[...]))
            o_ref[rows[u], :] = (jnp.concatenate(outs, axis=1) * go_ref[rows[u], :].astype(F32)).astype(BF16)
        for h in range(nh):
            st_ref[h] = st[h]
        return 0

    lax.fori_loop(0, n_chunk // group_size, group, 0)


def _gla(gqk, gv, la, go, tri, gn, batch, seq):
    spec = lambda w: pl.BlockSpec((seq, w), lambda b: (b, 0))
    return pl.pallas_call(
        _gla_kernel,
        grid=(batch,),
        in_specs=[spec(512), spec(512), spec(256), spec(512), _const_spec((GLA_CHUNK, GLA_CHUNK)),
                  _const_spec((1, GLA_DV))],
        out_specs=spec(512),
        out_shape=jax.ShapeDtypeStruct((batch * seq, 512), BF16),
        scratch_shapes=[pltpu.VMEM((N_HEADS, GLA_DV, GLA_DK), F32)],
        compiler_params=_params(("arbitrary",)),
        name="gla",
    )(gqk, gv, la, go, tri, gn)


def _merge_kernel(x_ref, an_ref, om_ref, on_ref, og_ref, wm_ref, pm_ref, pn_ref, pg_ref, wo_ref, o_ref):
    x = x_ref[...]
    h = _rms_rows(x, an_ref[...]).astype(BF16)
    gates = [_dot(h, wm_ref[:, i * D_MODEL:(i + 1) * D_MODEL]) for i in range(3)]
    branches = [_dot(o_b[...], p_b[...]) for o_b, p_b in ((om_ref, pm_ref), (on_ref, pn_ref), (og_ref, pg_ref))]
    gated = [jax.nn.sigmoid(g) * b for g, b in zip(gates, branches)]
    z = gated[0] + gated[1] + gated[2]
    o_ref[...] = x + _dot(z.astype(BF16), wo_ref[...])


def _merge(x, an, om, on, og, wm, pm, pn, pg, wo, tm=MERGE_ROWS):
    n = x.shape[0]
    row = lambda w: pl.BlockSpec((tm, w), lambda i: (i, 0))
    return pl.pallas_call(
        _merge_kernel,
        grid=(n // tm,),
        in_specs=[row(D_MODEL), _const_spec((1, D_MODEL)), row(256), row(256), row(512),
                  _weight_spec((D_MODEL, 3 * D_MODEL)), _weight_spec((256, D_MODEL)), _weight_spec((256, D_MODEL)),
                  _weight_spec((512, D_MODEL)), _weight_spec((D_MODEL, D_MODEL))],
        out_specs=row(D_MODEL),
        out_shape=jax.ShapeDtypeStruct((n, D_MODEL), F32),
        compiler_params=_params(("arbitrary",)),
        name="merge",
    )(x, an, om, on, og, wm, pm, pn, pg, wo)


FFN_CHUNK = 256


def _ffn_kernel(x_ref, fn_ref, wa_ref, wg_ref, cw_ref, cb_ref, wd_ref, o_ref, carry_ref, act_ref, *, tiles_per_seq):
    i = pl.program_id(0)
    tm = x_ref.shape[0]
    x = x_ref[...]
    h = _rms_rows(x, fn_ref[...]).astype(BF16)
    row = lax.broadcasted_iota(jnp.int32, (tm, FFN_CHUNK), 0)

    @pl.when((i % tiles_per_seq) == 0)
    def _():
        carry_ref[...] = jnp.zeros_like(carry_ref)

    for c in range(D_FF // FFN_CHUNK):
        cs = slice(c * FFN_CHUNK, (c + 1) * FFN_CHUNK)
        a = _dot(h, wa_ref[:, cs])
        g = _dot(h, wg_ref[:, cs])
        prev = carry_ref[:, cs]
        p1 = prev[7:8, :]
        p2 = prev[6:7, :]
        a1 = jnp.where(row == 0, p1, pltpu.roll(a, 1, axis=0))
        a2 = jnp.where(row == 0, p2, jnp.where(row == 1, p1, pltpu.roll(a, 2, axis=0)))
        carry_ref[:, cs] = a[tm - 8:tm, :]
        w = cw_ref[:, cs]
        conv = w[0:1, :] * a2 + w[1:2, :] * a1 + w[2:3, :] * a + cb_ref[:, cs]
        act_ref[:, cs] = (jax.nn.gelu(conv, approximate=True) * g).astype(BF16)
    o_ref[...] = x + _dot(act_ref[...], wd_ref[...])


def _ffn(x, fn, wa, wg, cw, cb, wd, seq, tm=FFN_ROWS):
    n = x.shape[0]
    row = pl.BlockSpec((tm, D_MODEL), lambda i: (i, 0))
    return pl.pallas_call(
        functools.partial(_ffn_kernel, tiles_per_seq=seq // tm),
        grid=(n // tm,),
        in_specs=[row, _const_spec((1, D_MODEL)), _weight_spec((D_MODEL, D_FF)), _weight_spec((D_MODEL, D_FF)),
                  _const_spec((8, D_FF)), _const_spec((1, D_FF)), _weight_spec((D_FF, D_MODEL))],
        out_specs=row,
        out_shape=jax.ShapeDtypeStruct((n, D_MODEL), F32),
        scratch_shapes=[pltpu.VMEM((8, D_FF), F32), pltpu.VMEM((tm, D_FF), BF16)],
        compiler_params=_params(("arbitrary",)),
        name="ffn",
    )(x, fn, wa, wg, cw, cb, wd)


def _rel_bucket(dist):
    n = jnp.maximum(dist, 0)
    max_exact = REL_BUCKETS // 2
    nf = jnp.maximum(n, 1).astype(F32)
    large = max_exact + (jnp.log(nf / max_exact) / math.log(REL_MAX_DIST / max_exact)
                         * (REL_BUCKETS - max_exact)).astype(jnp.int32)
    return jnp.where(n < max_exact, n, jnp.minimum(large, REL_BUCKETS - 1))


def _bias_tiles(rel_tab):
    t = ATT_TILE
    d0 = jnp.arange(t)[None, :] - jnp.arange(t)[:, None]
    bucket = jnp.stack([_rel_bucket(d0 + k * t) for k in range(3)])
    out = jnp.zeros((rel_tab.shape[1],) + bucket.shape, F32)
    for b in range(REL_BUCKETS):
        out = jnp.where(bucket[None] == b, rel_tab[b][:, None, None, None], out)
    return out


def _block_diag_mean():
    g = jnp.arange(256) // HEAD_DIM
    return jnp.where(g[:, None] == g[None, :], 1.0 / HEAD_DIM, 0.0).astype(BF16)


def _overlap_t(n_cmp_pad, n_cmp, n_blk):
    tok = jnp.arange(n_blk * NSA_SEL_LEN)
    starts = jnp.arange(n_cmp_pad) * NSA_CMP_STRIDE
    inside = (tok[None, :] >= starts[:, None]) & (tok[None, :] < starts[:, None] + NSA_CMP_LEN)
    m = inside.reshape(n_cmp_pad, n_blk, NSA_SEL_LEN).sum(-1).astype(F32) / NSA_CMP_LEN
    m = jnp.where(jnp.arange(n_cmp_pad)[:, None] < n_cmp, m, 0.0)
    return m.T.astype(BF16)


def _tile_gain(g, reps):
    return jnp.tile(g.astype(F32), reps)[None, :]


def kernel(x, rel_bias, attn_norm, w_in, moba_q_norm, moba_k_norm, nsa_q_norm, nsa_k_norm, cmp_pos_k, cmp_pos_v,
           cmp_k_w1, cmp_k_w2, cmp_v_w1, cmp_v_w2, gla_gate_w, gla_gate_b, gla_out_norm, w_branch_moba,
           w_branch_nsa, w_branch_gla, w_out, ffn_norm, w_up, conv_w, conv_b, w_down):
    batch, seq, _ = x.shape
    depth = w_in.shape[0]
    assert x.shape[2] == D_MODEL and w_in.shape[1:] == (D_MODEL, _W_END)
    assert MOBA_BLOCK == ATT_TILE and seq // MOBA_BLOCK <= 16
    assert seq % (ATT_TILES_PER_STEP * ATT_TILE) == 0 and seq % FFN_ROWS == 0
    assert (batch * seq) % PROJ_ROWS == 0 and (batch * seq) % MERGE_ROWS == 0
    assert (seq // GLA_CHUNK) % GLA_UNROLL == 0
    n_cmp = seq // NSA_CMP_STRIDE - NSA_CMP_LEN // NSA_CMP_STRIDE + 1
    n_cmp_pad = seq // NSA_CMP_STRIDE
    n_blk = seq // NSA_SEL_LEN

    tiles = _bias_tiles(rel_bias.astype(F32)) * LOG2_E
    key_le_query = jnp.arange(ATT_TILE)[:, None] <= jnp.arange(ATT_TILE)[None, :]
    own = jnp.where(key_le_query, tiles[:, 0], NEG_INF)
    band = jnp.where(key_le_query, NEG_INF, tiles[:, 2])
    near, far = tiles[:, 1], tiles[:, 2, :8]
    bias_moba = jnp.concatenate([near[:N_HEADS], own[:N_HEADS]], axis=1)
    far_moba = far[:N_HEADS]
    heads_on_lanes = lambda a: jnp.concatenate(list(a[N_HEADS:]), axis=1)
    bias_nsa = jnp.concatenate([heads_on_lanes(band), heads_on_lanes(near), heads_on_lanes(own)], axis=0)
    far_nsa = heads_on_lanes(far)
    bd = _block_diag_mean()
    ovt = _overlap_t(n_cmp_pad, n_cmp, n_blk)
    tri = (jnp.arange(GLA_CHUNK)[:, None] >= jnp.arange(GLA_CHUNK)[None, :]).astype(BF16)
    ones64 = jnp.ones((HEAD_DIM,), F32)

    xf = x.reshape(batch * seq, D_MODEL)
    for l in range(depth):
        w_att, w_gla, w_go, w_glr, w_ng, wmerge = _wsplit(w_in, l)
        w_misc = jnp.pad(jnp.concatenate([w_glr, w_ng], axis=1), ((0, 0), (0, LANE - GLA_GATE_RANK - 3 * N_HEADS)))
        gw = jnp.pad(gla_gate_w[l], ((0, LANE - GLA_GATE_RANK), (0, 0))).astype(BF16)

        mqt, mk_a, mvt, nqt, kvc, nkv_a, nkvt, gqk, gv_a, la, go, ngt = _proj(
            xf, attn_norm[l][None, :], w_att, w_gla, w_go, w_misc, bd,
            _tile_gain(moba_q_norm[l], 4), _tile_gain(moba_k_norm[l], 4), _tile_gain(nsa_q_norm[l], 4),
            jnp.concatenate([nsa_k_norm[l, 1], ones64])[None, :], jnp.concatenate([nsa_k_norm[l, 2], ones64])[None, :],
            gw, gla_gate_b[l][None, :], seq)

        o_moba = _moba(mqt, mk_a, mvt, bias_moba, far_moba, batch, seq)

        half = NSA_CMP_STRIDE * HEAD_DIM
        zero = jnp.zeros((NSA_CMP_STRIDE, HEAD_DIM, HEAD_DIM), F32)

        def w1_part(part):
            wk = cmp_k_w1[l][part * half:(part + 1) * half].reshape(NSA_CMP_STRIDE, HEAD_DIM, HEAD_DIM)
            wv = cmp_v_w1[l][part * half:(part + 1) * half].reshape(NSA_CMP_STRIDE, HEAD_DIM, HEAD_DIM)
            top = jnp.concatenate([wk, zero], axis=2)
            bot = jnp.concatenate([zero, wv], axis=2)
            return jnp.concatenate([top, bot], axis=1).reshape(NSA_CMP_STRIDE * LANE, LANE).astype(BF16)

        def pos_part(part):
            pk = cmp_pos_k[l][part * NSA_CMP_STRIDE:(part + 1) * NSA_CMP_STRIDE]
            pv = cmp_pos_v[l][part * NSA_CMP_STRIDE:(part + 1) * NSA_CMP_STRIDE]
            return jnp.concatenate([pk, pv], axis=1).reshape(1, NSA_CMP_STRIDE * LANE).astype(F32)

        z64 = jnp.zeros((HEAD_DIM, HEAD_DIM), F32)
        w2 = jnp.concatenate([jnp.concatenate([cmp_k_w2[l], z64], axis=1),
                              jnp.concatenate([z64, cmp_v_w2[l]], axis=1)], axis=0).astype(BF16)
        kcv, kcvt = _compress(kvc.reshape(batch, n_cmp_pad, NSA_CMP_STRIDE * LANE), pos_part(0), pos_part(1),
                              w1_part(0), w1_part(1), w2, jnp.concatenate([nsa_k_norm[l, 0], ones64])[None, :])

        o_nsa = _nsa(nqt, kcv, kcvt, nkv_a, nkvt, ngt, bias_nsa, far_nsa, ovt, batch, seq)
        o_gla = _gla(gqk, gv_a, la, go, tri, gla_out_norm[l][None, :].astype(F32), batch, seq)

        xf = _merge(xf, attn_norm[l][None, :], o_moba, o_nsa, o_gla, wmerge.astype(BF16),
                    w_branch_moba[l].astype(BF16), w_branch_nsa[l].astype(BF16), w_branch_gla[l].astype(BF16),
                    w_out[l].astype(BF16))

        cw = jnp.pad(conv_w[l], ((0, 8 - conv_w.shape[1]), (0, 0)))
        xf = _ffn(xf, ffn_norm[l][None, :], w_up[l][:, :D_FF].astype(BF16), w_up[l][:, D_FF:].astype(BF16),
                  cw, conv_b[l][None, :], w_down[l].astype(BF16), seq)
    return xf.reshape(batch, seq, D_MODEL)
```
